```python
import math
import jax, jax.numpy as jnp
from jax import lax
import numpy as np

D_MODEL = 1024
BATCH = 2
SEQ = 8192
DEPTH = 2

GRID_W = 64
CTX_LEN = 256
FOURIER_W = D_MODEL // 4
FOURIER_GC = 64
FOURIER_GROUPS = FOURIER_W // FOURIER_GC
DIFF_HD = 64
DIFF_VD = 2 * DIFF_HD
DIFF_HEADS = (D_MODEL - FOURIER_W) // DIFF_VD
QK_W = DIFF_HEADS * 2 * DIFF_HD
V_W = DIFF_HEADS * DIFF_VD
EVEN_IN_W = FOURIER_W + 2 * QK_W + V_W
DIFF_SCALE = DIFF_HD ** -0.5
Q_BLOCK = 128
ROPE_BASE = 10000.0
ROPE_FREQS = DIFF_HD // 4
S5_W = D_MODEL // 2
S5_GC = 16
S5_GROUPS = S5_W // S5_GC
S5_STATE = 64
GMLP_W = D_MODEL // 2
GMLP_GC = 128
GMLP_GROUPS = GMLP_W // GMLP_GC
CHUNK = 128
ODD_IN_W = S5_W + 2 * GMLP_W
MIX_W = D_MODEL
FFN_W = 4 * D_MODEL
LN_EPS = 1e-5
ALPHA = (2 * DEPTH) ** 0.25
BETA = (8 * DEPTH) ** -0.25
N_EVEN = (DEPTH + 1) // 2
N_ODD = DEPTH // 2

kernel_name = 'hybrid_fourier_diffattn_s5_gmlp_diffusion_trunk'


def _layernorm(x):
    xf = x.astype(jnp.float32)
    xc = xf - jnp.mean(xf, -1, keepdims=True)
    var = jnp.mean(xc * xc, -1, keepdims=True)
    return (xc * lax.rsqrt(var + LN_EPS)).astype(x.dtype)


def _modulate(x, shift, scale):
    return _layernorm(x) * (1 + scale) + shift


def _post_norm(x, y, gate, g, b):
    return _layernorm(ALPHA * x + gate * y) * g + b


def _axial_rope_tables(n):
    rows = n // GRID_W
    row = jnp.repeat(jnp.arange(rows, dtype=jnp.float32), GRID_W)
    col = jnp.tile(jnp.arange(GRID_W, dtype=jnp.float32), rows)
    inv = jnp.power(ROPE_BASE, -jnp.arange(ROPE_FREQS, dtype=jnp.float32) / ROPE_FREQS)
    ang = jnp.stack([row[:, None] * inv, col[:, None] * inv], axis=1)
    return jnp.cos(ang), jnp.sin(ang)


def _apply_rope(t, cos, sin):
    sh = t.shape
    tr = t.reshape(sh[:-1] + (2, 2, ROPE_FREQS))
    c = cos[None, :, None, None].astype(t.dtype)
    s = sin[None, :, None, None].astype(t.dtype)
    t1 = tr[..., 0, :]
    t2 = tr[..., 1, :]
    out = jnp.stack([t1 * c - t2 * s, t2 * c + t1 * s], axis=-2)
    return out.reshape(sh)


def _fourier_mix(f):
    B, L, _ = f.shape
    fg = f.astype(jnp.float32).reshape(B, L, FOURIER_GROUPS, FOURIER_GC)
    out = jnp.fft.fft2(fg, axes=(1, 3), norm='ortho').real
    return out.reshape(B, L, FOURIER_W).astype(f.dtype)


def _diff_attend(q, k_all, v_all, lam):
    s = jnp.einsum('bqhcd,bkhcd->bhcqk', q, k_all).astype(jnp.float32) * DIFF_SCALE
    p = jax.nn.softmax(s, axis=-1)
    a = p[:, :, 0] - lam * p[:, :, 1]
    return jnp.einsum('bhqk,bkhe->bqhe', a.astype(v_all.dtype), v_all)


def _diff_post(o, subln_g, lam_init):
    of = o.astype(jnp.float32)
    of = of * lax.rsqrt(jnp.mean(of * of, -1, keepdims=True) + LN_EPS)
    out = of.astype(o.dtype) * subln_g * (1 - lam_init)
    return out.reshape(o.shape[0], o.shape[1], V_W)


def _even_mixer(h, hc, w_in, lam_q1, lam_k1, lam_q2, lam_k2, subln_g, cos, sin, layer_idx, need_ctx):
    B, L, _ = h.shape
    Lc = hc.shape[1]
    kv0 = FOURIER_W + QK_W
    z = h @ w_in
    f = z[..., :FOURIER_W]
    q = z[..., FOURIER_W:kv0].reshape(B, L, DIFF_HEADS, 2, DIFF_HD)
    k = z[..., kv0:kv0 + QK_W].reshape(B, L, DIFF_HEADS, 2, DIFF_HD)
    v = z[..., kv0 + QK_W:].reshape(B, L, DIFF_HEADS, DIFF_VD)
    if need_ctx:
        zc = hc @ w_in
        fc = zc[..., :FOURIER_W]
        qc = zc[..., FOURIER_W:kv0].reshape(B, Lc, DIFF_HEADS, 2, DIFF_HD)
        zc_kv = zc[..., kv0:]
    else:
        zc_kv = hc @ w_in[:, kv0:]
    kc = zc_kv[..., :QK_W].reshape(B, Lc, DIFF_HEADS, 2, DIFF_HD)
    vc = zc_kv[..., QK_W:].reshape(B, Lc, DIFF_HEADS, DIFF_VD)
    lam_init = 0.8 - 0.6 * math.exp(-0.3 * layer_idx)
    f32 = jnp.float32
    lam = (jnp.exp(jnp.sum(lam_q1.astype(f32) * lam_k1.astype(f32)))
           - jnp.exp(jnp.sum(lam_q2.astype(f32) * lam_k2.astype(f32))) + lam_init)
    q = _apply_rope(q, cos, sin)
    k = _apply_rope(k, cos, sin)
    k_all = jnp.concatenate([kc, k], axis=1)
    v_all = jnp.concatenate([vc, v], axis=1)
    nb = L // Q_BLOCK
    qb = jnp.moveaxis(q.reshape(B, nb, Q_BLOCK, DIFF_HEADS, 2, DIFF_HD), 1, 0)
    o = lax.map(lambda qi: _diff_attend(qi, k_all, v_all, lam), qb)
    o = jnp.moveaxis(o, 0, 1).reshape(B, L, DIFF_HEADS, DIFF_VD)
    y = jnp.concatenate([_fourier_mix(f), _diff_post(o, subln_g, lam_init)], axis=-1)
    if need_ctx:
        oc = _diff_attend(qc, kc, vc, lam)
        yc = jnp.concatenate([_fourier_mix(fc), _diff_post(oc, subln_g, lam_init)], axis=-1)
    else:
        yc = None
    return y, yc


def _lin_recur(left, right):
    a_l, b_l = left
    a_r, b_r = right
    return a_r * a_l, a_r * b_l + b_r


def _s5_drive(u, bbar):
    B, L, _ = u.shape
    ug = u.astype(jnp.float32).reshape(B, L, S5_GROUPS, S5_GC).astype(jnp.complex64)
    return jnp.einsum('bsgi,gpi->bsgp', ug, bbar)


def _s5_scan(bu, lam_bar, h0, reverse):
    if h0 is not None:
        idx = -1 if reverse else 0
        bu = bu.at[:, idx].add(lam_bar * h0)
    a = jnp.broadcast_to(lam_bar, bu.shape)
    _, hs = lax.associative_scan(_lin_recur, (a, bu), reverse=reverse, axis=1)
    return hs


def _s5_mixer(s, sc, lam_re, lam_im, log_dt, b_re, b_im, c_re, c_im, d_skip, w_glu, b_glu, need_ctx):
    f32 = jnp.float32
    B, L, _ = s.shape
    Lc = sc.shape[1]
    lam = lax.complex(lam_re.astype(f32), lam_im.astype(f32))
    dt = jnp.exp(log_dt.astype(f32))[..., None]
    lam_bar = jnp.exp(lam * dt)
    bbar = ((lam_bar - 1) / lam)[..., None] * lax.complex(b_re.astype(f32), b_im.astype(f32))
    cmat = lax.complex(c_re.astype(f32), c_im.astype(f32))
    d = d_skip.astype(f32)
    y = d * s.astype(f32)
    yc = d * sc.astype(f32) if need_ctx else None
    for r, rev in enumerate((False, True)):
        hs_c = _s5_scan(_s5_drive(sc, bbar[r]), lam_bar[r], None, rev)
        h0 = hs_c[:, 0] if rev else hs_c[:, -1]
        hs = _s5_scan(_s5_drive(s, bbar[r]), lam_bar[r], h0, rev)
        y = y + jnp.einsum('bsgp,gip->bsgi', hs, cmat[r]).real.reshape(B, L, S5_W)
        if need_ctx:
            yc = yc + jnp.einsum('bsgp,gip->bsgi', hs_c, cmat[r]).real.reshape(B, Lc, S5_W)
    wg = w_glu.astype(f32)
    bg = b_glu.astype(f32)

    def glu(t):
        g = jax.nn.gelu(t)
        return (g * jax.nn.sigmoid(g @ wg + bg)).astype(s.dtype)

    return glu(y), (glu(yc) if need_ctx else None)


def _chunk_gmlp(u, v, w_sp, b_sp):
    B, L, _ = u.shape
    n = L // CHUNK
    vg = _layernorm(v.reshape(B, n, CHUNK, GMLP_GROUPS, GMLP_GC))
    sp = jnp.einsum('gpq,bnqgc->bnpgc', w_sp, vg) + jnp.swapaxes(b_sp, 0, 1)[:, :, None]
    return (u.reshape(B, n, CHUNK, GMLP_GROUPS, GMLP_GC) * sp).reshape(B, L, GMLP_W)


def _odd_mixer(h, hc, w_in, lam_re, lam_im, log_dt, b_re, b_im, c_re, c_im, d_skip, w_glu, b_glu,
               w_sp, b_sp, need_ctx):
    z = h @ w_in
    s = z[..., :S5_W]
    u = z[..., S5_W:S5_W + GMLP_W]
    v = z[..., S5_W + GMLP_W:]
    if need_ctx:
        zc = hc @ w_in
        sc = zc[..., :S5_W]
        uc = zc[..., S5_W:S5_W + GMLP_W]
        vc = zc[..., S5_W + GMLP_W:]
    else:
        sc = hc @ w_in[:, :S5_W]
    ys, ysc = _s5_mixer(s, sc, lam_re, lam_im, log_dt, b_re, b_im, c_re, c_im, d_skip, w_glu, b_glu, need_ctx)
    y = jnp.concatenate([ys, _chunk_gmlp(u, v, w_sp, b_sp)], axis=-1)
    yc = jnp.concatenate([ysc, _chunk_gmlp(uc, vc, w_sp, b_sp)], axis=-1) if need_ctx else None
    return y, yc


def _sq_relu_mlp(h, w1, b1, w2, b2):
    a = jax.nn.relu(h @ w1 + b1)
    return (a * a) @ w2 + b2


def setup_inputs(seed: int = 0) -> dict:
    key = jax.random.key(seed)
    ks = iter(jax.random.split(key, 40))
    f32 = jnp.float32
    D = D_MODEL

    def nrm(shape, s):
        return jax.random.normal(next(ks), shape, f32) * s

    lam_im_base = jnp.broadcast_to(jnp.pi * jnp.arange(S5_STATE, dtype=f32), (N_ODD, 2, S5_GROUPS, S5_STATE))
    return {
        'x': nrm((BATCH, SEQ, D), 1.0),
        'c': nrm((BATCH, D), 1.0),
        'ctx': nrm((BATCH, CTX_LEN, D), 1.0),
        'c_ctx': nrm((D,), 1.0),
        'w_mod': nrm((DEPTH, D, 6 * D), D ** -0.5),
        'b_mod': nrm((DEPTH, 6 * D), 0.01),
        'w_out': nrm((DEPTH, MIX_W, D), BETA * MIX_W ** -0.5),
        'b_out': nrm((DEPTH, D), 0.01),
        'ln_mix_g': 1.0 + nrm((DEPTH, D), 0.01),
        'ln_mix_b': nrm((DEPTH, D), 0.01),
        'w_ffn1': nrm((DEPTH, D, FFN_W), D ** -0.5),
        'b_ffn1': nrm((DEPTH, FFN_W), 0.01),
        'w_ffn2': nrm((DEPTH, FFN_W, D), BETA * FFN_W ** -0.5),
        'b_ffn2': nrm((DEPTH, D), 0.01),
        'ln_ffn_g': 1.0 + nrm((DEPTH, D), 0.01),
        'ln_ffn_b': nrm((DEPTH, D), 0.01),
        'w_in_ab': nrm((N_EVEN, D, EVEN_IN_W), D ** -0.5),
        'lam_q1': nrm((N_EVEN, DIFF_HD), 0.1),
        'lam_k1': nrm((N_EVEN, DIFF_HD), 0.1),
        'lam_q2': nrm((N_EVEN, DIFF_HD), 0.1),
        'lam_k2': nrm((N_EVEN, DIFF_HD), 0.1),
        'subln_g': 1.0 + nrm((N_EVEN, DIFF_VD), 0.01),
        'w_in_cd': nrm((N_ODD, D, ODD_IN_W), D ** -0.5),
        's5_lam_re': -0.5 + nrm((N_ODD, 2, S5_GROUPS, S5_STATE), 0.01),
        's5_lam_im': lam_im_base + nrm((N_ODD, 2, S5_GROUPS, S5_STATE), 0.01),
        's5_log_dt': jax.random.uniform(next(ks), (N_ODD, 2, S5_GROUPS), f32, math.log(1e-3), math.log(1e-1)),
        's5_b_re': nrm((N_ODD, 2, S5_GROUPS, S5_STATE, S5_GC), (2 * S5_GC) ** -0.5),
        's5_b_im': nrm((N_ODD, 2, S5_GROUPS, S5_STATE, S5_GC), (2 * S5_GC) ** -0.5),
        's5_c_re': nrm((N_ODD, 2, S5_GROUPS, S5_GC, S5_STATE), S5_STATE ** -0.5),
        's5_c_im': nrm((N_ODD, 2, S5_GROUPS, S5_GC, S5_STATE), S5_STATE ** -0.5),
        's5_d': nrm((N_ODD, S5_W), 1.0),
        'w_glu': nrm((N_ODD, S5_W, S5_W), S5_W ** -0.5),
        'b_glu': nrm((N_ODD, S5_W), 0.01),
        'w_sp': nrm((N_ODD, GMLP_GROUPS, CHUNK, CHUNK), CHUNK ** -0.5),
        'b_sp': 1.0 + nrm((N_ODD, GMLP_GROUPS, CHUNK), 0.01),
    }


def reference(x, c, ctx, c_ctx, w_mod, b_mod, w_out, b_out, ln_mix_g, ln_mix_b, w_ffn1, b_ffn1, w_ffn2,
              b_ffn2, ln_ffn_g, ln_ffn_b, w_in_ab, lam_q1, lam_k1, lam_q2, lam_k2, subln_g, w_in_cd,
              s5_lam_re, s5_lam_im, s5_log_dt, s5_b_re, s5_b_im, s5_c_re, s5_c_im, s5_d, w_glu, b_glu,
              w_sp, b_sp):
    L = x.shape[1]
    cos, sin = _axial_rope_tables(L)
    xc = ctx
    for l in range(DEPTH):
        last = l == DEPTH - 1
        e = l // 2
        mod = jax.nn.silu(c) @ w_mod[l] + b_mod[l]
        modc = jax.nn.silu(c_ctx) @ w_mod[l] + b_mod[l]
        sh1, sc1, g1, sh2, sc2, g2 = jnp.split(mod[:, None, :], 6, axis=-1)
        shc1, scc1, gc1, shc2, scc2, gc2 = jnp.split(modc, 6, axis=-1)
        h = _modulate(x, sh1, sc1)
        hc = _modulate(xc, shc1, scc1)
        if l % 2 == 0:
            y, yc = _even_mixer(h, hc, w_in_ab[e], lam_q1[e], lam_k1[e], lam_q2[e], lam_k2[e], subln_g[e],
                                cos, sin, l, not last)
        else:
            y, yc = _odd_mixer(h, hc, w_in_cd[e], s5_lam_re[e], s5_lam_im[e], s5_log_dt[e], s5_b_re[e],
                               s5_b_im[e], s5_c_re[e], s5_c_im[e], s5_d[e], w_glu[e], b_glu[e],
                               w_sp[e], b_sp[e], not last)
        x = _post_norm(x, y @ w_out[l] + b_out[l], g1, ln_mix_g[l], ln_mix_b[l])
        x = _post_norm(x, _sq_relu_mlp(_modulate(x, sh2, sc2), w_ffn1[l], b_ffn1[l], w_ffn2[l], b_ffn2[l]),
                       g2, ln_ffn_g[l], ln_ffn_b[l])
        if not last:
            xc = _post_norm(xc, yc @ w_out[l] + b_out[l], gc1, ln_mix_g[l], ln_mix_b[l])
            xc = _post_norm(xc, _sq_relu_mlp(_modulate(xc, shc2, scc2), w_ffn1[l], b_ffn1[l], w_ffn2[l],
                                             b_ffn2[l]), gc2, ln_ffn_g[l], ln_ffn_b[l])
    return x
```

```python
import functools
import math

import numpy as np
import jax
import jax.numpy as jnp
from jax import lax
from jax.experimental import pallas as pl
from jax.experimental.pallas import tpu as pltpu

D_MODEL = 1024
DEPTH = 2
GRID_W = 64
FOURIER_W = D_MODEL // 4
FOURIER_GC = 64
DIFF_HD = 64
DIFF_VD = 2 * DIFF_HD
DIFF_HEADS = (D_MODEL - FOURIER_W) // DIFF_VD
QK_W = DIFF_HEADS * 2 * DIFF_HD
V_W = DIFF_HEADS * DIFF_VD
EVEN_IN_W = FOURIER_W + 2 * QK_W + V_W
DIFF_SCALE = DIFF_HD ** -0.5
ROPE_BASE = 10000.0
ROPE_FREQS = DIFF_HD // 4
S5_W = D_MODEL // 2
S5_GC = 16
S5_GROUPS = S5_W // S5_GC
S5_STATE = 64
GMLP_W = D_MODEL // 2
GMLP_GC = 128
GMLP_GROUPS = GMLP_W // GMLP_GC
CHUNK = 128
ODD_IN_W = S5_W + 2 * GMLP_W
FFN_W = 4 * D_MODEL
LN_EPS = 1e-5
ALPHA = (2 * DEPTH) ** 0.25

F32 = jnp.float32
BF16 = jnp.bfloat16
HIGHEST = lax.Precision.HIGHEST

LANES = 128
MOD_ROWS = 8
ROW_TILE = 512
ATTN_TQ = 256
ATTN_TK = ROW_TILE
S5_T = 16
S5_PAIRS = S5_GROUPS // 2
S5_SCAN_LANES = 512
FFN_CHUNK = 1024
VMEM_LIMIT = 56 * 2 ** 20
Q_SCALE = DIFF_SCALE * math.log2(math.e)


def _cparams(n_axes, vmem=VMEM_LIMIT):
    return pltpu.CompilerParams(dimension_semantics=("arbitrary",) * n_axes, vmem_limit_bytes=vmem)


def _const_spec(shape):
    zeros = (0,) * len(shape)
    return pl.BlockSpec(shape, lambda *_: zeros, pipeline_mode=pl.Buffered(1))


def _ln(x):
    xc = x - jnp.mean(x, -1, keepdims=True)
    var = jnp.mean(xc * xc, -1, keepdims=True)
    return xc * lax.rsqrt(var + LN_EPS)


def _mod_spec(row_fn, which):
    return pl.BlockSpec((None, 1, D_MODEL), lambda i: (row_fn(i) * 6 + which, 0, 0))


def _mod_kernel(c_ref, w_ref, b_ref, o_ref):
    c = c_ref[...]
    a = c * jax.nn.sigmoid(c)
    o_ref[...] = jnp.dot(a, w_ref[...], preferred_element_type=F32, precision=HIGHEST) + b_ref[...]


def _modulation(c, c_ctx, w_mod, b_mod):
    batch = c.shape[0]
    rows = jnp.concatenate([c, c_ctx[None], jnp.zeros((MOD_ROWS - batch - 1, D_MODEL), F32)], axis=0)
    out = pl.pallas_call(
        _mod_kernel,
        grid=(DEPTH, 6),
        in_specs=[
            pl.BlockSpec((MOD_ROWS, D_MODEL), lambda l, j: (0, 0)),
            pl.BlockSpec((None, D_MODEL, D_MODEL), lambda l, j: (l, 0, j)),
            pl.BlockSpec((None, 1, D_MODEL), lambda l, j: (l, 0, j)),
        ],
        out_specs=pl.BlockSpec((None, MOD_ROWS, D_MODEL), lambda l, j: (l, 0, j)),
        out_shape=jax.ShapeDtypeStruct((DEPTH, MOD_ROWS, 6 * D_MODEL), F32),
        compiler_params=_cparams(2),
        name="modulation",
    )(rows, w_mod, b_mod.reshape(DEPTH, 1, 6 * D_MODEL))
    return out.reshape(DEPTH, MOD_ROWS * 6, 1, D_MODEL)


def _inproj_even_kernel(*refs, rope):
    if rope:
        x_ref, sh_ref, sc_ref, w_ref, cos_ref, sin_ref, f_ref, q_ref, k_ref, vt_ref = refs
    else:
        x_ref, sh_ref, sc_ref, w_ref, f_ref, q_ref, k_ref, vt_ref = refs
    h = (_ln(x_ref[...]) * (1.0 + sc_ref[...]) + sh_ref[...]).astype(BF16)
    f_ref[...] = jnp.dot(h, w_ref[:, :FOURIER_W], preferred_element_type=F32)
    width = 2 * LANES
    if rope:
        cos = cos_ref[...]
        sin = sin_ref[...]
        lane = lax.broadcasted_iota(jnp.int32, cos.shape, 1)
        first_half = (lane & ROPE_FREQS) == 0
    for j in range(QK_W // width):
        for base, o_ref, scale in ((FOURIER_W, q_ref, Q_SCALE), (FOURIER_W + QK_W, k_ref, None)):
            t = jnp.dot(h, w_ref[:, base + j * width:base + (j + 1) * width], preferred_element_type=F32)
            if rope:
                partner = jnp.where(first_half, pltpu.roll(t, width - ROPE_FREQS, 1),
                                    pltpu.roll(t, ROPE_FREQS, 1))
                t = t * cos + partner * sin
            if scale is not None:
                t = t * scale
            o_ref[:, j * width:(j + 1) * width] = t.astype(BF16)
    v0 = FOURIER_W + 2 * QK_W
    for j in range(V_W // width):
        v = jnp.dot(h, w_ref[:, v0 + j * width:v0 + (j + 1) * width], preferred_element_type=F32)
        vt_ref[j * width:(j + 1) * width, :] = v.T.astype(BF16)


def _inproj_even(x2d, modl, w_bf, rope_tabs, *, batch, tiles_per_batch, tm, row_fn):
    rows = x2d.shape[0]
    rope = rope_tabs is not None
    in_specs = [
        pl.BlockSpec((tm, D_MODEL), lambda i: (i, 0)),
        _mod_spec(row_fn, 0),
        _mod_spec(row_fn, 1),
        _const_spec((D_MODEL, EVEN_IN_W)),
    ]
    args = [x2d, modl, modl, w_bf]
    if rope:
        in_specs += [pl.BlockSpec((tm, 2 * LANES), lambda i: (i % tiles_per_batch, 0))] * 2
        args += list(rope_tabs)
    return pl.pallas_call(
        functools.partial(_inproj_even_kernel, rope=rope),
        grid=(rows // tm,),
        in_specs=in_specs,
        out_specs=[
            pl.BlockSpec((tm, FOURIER_W), lambda i: (i, 0)),
            pl.BlockSpec((tm, QK_W), lambda i: (i, 0)),
            pl.BlockSpec((tm, QK_W), lambda i: (i, 0)),
            pl.BlockSpec((None, None, V_W, tm), lambda i: (i // tiles_per_batch, i % tiles_per_batch, 0, 0)),
        ],
        out_shape=[
            jax.ShapeDtypeStruct((rows, FOURIER_W), F32),
            jax.ShapeDtypeStruct((rows, QK_W), BF16),
            jax.ShapeDtypeStruct((rows, QK_W), BF16),
            jax.ShapeDtypeStruct((batch, tiles_per_batch, V_W, tm), BF16),
        ],
        compiler_params=_cparams(1),
        name="inproj_even_rope" if rope else "inproj_even",
    )(*args)


def _rope_tables(n):
    rows = n // GRID_W
    row = jnp.repeat(jnp.arange(rows, dtype=F32), GRID_W)
    col = jnp.tile(jnp.arange(GRID_W, dtype=F32), rows)
    inv = jnp.power(ROPE_BASE, -jnp.arange(ROPE_FREQS, dtype=F32) / ROPE_FREQS)
    ang = jnp.stack([row[:, None] * inv, col[:, None] * inv], axis=1)
    cos, sin = jnp.cos(ang), jnp.sin(ang)
    cos_map = jnp.concatenate([cos, cos], axis=-1).reshape(n, 4 * ROPE_FREQS)
    sin_map = jnp.concatenate([-sin, sin], axis=-1).reshape(n, 4 * ROPE_FREQS)
    reps = 2 * LANES // (4 * ROPE_FREQS)
    return jnp.tile(cos_map, (1, reps)), jnp.tile(sin_map, (1, reps))


def _attn_kernel(*refs, has_latent, lam_init):
    if has_latent:
        (lq1, lk1, lq2, lk2, g_ref, q_ref, kc_ref, vct_ref, k_ref, vt_ref, o_ref, acc_ref) = refs
    else:
        (lq1, lk1, lq2, lk2, g_ref, q_ref, kc_ref, vct_ref, o_ref, acc_ref) = refs
    q = q_ref[...]
    lane = lax.broadcasted_iota(jnp.int32, q.shape, 1)
    zero = jnp.zeros_like(q)
    q_maps = (jnp.where(lane < DIFF_HD, q, zero), jnp.where(lane >= DIFF_HD, q, zero))
    nt_dims = (((1,), (1,)), ((), ()))

    def scores(kb, mp):
        return lax.dot_general(kb, q_maps[mp], nt_dims, preferred_element_type=F32)

    kb = kc_ref[...]
    vtb = vct_ref[...]
    stats = []
    for mp in range(2):
        s = scores(kb, mp)
        m = jnp.max(s, axis=0, keepdims=True)
        p = jnp.exp2(s - m)
        stats += [m, jnp.sum(p, axis=0, keepdims=True)]
        acc_ref[mp] = jnp.dot(vtb, p.astype(BF16), preferred_element_type=F32)

    if has_latent:
        tk = vt_ref.shape[2]

        def body(c, carry):
            kb = k_ref[pl.ds(pl.multiple_of(c * tk, tk), tk), :]
            vtb = vt_ref[c]
            out = []
            for mp in range(2):
                m_prev, l_prev = carry[2 * mp], carry[2 * mp + 1]
                s = scores(kb, mp)
                m = jnp.maximum(m_prev, jnp.max(s, axis=0, keepdims=True))
                alpha = jnp.exp2(m_prev - m)
                p = jnp.exp2(s - m)
                out += [m, alpha * l_prev + jnp.sum(p, axis=0, keepdims=True)]
                acc_ref[mp] = acc_ref[mp] * alpha + jnp.dot(vtb, p.astype(BF16), preferred_element_type=F32)
            return tuple(out)

        stats = lax.fori_loop(0, vt_ref.shape[0], body, tuple(stats))

    lam = (jnp.exp(jnp.sum(lq1[...] * lk1[...], keepdims=True))
           - jnp.exp(jnp.sum(lq2[...] * lk2[...], keepdims=True)) + lam_init)
    o = acc_ref[0] / stats[1] - lam * (acc_ref[1] / stats[3])
    o = o * lax.rsqrt(jnp.mean(o * o, axis=0, keepdims=True) + LN_EPS)
    o_ref[...] = o.T * (g_ref[...] * (1.0 - lam_init))


def _diff_attention(q, kc, vct, k, vt, lam_vecs, subln_g, lam_init, *, batch, tq):
    rows = q.shape[0]
    nq = rows // batch // tq
    lc = kc.shape[0] // batch
    has_latent = k is not None
    vec_spec = pl.BlockSpec((1, DIFF_HD), lambda b, h, i: (0, 0))
    in_specs = [vec_spec] * 4 + [
        pl.BlockSpec((1, DIFF_VD), lambda b, h, i: (0, 0)),
        pl.BlockSpec((tq, DIFF_VD), lambda b, h, i: (b * nq + i, h)),
        pl.BlockSpec((lc, DIFF_VD), lambda b, h, i: (b, h)),
        pl.BlockSpec((None, None, DIFF_VD, lc), lambda b, h, i: (b, 0, h, 0)),
    ]
    args = [v.reshape(1, DIFF_HD) for v in lam_vecs] + [subln_g.reshape(1, DIFF_VD), q, kc, vct]
    if has_latent:
        n_k = k.shape[0] // batch
        in_specs += [
            pl.BlockSpec((n_k, DIFF_VD), lambda b, h, i: (b, h)),
            pl.BlockSpec((None, vt.shape[1], DIFF_VD, vt.shape[3]), lambda b, h, i: (b, 0, h, 0)),
        ]
        args += [k, vt]
    return pl.pallas_call(
        functools.partial(_attn_kernel, has_latent=has_latent, lam_init=lam_init),
        grid=(batch, DIFF_HEADS, nq),
        in_specs=in_specs,
        out_specs=pl.BlockSpec((tq, DIFF_VD), lambda b, h, i: (b * nq + i, h)),
        out_shape=jax.ShapeDtypeStruct((rows, V_W), F32),
        scratch_shapes=[pltpu.VMEM((2, DIFF_VD, tq), F32)],
        compiler_params=_cparams(3),
        name="diff_attention" if has_latent else "diff_attention_ctx",
    )(*args)


def _dft_cos_sin(n):
    idx = np.outer(np.arange(n), np.arange(n)) % n
    ang = 2.0 * np.pi * idx / n
    return np.cos(ang), np.sin(ang)


def _channel_dft():
    c, s = _dft_cos_sin(FOURIER_GC)
    eye = np.eye(LANES // FOURIER_GC)
    return jnp.asarray(np.kron(eye, c), F32), jnp.asarray(np.kron(eye, s), F32)


def _fourier_dense_kernel(x_ref, cl_ref, sl_ref, bdc_ref, bds_ref, o_ref, *, norm):
    x = x_ref[...]
    pr = jnp.dot(cl_ref[...], x, preferred_element_type=F32, precision=HIGHEST)
    pi = -jnp.dot(sl_ref[...], x, preferred_element_type=F32, precision=HIGHEST)
    out = (jnp.dot(pr, bdc_ref[...], preferred_element_type=F32, precision=HIGHEST)
           + jnp.dot(pi, bds_ref[...], preferred_element_type=F32, precision=HIGHEST))
    o_ref[...] = out * norm


def _fourier_fft_kernel(x_ref, f1_ref, twc_ref, tws_ref, f3_ref, bdc_ref, bds_ref, o_ref, ur_ref, ui_ref,
                        *, n1, norm):
    n2 = LANES

    def stage1(i, carry):
        xs = x_ref[pl.ds(i, n2, stride=n1), :]
        t = jnp.dot(f1_ref[...], xs, preferred_element_type=F32, precision=HIGHEST)
        tr, ti = t[:n2], t[n2:]
        r0 = pl.multiple_of(i * n2, n2)
        c = twc_ref[pl.ds(r0, n2), :]
        s = tws_ref[pl.ds(r0, n2), :]
        ur_ref[pl.ds(r0, n2), :] = tr * c + ti * s
        ui_ref[pl.ds(r0, n2), :] = ti * c - tr * s
        return carry

    lax.fori_loop(0, n1, stage1, 0)

    def stage2(k2, carry):
        u = jnp.concatenate([ur_ref[pl.ds(k2, n1, stride=n2), :], ui_ref[pl.ds(k2, n1, stride=n2), :]], axis=0)
        p = jnp.dot(f3_ref[...], u, preferred_element_type=F32, precision=HIGHEST)
        ur_ref[pl.ds(k2, n1, stride=n2), :] = p[:n1]
        ui_ref[pl.ds(k2, n1, stride=n2), :] = p[n1:]
        return carry

    lax.fori_loop(0, n2, stage2, 0)

    rows = ROW_TILE

    def stage3(r, carry):
        r0 = pl.multiple_of(r * rows, rows)
        out = (jnp.dot(ur_ref[pl.ds(r0, rows), :], bdc_ref[...], preferred_element_type=F32, precision=HIGHEST)
               + jnp.dot(ui_ref[pl.ds(r0, rows), :], bds_ref[...], preferred_element_type=F32, precision=HIGHEST))
        o_ref[pl.ds(r0, rows), :] = out * norm
        return carry

    lax.fori_loop(0, (n1 * n2) // rows, stage3, 0)


def _fourier_mix(f2d, *, batch):
    rows = f2d.shape[0]
    n = rows // batch
    norm = 1.0 / math.sqrt(n * FOURIER_GC)
    bdc, bds = _channel_dft()
    blk = pl.BlockSpec((n, LANES), lambda b, j: (b, j))
    grid = (batch, FOURIER_W // LANES)
    out_shape = jax.ShapeDtypeStruct((rows, FOURIER_W), F32)
    mat = _const_spec((LANES, LANES))
    if n <= ROW_TILE:
        c, s = _dft_cos_sin(n)
        return pl.pallas_call(
            functools.partial(_fourier_dense_kernel, norm=norm),
            grid=grid,
            in_specs=[blk, _const_spec((n, n)), _const_spec((n, n)), mat, mat],
            out_specs=blk,
            out_shape=out_shape,
            compiler_params=_cparams(2),
            name="fourier_dense",
        )(f2d, jnp.asarray(c, F32), jnp.asarray(s, F32), bdc, bds)
    n2 = LANES
    n1 = n // n2
    c2, s2 = _dft_cos_sin(n2)
    f1 = jnp.asarray(np.concatenate([c2, -s2], axis=0), F32)
    c1, s1 = _dft_cos_sin(n1)
    f3 = jnp.asarray(np.block([[c1, s1], [-s1, c1]]), F32)
    tw_idx = np.outer(np.arange(n1), np.arange(n2)).reshape(-1)
    tw_ang = 2.0 * np.pi * tw_idx / n
    twc = jnp.broadcast_to(jnp.asarray(np.cos(tw_ang), F32)[:, None], (n, LANES))
    tws = jnp.broadcast_to(jnp.asarray(np.sin(tw_ang), F32)[:, None], (n, LANES))
    return pl.pallas_call(
        functools.partial(_fourier_fft_kernel, n1=n1, norm=norm),
        grid=grid,
        in_specs=[blk, _const_spec((2 * n2, n2)), _const_spec((n, LANES)), _const_spec((n, LANES)),
                  _const_spec((2 * n1, 2 * n1)), mat, mat],
        out_specs=blk,
        out_shape=out_shape,
        scratch_shapes=[pltpu.VMEM((n, LANES), F32), pltpu.VMEM((n, LANES), F32)],
        compiler_params=_cparams(2),
        name="fourier_fft",
    )(f2d, f1, twc, tws, f3, bdc, bds)


def _gelu_tanh(x):
    return 0.5 * x * (1.0 + jnp.tanh(math.sqrt(2.0 / math.pi) * (x + 0.044715 * (x * x * x))))


def _tail_kernel(*refs, glu, wa):
    if glu:
        x_ref, ya_ref, yb_ref, wg_ref, bg_ref, *rest = refs
    else:
        x_ref, ya_ref, yb_ref, *rest = refs
    (wo_ref, bo_ref, g1_ref, lg1_ref, lb1_ref, sh2_ref, sc2_ref, g2_ref,
     w1_ref, b1_ref, w2_ref, b2_ref, lg2_ref, lb2_ref, o_ref) = rest
    ya = ya_ref[...]
    if glu:
        g = _gelu_tanh(ya)
        ya = g * jax.nn.sigmoid(jnp.dot(g.astype(BF16), wg_ref[...], preferred_element_type=F32) + bg_ref[...])
    y = (jnp.dot(ya.astype(BF16), wo_ref[:wa, :], preferred_element_type=F32)
         + jnp.dot(yb_ref[...].astype(BF16), wo_ref[wa:, :], preferred_element_type=F32) + bo_ref[...])
    x1 = _ln(ALPHA * x_ref[...] + g1_ref[...] * y) * lg1_ref[...] + lb1_ref[...]
    h = (_ln(x1) * (1.0 + sc2_ref[...]) + sh2_ref[...]).astype(BF16)
    y2 = jnp.zeros_like(x1)
    for c in range(FFN_W // FFN_CHUNK):
        cols = slice(c * FFN_CHUNK, (c + 1) * FFN_CHUNK)
        a = jnp.maximum(jnp.dot(h, w1_ref[:, cols], preferred_element_type=F32) + b1_ref[:, cols], 0.0)
        y2 = y2 + jnp.dot((a * a).astype(BF16), w2_ref[cols, :], preferred_element_type=F32)
    y2 = y2 + b2_ref[...]
    o_ref[...] = _ln(ALPHA * x1 + g2_ref[...] * y2) * lg2_ref[...] + lb2_ref[...]


def _layer_tail(x2d, ya, yb, modl, glu_params, w_out, b_out, lg1, lb1, w1, b1, w2, b2, lg2, lb2, *, tm, row_fn):
    rows = x2d.shape[0]
    wa, wb = ya.shape[1], yb.shape[1]
    glu = glu_params is not None
    row = lambda v: v.reshape(1, -1)
    vec = _const_spec((1, D_MODEL))
    in_specs = [pl.BlockSpec((tm, D_MODEL), lambda i: (i, 0)),
                pl.BlockSpec((tm, wa), lambda i: (i, 0)),
                pl.BlockSpec((tm, wb), lambda i: (i, 0))]
    args = [x2d, ya, yb]
    if glu:
        in_specs += [_const_spec((wa, wa)), _const_spec((1, wa))]
        args += [glu_params[0], row(glu_params[1])]
    in_specs += [_const_spec((D_MODEL, D_MODEL)), vec, _mod_spec(row_fn, 2), vec, vec,
                 _mod_spec(row_fn, 3), _mod_spec(row_fn, 4), _mod_spec(row_fn, 5),
                 _const_spec((D_MODEL, FFN_W)), _const_spec((1, FFN_W)), _const_spec((FFN_W, D_MODEL)), vec, vec, vec]
    args += [w_out, row(b_out), modl, row(lg1), row(lb1), modl, modl, modl,
             w1, row(b1), w2, row(b2), row(lg2), row(lb2)]
    return pl.pallas_call(
        functools.partial(_tail_kernel, glu=glu, wa=wa),
        grid=(rows // tm,),
        in_specs=in_specs,
        out_specs=pl.BlockSpec((tm, D_MODEL), lambda i: (i, 0)),
        out_shape=jax.ShapeDtypeStruct((rows, D_MODEL), F32),
        compiler_params=_cparams(1),
        name="layer_tail_glu" if glu else "layer_tail",
    )(*args)


def _inproj_odd_kernel(x_ref, sh_ref, sc_ref, w_ref, wsp_ref, bsp_ref, s_ref, gm_ref):
    h = (_ln(x_ref[...]) * (1.0 + sc_ref[...]) + sh_ref[...]).astype(BF16)
    s_ref[...] = jnp.dot(h, w_ref[:, :S5_W], preferred_element_type=F32)
    u = jnp.dot(h, w_ref[:, S5_W:S5_W + GMLP_W], preferred_element_type=F32)
    v = jnp.dot(h, w_ref[:, S5_W + GMLP_W:], preferred_element_type=F32)
    for ch in range(x_ref.shape[0] // CHUNK):
        rows = slice(ch * CHUNK, (ch + 1) * CHUNK)
        for g in range(GMLP_GROUPS):
            cols = slice(g * GMLP_GC, (g + 1) * GMLP_GC)
            vn = _ln(v[rows, cols]).astype(BF16)
            sp = jnp.dot(wsp_ref[g], vn, preferred_element_type=F32) + bsp_ref[g]
            gm_ref[rows, cols] = u[rows, cols] * sp


def _inproj_odd(x2d, modl, w_bf, wsp_bf, bsp_lanes, *, tm, row_fn):
    rows = x2d.shape[0]
    return pl.pallas_call(
        _inproj_odd_kernel,
        grid=(rows // tm,),
        in_specs=[pl.BlockSpec((tm, D_MODEL), lambda i: (i, 0)), _mod_spec(row_fn, 0), _mod_spec(row_fn, 1),
                  _const_spec((D_MODEL, ODD_IN_W)), _const_spec((GMLP_GROUPS, CHUNK, CHUNK)),
                  _const_spec((GMLP_GROUPS, CHUNK, GMLP_GC))],
        out_specs=[pl.BlockSpec((tm, S5_W), lambda i: (i, 0)), pl.BlockSpec((tm, GMLP_W), lambda i: (i, 0))],
        out_shape=[jax.ShapeDtypeStruct((rows, S5_W), F32), jax.ShapeDtypeStruct((rows, GMLP_W), F32)],
        compiler_params=_cparams(1),
        name="inproj_odd_gmlp",
    )(x2d, modl, modl, w_bf, wsp_bf, bsp_lanes)


def _ln_mod_matmul_kernel(x_ref, sh_ref, sc_ref, w_ref, o_ref):
    h = (_ln(x_ref[...]) * (1.0 + sc_ref[...]) + sh_ref[...]).astype(BF16)
    o_ref[...] = jnp.dot(h, w_ref[...], preferred_element_type=F32)


def _ln_mod_matmul(x2d, modl, w_bf, *, tm, row_fn):
    rows, n = x2d.shape[0], w_bf.shape[1]
    return pl.pallas_call(
        _ln_mod_matmul_kernel,
        grid=(rows // tm,),
        in_specs=[pl.BlockSpec((tm, D_MODEL), lambda i: (i, 0)), _mod_spec(row_fn, 0), _mod_spec(row_fn, 1),
                  _const_spec((D_MODEL, n))],
        out_specs=pl.BlockSpec((tm, n), lambda i: (i, 0)),
        out_shape=jax.ShapeDtypeStruct((rows, n), F32),
        compiler_params=_cparams(1),
        name="ln_mod_matmul",
    )(x2d, modl, modl, w_bf)


def _s5_matrices(lam_re, lam_im, log_dt, b_re, b_im, c_re, c_im, d_skip):
    t_len, g_n, p_n, i_n = S5_T, S5_GROUPS, S5_STATE, S5_GC
    dt = jnp.exp(log_dt)[..., None]
    mag, th = lam_re * dt, lam_im * dt
    er = jnp.exp(mag)
    lbr, lbi = er * jnp.cos(th), er * jnp.sin(th)
    den = lam_re * lam_re + lam_im * lam_im
    nr, ni = lbr - 1.0, lbi
    cr = (nr * lam_re + ni * lam_im) / den
    ci = (ni * lam_re - nr * lam_im) / den
    bbr = cr[..., None] * b_re - ci[..., None] * b_im
    bbi = cr[..., None] * b_im + ci[..., None] * b_re
    tau = jnp.arange(t_len + 1, dtype=F32)[:, None, None, None]
    pm = jnp.exp(tau * mag)
    pr, pi = pm * jnp.cos(tau * th), pm * jnp.sin(tau * th)
    lbr_t = pr[..., None] * bbr - pi[..., None] * bbi
    lbi_t = pr[..., None] * bbi + pi[..., None] * bbr
    kern = (jnp.einsum('dgop,tdgpi->tdgoi', c_re, lbr_t, precision=HIGHEST)
            - jnp.einsum('dgop,tdgpi->tdgoi', c_im, lbi_t, precision=HIGHEST))
    s_idx = np.arange(t_len)[:, None]
    t_idx = np.arange(t_len)[None, :]
    fwd = jnp.where((t_idx >= s_idx)[..., None, None, None], kern[:, 0][np.clip(t_idx - s_idx, 0, None)], 0.0)
    bwd = jnp.where((s_idx >= t_idx)[..., None, None, None], kern[:, 1][np.clip(s_idx - t_idx, 0, None)], 0.0)
    skip = (jnp.asarray(np.eye(t_len), F32)[:, :, None, None, None]
            * (d_skip.reshape(g_n, i_n)[:, :, None] * jnp.asarray(np.eye(i_n), F32))[None, None])
    toep = fwd + bwd + skip
    m_mat = toep.transpose(2, 0, 4, 1, 3).reshape(g_n, t_len * i_n, t_len * i_n)

    def state_in(re, im, order):
        sel = lambda a: a[order].transpose(1, 0, 3, 2).reshape(g_n, t_len * i_n, p_n)
        return sel(re), sel(im)

    rev = np.arange(t_len - 1, -1, -1)
    wf_re, wf_im = state_in(lbr_t[:t_len, 0], lbi_t[:t_len, 0], rev)
    wb_re, wb_im = state_in(lbr_t[:t_len, 1], lbi_t[:t_len, 1], np.arange(t_len))
    w_state = jnp.stack([wf_re, wf_im, wb_re, wb_im], axis=2)
    eye2 = jnp.asarray(np.eye(2), F32)
    w_state = w_state.reshape(S5_PAIRS, 2, t_len * i_n, 4, p_n)
    w_state = jnp.einsum('jgkqp,gh->jgkqhp', w_state, eye2).reshape(S5_PAIRS, 2 * t_len * i_n, 4 * 2 * p_n)

    def state_out(d, powers):
        prd, pid = pr[powers, d], pi[powers, d]
        re = c_re[d][None] * prd[:, :, None, :] - c_im[d][None] * pid[:, :, None, :]
        im = c_re[d][None] * pid[:, :, None, :] + c_im[d][None] * prd[:, :, None, :]
        fl = lambda a: a.transpose(1, 3, 0, 2).reshape(g_n, p_n, t_len * i_n)
        return fl(re), fl(-im)

    vf_re, vf_im = state_out(0, np.arange(1, t_len + 1))
    vb_re, vb_im = state_out(1, np.arange(t_len, 0, -1))
    v_state = jnp.stack([vf_re, vf_im, vb_re, vb_im], axis=1)
    v_state = v_state.reshape(S5_PAIRS, 2, 4, p_n, t_len * i_n)
    v_state = jnp.einsum('jgqpk,gh->jqgphk', v_state, eye2).reshape(S5_PAIRS, 4 * 2 * p_n, 2 * t_len * i_n)
    a_pow = jnp.stack([pr[t_len, 0], pi[t_len, 0], pr[t_len, 1], pi[t_len, 1]], axis=0)
    a_pow = a_pow.reshape(4, g_n * p_n)
    m_pair = m_mat.reshape(S5_PAIRS, 2, t_len * i_n, t_len * i_n)
    return m_pair.astype(BF16), w_state.astype(BF16), v_state.astype(BF16), a_pow


def _s5_state_kernel(u_ref, w_ref, fr_ref, fi_ref, br_ref, bi_ref):
    st = jnp.dot(u_ref[...], w_ref[...], preferred_element_type=F32)
    for q, ref in enumerate((fr_ref, fi_ref, br_ref, bi_ref)):
        ref[...] = st[:, q * LANES:(q + 1) * LANES]


def _s5_scan_kernel(a_ref, fr_ref, fi_ref, br_ref, bi_ref, hfr_ref, hfi_ref, hbr_ref, hbi_ref,
                    *, batch, n_ctx, n_all):
    afr, afi, abr, abi = (a_ref[pl.ds(q, 1), :] for q in range(4))
    zero = jnp.zeros_like(afr)

    def body(k, carry):
        kb = jnp.where(k < n_ctx, n_ctx - 1 - k, n_all + n_ctx - 1 - k)
        out = []
        for b in range(batch):
            hr, hi, gr, gi = carry[4 * b:4 * b + 4]
            rf = pl.ds(b * n_all + k, 1)
            rb = pl.ds(b * n_all + kb, 1)
            hfr_ref[rf, :] = hr
            hfi_ref[rf, :] = hi
            hbr_ref[rb, :] = gr
            hbi_ref[rb, :] = gi
            out += [afr * hr - afi * hi + fr_ref[rf, :], afr * hi + afi * hr + fi_ref[rf, :],
                    abr * gr - abi * gi + br_ref[rb, :], abr * gi + abi * gr + bi_ref[rb, :]]
        return tuple(out)

    lax.fori_loop(0, n_all, body, (zero,) * (4 * batch))


def _s5_out_kernel(u_ref, m_ref, v_ref, hfr_ref, hfi_ref, hbr_ref, hbi_ref, y_ref):
    width = S5_T * S5_GC
    st = jnp.concatenate([r[...].astype(BF16) for r in (hfr_ref, hfi_ref, hbr_ref, hbi_ref)], axis=1)
    y = jnp.dot(st, v_ref[...], preferred_element_type=F32)
    for g in range(2):
        cols = slice(g * width, (g + 1) * width)
        y_ref[:, cols] = y[:, cols] + jnp.dot(u_ref[:, cols], m_ref[g], preferred_element_type=F32)


def _s5_mix(s_lat, s_ctx, mats, *, batch):
    m_pair, w_state, v_state, a_pow = mats
    n_lat = s_lat.shape[0] // batch // S5_T
    n_ctx = s_ctx.shape[0] // batch // S5_T
    n_all = n_lat + n_ctx
    width = S5_T * S5_GC

    def fold(s, n):
        return s.reshape(batch, n, S5_T, S5_GROUPS, S5_GC).transpose(0, 1, 3, 2, 4).reshape(
            batch, n, S5_GROUPS * width)

    u = jnp.concatenate([fold(s_ctx, n_ctx), fold(s_lat, n_lat)], axis=1).astype(BF16)
    rows = batch * n_all
    u = u.reshape(rows, S5_GROUPS * width)
    n_state = S5_GROUPS * S5_STATE
    u_spec = pl.BlockSpec((rows, 2 * width), lambda j: (0, j))
    st_spec = pl.BlockSpec((rows, LANES), lambda j: (0, j))
    st_shape = jax.ShapeDtypeStruct((rows, n_state), F32)
    s_in = pl.pallas_call(
        _s5_state_kernel,
        grid=(S5_PAIRS,),
        in_specs=[u_spec, pl.BlockSpec((None, 2 * width, 4 * LANES), lambda j: (j, 0, 0))],
        out_specs=[st_spec] * 4,
        out_shape=[st_shape] * 4,
        compiler_params=_cparams(1),
        name="s5_chunk_states",
    )(u, w_state)
    full = pl.BlockSpec((rows, S5_SCAN_LANES), lambda i: (0, i))
    h_prev = pl.pallas_call(
        functools.partial(_s5_scan_kernel, batch=batch, n_ctx=n_ctx, n_all=n_all),
        grid=(n_state // S5_SCAN_LANES,),
        in_specs=[pl.BlockSpec((4, S5_SCAN_LANES), lambda i: (0, i))] + [full] * 4,
        out_specs=[full] * 4,
        out_shape=[st_shape] * 4,
        compiler_params=_cparams(1),
        name="s5_chunk_scan",
    )(a_pow, *s_in)
    y = pl.pallas_call(
        _s5_out_kernel,
        grid=(S5_PAIRS,),
        in_specs=[u_spec, pl.BlockSpec((None, 2, width, width), lambda j: (j, 0, 0, 0)),
                  pl.BlockSpec((None, 4 * LANES, 2 * width), lambda j: (j, 0, 0))] + [st_spec] * 4,
        out_specs=u_spec,
        out_shape=jax.ShapeDtypeStruct((rows, S5_GROUPS * width), F32),
        compiler_params=_cparams(1),
        name="s5_output",
    )(u, m_pair, v_state, *h_prev)
    y = y.reshape(batch, n_all, S5_GROUPS, S5_T, S5_GC)[:, n_ctx:]
    return y.transpose(0, 1, 3, 2, 4).reshape(batch * n_lat * S5_T, S5_W)


def kernel(x, c, ctx, c_ctx, w_mod, b_mod, w_out, b_out, ln_mix_g, ln_mix_b, w_ffn1, b_ffn1, w_ffn2,
           b_ffn2, ln_ffn_g, ln_ffn_b, w_in_ab, lam_q1, lam_k1, lam_q2, lam_k2, subln_g, w_in_cd,
           s5_lam_re, s5_lam_im, s5_log_dt, s5_b_re, s5_b_im, s5_c_re, s5_c_im, s5_d, w_glu, b_glu,
           w_sp, b_sp):
    batch, n_lat, d = x.shape
    n_ctx = ctx.shape[1]
    assert d == D_MODEL and n_lat % ROW_TILE == 0 and n_ctx % CHUNK == 0 and batch + 1 <= MOD_ROWS
    mod = _modulation(c, c_ctx, w_mod, b_mod)
    xl = x.reshape(batch * n_lat, d)
    xc = ctx.reshape(batch * n_ctx, d)
    tiles = n_lat // ROW_TILE
    lat_rows = dict(tm=ROW_TILE, row_fn=lambda i: i // tiles)
    ctx_rows = dict(tm=n_ctx, row_fn=lambda i: batch)
    rope_tabs = _rope_tables(n_lat)
    for l in range(DEPTH):
        last = l == DEPTH - 1
        e = l // 2
        modl = mod[l]
        tail_w = (w_out[l].astype(BF16), b_out[l], ln_mix_g[l], ln_mix_b[l], w_ffn1[l].astype(BF16), b_ffn1[l],
                  w_ffn2[l].astype(BF16), b_ffn2[l], ln_ffn_g[l], ln_ffn_b[l])
        if l % 2 == 0:
            w_in = w_in_ab[e].astype(BF16)
            lam_init = 0.8 - 0.6 * math.exp(-0.3 * l)
            lam_vecs = (lam_q1[e], lam_k1[e], lam_q2[e], lam_k2[e])
            f, q, k, vt = _inproj_even(xl, modl, w_in, rope_tabs, batch=batch, tiles_per_batch=tiles, **lat_rows)
            fc, qc, kc, vct = _inproj_even(xc, modl, w_in, None, batch=batch, tiles_per_batch=1, **ctx_rows)
            ya = _diff_attention(q, kc, vct, k, vt, lam_vecs, subln_g[e], lam_init, batch=batch, tq=ATTN_TQ)
            yf = _fourier_mix(f, batch=batch)
            xl = _layer_tail(xl, yf, ya, modl, None, *tail_w, **lat_rows)
            if not last:
                yac = _diff_attention(qc, kc, vct, None, None, lam_vecs, subln_g[e], lam_init, batch=batch,
                                      tq=n_ctx)
                yfc = _fourier_mix(fc, batch=batch)
                xc = _layer_tail(xc, yfc, yac, modl, None, *tail_w, **ctx_rows)
        else:
            assert last, "the S5 / gMLP layer is only implemented as the final layer"
            w_in = w_in_cd[e].astype(BF16)
            bsp_lanes = jnp.broadcast_to(b_sp[e][:, :, None], (GMLP_GROUPS, CHUNK, GMLP_GC))
            s_lat, gm = _inproj_odd(xl, modl, w_in, w_sp[e].astype(BF16), bsp_lanes, **lat_rows)
            s_ctx = _ln_mod_matmul(xc, modl, w_in[:, :S5_W], **ctx_rows)
            mats = _s5_matrices(s5_lam_re[e], s5_lam_im[e], s5_log_dt[e], s5_b_re[e], s5_b_im[e], s5_c_re[e],
                                s5_c_im[e], s5_d[e])
            ys = _s5_mix(s_lat, s_ctx, mats, batch=batch)
            xl = _layer_tail(xl, ys, gm, modl, (w_glu[e].astype(BF16), b_glu[e]), *tail_w, **lat_rows)
    return xl.reshape(batch, n_lat, d)
```

```python
import functools
import math

import numpy as np
import jax
import jax.numpy as jnp
from jax import lax
from jax.experimental import pallas as pl
from jax.experimental.pallas import tpu as pltpu

D_MODEL = 1024
DEPTH = 2
GRID_W = 64
FOURIER_W = D_MODEL // 4
FOURIER_GC = 64
DIFF_HD = 64
DIFF_VD = 2 * DIFF_HD
DIFF_HEADS = (D_MODEL - FOURIER_W) // DIFF_VD
QK_W = DIFF_HEADS * 2 * DIFF_HD
V_W = DIFF_HEADS * DIFF_VD
EVEN_IN_W = FOURIER_W + 2 * QK_W + V_W
DIFF_SCALE = DIFF_HD ** -0.5
ROPE_BASE = 10000.0
ROPE_FREQS = DIFF_HD // 4
S5_W = D_MODEL // 2
S5_GC = 16
S5_GROUPS = S5_W // S5_GC
S5_STATE = 64
GMLP_W = D_MODEL // 2
GMLP_GC = 128
GMLP_GROUPS = GMLP_W // GMLP_GC
CHUNK = 128
ODD_IN_W = S5_W + 2 * GMLP_W
FFN_W = 4 * D_MODEL
LN_EPS = 1e-5
ALPHA = (2 * DEPTH) ** 0.25

F32 = jnp.float32
BF16 = jnp.bfloat16
HIGHEST = lax.Precision.HIGHEST

LANES = 128
MOD_ROWS = 8
ROW_TILE = 512
ATTN_TQ = 512
ATTN_TK = 256
S5_T = 16
S5_PAIRS = S5_GROUPS // 2
S5_SCAN_LANES = 512
FFN_CHUNK = 1024
VMEM_LIMIT = 56 * 2 ** 20
Q_SCALE = DIFF_SCALE * math.log2(math.e)


def _cparams(n_axes, vmem=VMEM_LIMIT):
    return pltpu.CompilerParams(dimension_semantics=("arbitrary",) * n_axes, vmem_limit_bytes=vmem)


def _const_spec(shape):
    zeros = (0,) * len(shape)
    return pl.BlockSpec(shape, lambda *_: zeros, pipeline_mode=pl.Buffered(1))


def _ln(x):
    xc = x - jnp.mean(x, -1, keepdims=True)
    var = jnp.mean(xc * xc, -1, keepdims=True)
    return xc * lax.rsqrt(var + LN_EPS)


def _mod_spec(row_fn, which):
    return pl.BlockSpec((None, 1, D_MODEL), lambda i: (row_fn(i) * 6 + which, 0, 0))


def _mod_kernel(c_ref, w_ref, b_ref, o_ref):
    c = c_ref[...]
    a = c * jax.nn.sigmoid(c)
    o_ref[...] = jnp.dot(a, w_ref[...], preferred_element_type=F32, precision=HIGHEST) + b_ref[...]


def _modulation(c, c_ctx, w_mod, b_mod):
    batch = c.shape[0]
    rows = jnp.concatenate([c, c_ctx[None], jnp.zeros((MOD_ROWS - batch - 1, D_MODEL), F32)], axis=0)
    out = pl.pallas_call(
        _mod_kernel,
        grid=(DEPTH, 6),
        in_specs=[
            pl.BlockSpec((MOD_ROWS, D_MODEL), lambda l, j: (0, 0)),
            pl.BlockSpec((None, D_MODEL, D_MODEL), lambda l, j: (l, 0, j)),
            pl.BlockSpec((None, 1, D_MODEL), lambda l, j: (l, 0, j)),
        ],
        out_specs=pl.BlockSpec((None, MOD_ROWS, D_MODEL), lambda l, j: (l, 0, j)),
        out_shape=jax.ShapeDtypeStruct((DEPTH, MOD_ROWS, 6 * D_MODEL), F32),
        compiler_params=_cparams(2),
        name="modulation",
    )(rows, w_mod, b_mod.reshape(DEPTH, 1, 6 * D_MODEL))
    return out.reshape(DEPTH, MOD_ROWS * 6, 1, D_MODEL)


def _inproj_even_kernel(*refs, rope):
    if rope:
        x_ref, sh_ref, sc_ref, w_ref, cos_ref, sin_ref, f_ref, q_ref, k_ref, vt_ref = refs
    else:
        x_ref, sh_ref, sc_ref, w_ref, _, _, f_ref, q_ref, k_ref, vt_ref = refs
    h = (_ln(x_ref[...]) * (1.0 + sc_ref[...]) + sh_ref[...]).astype(BF16)
    f_ref[...] = jnp.dot(h, w_ref[:, :FOURIER_W], preferred_element_type=F32)
    width = 2 * LANES
    if rope:
        cos = cos_ref[...]
        sin = sin_ref[...]
        lane = lax.broadcasted_iota(jnp.int32, cos.shape, 1)
        first_half = (lane & ROPE_FREQS) == 0
    for j in range(QK_W // width):
        for base, o_ref, scale in ((FOURIER_W, q_ref, Q_SCALE), (FOURIER_W + QK_W, k_ref, None)):
            t = jnp.dot(h, w_ref[:, base + j * width:base + (j + 1) * width], preferred_element_type=F32)
            if rope:
                partner = jnp.where(first_half, pltpu.roll(t, width - ROPE_FREQS, 1),
                                    pltpu.roll(t, ROPE_FREQS, 1))
                t = t * cos + partner * sin
            if scale is not None:
                t = t * scale
            o_ref[:, j * width:(j + 1) * width] = t.astype(BF16)
    v0 = FOURIER_W + 2 * QK_W
    for j in range(V_W // width):
        v = jnp.dot(h, w_ref[:, v0 + j * width:v0 + (j + 1) * width], preferred_element_type=F32)
        tk = vt_ref.shape[2]
        for c in range(vt_ref.shape[0]):
            vt_ref[c, j * width:(j + 1) * width, :] = v[c * tk:(c + 1) * tk].T.astype(BF16)


def _inproj_even(x2d, modl, w_bf, rope_tabs, kv_all, *, batch, n_keys, key_row0, tm, row_fn):
    rows = x2d.shape[0]
    rope = rope_tabs is not None
    tiles_per_batch = rows // batch // tm
    blk0 = key_row0 // tm
    in_specs = [
        pl.BlockSpec((tm, D_MODEL), lambda i: (i, 0)),
        _mod_spec(row_fn, 0),
        _mod_spec(row_fn, 1),
        _const_spec((D_MODEL, EVEN_IN_W)),
    ]
    args = [x2d, modl, modl, w_bf]
    if rope:
        in_specs += [pl.BlockSpec((tm, 2 * LANES), lambda i: (i % tiles_per_batch, 0))] * 2
        args += list(rope_tabs)
        aliases = {}
    else:
        in_specs += [pl.BlockSpec(memory_space=pl.ANY)] * 2
        args += list(kv_all)
        aliases = {4: 2, 5: 3}
    kv_idx = lambda i: (i // tiles_per_batch, blk0 + i % tiles_per_batch, 0)
    return pl.pallas_call(
        functools.partial(_inproj_even_kernel, rope=rope),
        grid=(rows // tm,),
        in_specs=in_specs,
        out_specs=[
            pl.BlockSpec((tm, FOURIER_W), lambda i: (i, 0)),
            pl.BlockSpec((tm, QK_W), lambda i: (i, 0)),
            pl.BlockSpec((None, tm, QK_W), kv_idx),
            pl.BlockSpec((None, tm // ATTN_TK, V_W, ATTN_TK), lambda i: kv_idx(i) + (0,)),
        ],
        out_shape=[
            jax.ShapeDtypeStruct((rows, FOURIER_W), F32),
            jax.ShapeDtypeStruct((rows, QK_W), BF16),
            jax.ShapeDtypeStruct((batch, n_keys, QK_W), BF16),
            jax.ShapeDtypeStruct((batch, n_keys // ATTN_TK, V_W, ATTN_TK), BF16),
        ],
        input_output_aliases=aliases,
        compiler_params=_cparams(1),
        name="inproj_even_rope" if rope else "inproj_even",
    )(*args)


def _rope_tables(n):
    rows = n // GRID_W
    row = jnp.repeat(jnp.arange(rows, dtype=F32), GRID_W)
    col = jnp.tile(jnp.arange(GRID_W, dtype=F32), rows)
    inv = jnp.power(ROPE_BASE, -jnp.arange(ROPE_FREQS, dtype=F32) / ROPE_FREQS)
    ang = jnp.stack([row[:, None] * inv, col[:, None] * inv], axis=1)
    cos, sin = jnp.cos(ang), jnp.sin(ang)
    cos_map = jnp.concatenate([cos, cos], axis=-1).reshape(n, 4 * ROPE_FREQS)
    sin_map = jnp.concatenate([-sin, sin], axis=-1).reshape(n, 4 * ROPE_FREQS)
    reps = 2 * LANES // (4 * ROPE_FREQS)
    return jnp.tile(cos_map, (1, reps)), jnp.tile(sin_map, (1, reps))


def _attn_kernel(lq1, lk1, lq2, lk2, g_ref, q_ref, k_ref, vt_ref, o_ref, acc_ref, sa_ref, sb_ref, ma_ref, mb_ref,
                 *, lam_init):
    tq = q_ref.shape[0]
    n_chunks, _, tk = vt_ref.shape
    q = q_ref[...]
    lane = lax.broadcasted_iota(jnp.int32, q.shape, 1)
    zero = jnp.zeros_like(q)
    q_cat = jnp.concatenate([jnp.where(lane < DIFF_HD, q, zero), jnp.where(lane >= DIFF_HD, q, zero)], axis=0)
    nt_dims = (((1,), (1,)), ((), ()))

    def scores(c, s_ref, mc_ref):
        kb = k_ref[pl.ds(pl.multiple_of(c * tk, tk), tk), :]
        s = lax.dot_general(kb, q_cat, nt_dims, preferred_element_type=F32)
        s_ref[...] = s
        mc_ref[...] = jnp.max(s, axis=0, keepdims=True)

    def softmax_pv(c, s_ref, mc_ref, m_prev, l_prev):
        m = jnp.maximum(m_prev, mc_ref[...])
        alpha = jnp.exp2(m_prev - m)
        p = jnp.exp2(s_ref[...] - m)
        l = alpha * l_prev + jnp.sum(p, axis=0, keepdims=True)
        acc_ref[...] = acc_ref[...] * alpha + jnp.dot(vt_ref[c], p.astype(BF16), preferred_element_type=F32)
        return m, l

    acc_ref[...] = jnp.zeros_like(acc_ref)
    scores(0, sa_ref, ma_ref)

    def body(j, carry):
        c = 2 * j
        scores(c + 1, sb_ref, mb_ref)
        carry = softmax_pv(c, sa_ref, ma_ref, *carry)
        scores(c + 2, sa_ref, ma_ref)
        return softmax_pv(c + 1, sb_ref, mb_ref, *carry)

    init = (jnp.full((1, 2 * tq), -jnp.inf, F32), jnp.zeros((1, 2 * tq), F32))
    m, l = lax.fori_loop(0, (n_chunks - 1) // 2, body, init)
    m, l = softmax_pv(n_chunks - 1, sa_ref, ma_ref, m, l)

    lam = (jnp.exp(jnp.sum(lq1[...] * lk1[...], keepdims=True))
           - jnp.exp(jnp.sum(lq2[...] * lk2[...], keepdims=True)) + lam_init)
    on = acc_ref[...] / l
    o = on[:, :tq] - lam * on[:, tq:]
    o = o * lax.rsqrt(jnp.mean(o * o, axis=0, keepdims=True) + LN_EPS)
    o_ref[...] = o.T * (g_ref[...] * (1.0 - lam_init))


def _diff_attention(q, k_all, vt_all, lam_vecs, subln_g, lam_init, *, batch, tq, key_row0, n_keys):
    rows = q.shape[0]
    nq = rows // batch // tq
    tk = vt_all.shape[3]
    n_chunks = n_keys // tk
    assert n_chunks % 2 == 1 and key_row0 % n_keys == 0
    kblk = key_row0 // n_keys
    vec_spec = pl.BlockSpec((1, DIFF_HD), lambda b, h, i: (0, 0))
    in_specs = [vec_spec] * 4 + [
        pl.BlockSpec((1, DIFF_VD), lambda b, h, i: (0, 0)),
        pl.BlockSpec((tq, DIFF_VD), lambda b, h, i: (b * nq + i, h)),
        pl.BlockSpec((None, n_keys, DIFF_VD), lambda b, h, i: (b, kblk, h)),
        pl.BlockSpec((None, n_chunks, DIFF_VD, tk), lambda b, h, i: (b, kblk, h, 0)),
    ]
    args = [v.reshape(1, DIFF_HD) for v in lam_vecs] + [subln_g.reshape(1, DIFF_VD), q, k_all, vt_all]
    return pl.pallas_call(
        functools.partial(_attn_kernel, lam_init=lam_init),
        grid=(batch, DIFF_HEADS, nq),
        in_specs=in_specs,
        out_specs=pl.BlockSpec((tq, DIFF_VD), lambda b, h, i: (b * nq + i, h)),
        out_shape=jax.ShapeDtypeStruct((rows, V_W), F32),
        scratch_shapes=[pltpu.VMEM((DIFF_VD, 2 * tq), F32), pltpu.VMEM((tk, 2 * tq), F32),
                        pltpu.VMEM((tk, 2 * tq), F32), pltpu.VMEM((1, 2 * tq), F32), pltpu.VMEM((1, 2 * tq), F32)],
        compiler_params=_cparams(3),
        name="diff_attention" if n_chunks > 1 else "diff_attention_ctx",
    )(*args)


def _dft_cos_sin(n):
    idx = np.outer(np.arange(n), np.arange(n)) % n
    ang = 2.0 * np.pi * idx / n
    return np.cos(ang), np.sin(ang)


def _channel_dft():
    c, s = _dft_cos_sin(FOURIER_GC)
    eye = np.eye(LANES // FOURIER_GC)
    return jnp.asarray(np.kron(eye, c), F32), jnp.asarray(np.kron(eye, s), F32)


def _fourier_dense_kernel(x_ref, cl_ref, sl_ref, bdc_ref, bds_ref, o_ref, *, norm):
    x = x_ref[...]
    pr = jnp.dot(cl_ref[...], x, preferred_element_type=F32, precision=HIGHEST)
    pi = -jnp.dot(sl_ref[...], x, preferred_element_type=F32, precision=HIGHEST)
    out = (jnp.dot(pr, bdc_ref[...], preferred_element_type=F32, precision=HIGHEST)
           + jnp.dot(pi, bds_ref[...], preferred_element_type=F32, precision=HIGHEST))
    o_ref[...] = out * norm


def _fourier_fft_kernel(x_ref, f1_ref, twc_ref, tws_ref, f3_ref, bdc_ref, bds_ref, o_ref, ur_ref, ui_ref,
                        *, n1, norm):
    n2 = LANES

    def stage1(i, carry):
        xs = x_ref[pl.ds(i, n2, stride=n1), :]
        t = jnp.dot(f1_ref[...], xs, preferred_element_type=F32, precision=HIGHEST)
        tr, ti = t[:n2], t[n2:]
        r0 = pl.multiple_of(i * n2, n2)
        c = twc_ref[pl.ds(r0, n2), :]
        s = tws_ref[pl.ds(r0, n2), :]
        ur_ref[pl.ds(r0, n2), :] = tr * c + ti * s
        ui_ref[pl.ds(r0, n2), :] = ti * c - tr * s
        return carry

    lax.fori_loop(0, n1, stage1, 0)

    def stage2(k2, carry):
        u = jnp.concatenate([ur_ref[pl.ds(k2, n1, stride=n2), :], ui_ref[pl.ds(k2, n1, stride=n2), :]], axis=0)
        p = jnp.dot(f3_ref[...], u, preferred_element_type=F32, precision=HIGHEST)
        ur_ref[pl.ds(k2, n1, stride=n2), :] = p[:n1]
        ui_ref[pl.ds(k2, n1, stride=n2), :] = p[n1:]
        return carry

    lax.fori_loop(0, n2, stage2, 0)

    rows = ROW_TILE

    def stage3(r, carry):
        r0 = pl.multiple_of(r * rows, rows)
        out = (jnp.dot(ur_ref[pl.ds(r0, rows), :], bdc_ref[...], preferred_element_type=F32, precision=HIGHEST)
               + jnp.dot(ui_ref[pl.ds(r0, rows), :], bds_ref[...], preferred_element_type=F32, precision=HIGHEST))
        o_ref[pl.ds(r0, rows), :] = out * norm
        return carry

    lax.fori_loop(0, (n1 * n2) // rows, stage3, 0)


def _fourier_mix(f2d, *, batch):
    rows = f2d.shape[0]
    n = rows // batch
    norm = 1.0 / math.sqrt(n * FOURIER_GC)
    bdc, bds = _channel_dft()
    blk = pl.BlockSpec((n, LANES), lambda b, j: (b, j))
    grid = (batch, FOURIER_W // LANES)
    out_shape = jax.ShapeDtypeStruct((rows, FOURIER_W), F32)
    mat = _const_spec((LANES, LANES))
    if n <= ROW_TILE:
        c, s = _dft_cos_sin(n)
        return pl.pallas_call(
            functools.partial(_fourier_dense_kernel, norm=norm),
            grid=grid,
            in_specs=[blk, _const_spec((n, n)), _const_spec((n, n)), mat, mat],
            out_specs=blk,
            out_shape=out_shape,
            compiler_params=_cparams(2),
            name="fourier_dense",
        )(f2d, jnp.asarray(c, F32), jnp.asarray(s, F32), bdc, bds)
    n2 = LANES
    n1 = n // n2
    c2, s2 = _dft_cos_sin(n2)
    f1 = jnp.asarray(np.concatenate([c2, -s2], axis=0), F32)
    c1, s1 = _dft_cos_sin(n1)
    f3 = jnp.asarray(np.block([[c1, s1], [-s1, c1]]), F32)
    tw_idx = np.outer(np.arange(n1), np.arange(n2)).reshape(-1)
    tw_ang = 2.0 * np.pi * tw_idx / n
    twc = jnp.broadcast_to(jnp.asarray(np.cos(tw_ang), F32)[:, None], (n, LANES))
    tws = jnp.broadcast_to(jnp.asarray(np.sin(tw_ang), F32)[:, None], (n, LANES))
    return pl.pallas_call(
        functools.partial(_fourier_fft_kernel, n1=n1, norm=norm),
        grid=grid,
        in_specs=[blk, _const_spec((2 * n2, n2)), _const_spec((n, LANES)), _const_spec((n, LANES)),
                  _const_spec((2 * n1, 2 * n1)), mat, mat],
        out_specs=blk,
        out_shape=out_shape,
        scratch_shapes=[pltpu.VMEM((n, LANES), F32), pltpu.VMEM((n, LANES), F32)],
        compiler_params=_cparams(2),
        name="fourier_fft",
    )(f2d, f1, twc, tws, f3, bdc, bds)


def _gelu_tanh(x):
    return 0.5 * x * (1.0 + jnp.tanh(math.sqrt(2.0 / math.pi) * (x + 0.044715 * (x * x * x))))


def _tail_kernel(*refs, glu, wa):
    if glu:
        x_ref, ya_ref, yb_ref, wg_ref, bg_ref, *rest = refs
    else:
        x_ref, ya_ref, yb_ref, *rest = refs
    (wo_ref, bo_ref, g1_ref, lg1_ref, lb1_ref, sh2_ref, sc2_ref, g2_ref,
     w1_ref, b1_ref, w2_ref, b2_ref, lg2_ref, lb2_ref, o_ref) = rest
    ya = ya_ref[...]
    if glu:
        g = _gelu_tanh(ya)
        ya = g * jax.nn.sigmoid(jnp.dot(g.astype(BF16), wg_ref[...], preferred_element_type=F32) + bg_ref[...])
    y = (jnp.dot(ya.astype(BF16), wo_ref[:wa, :], preferred_element_type=F32)
         + jnp.dot(yb_ref[...].astype(BF16), wo_ref[wa:, :], preferred_element_type=F32) + bo_ref[...])
    x1 = _ln(ALPHA * x_ref[...] + g1_ref[...] * y) * lg1_ref[...] + lb1_ref[...]
    h = (_ln(x1) * (1.0 + sc2_ref[...]) + sh2_ref[...]).astype(BF16)
    y2 = jnp.zeros_like(x1)
    for c in range(FFN_W // FFN_CHUNK):
        cols = slice(c * FFN_CHUNK, (c + 1) * FFN_CHUNK)
        a = jnp.maximum(jnp.dot(h, w1_ref[:, cols], preferred_element_type=F32) + b1_ref[:, cols], 0.0)
        y2 = y2 + jnp.dot((a * a).astype(BF16), w2_ref[cols, :], preferred_element_type=F32)
    y2 = y2 + b2_ref[...]
    o_ref[...] = _ln(ALPHA * x1 + g2_ref[...] * y2) * lg2_ref[...] + lb2_ref[...]


def _layer_tail(x2d, ya, yb, modl, glu_params, w_out, b_out, lg1, lb1, w1, b1, w2, b2, lg2, lb2, *, tm, row_fn):
    rows = x2d.shape[0]
    wa, wb = ya.shape[1], yb.shape[1]
    glu = glu_params is not None
    row = lambda v: v.reshape(1, -1)
    vec = _const_spec((1, D_MODEL))
    in_specs = [pl.BlockSpec((tm, D_MODEL), lambda i: (i, 0)),
                pl.BlockSpec((tm, wa), lambda i: (i, 0)),
                pl.BlockSpec((tm, wb), lambda i: (i, 0))]
    args = [x2d, ya, yb]
    if glu:
        in_specs += [_const_spec((wa, wa)), _const_spec((1, wa))]
        args += [glu_params[0], row(glu_params[1])]
    in_specs += [_const_spec((D_MODEL, D_MODEL)), vec, _mod_spec(row_fn, 2), vec, vec,
                 _mod_spec(row_fn, 3), _mod_spec(row_fn, 4), _mod_spec(row_fn, 5),
                 _const_spec((D_MODEL, FFN_W)), _const_spec((1, FFN_W)), _const_spec((FFN_W, D_MODEL)), vec, vec, vec]
    args += [w_out, row(b_out), modl, row(lg1), row(lb1), modl, modl, modl,
             w1, row(b1), w2, row(b2), row(lg2), row(lb2)]
    return pl.pallas_call(
        functools.partial(_tail_kernel, glu=glu, wa=wa),
        grid=(rows // tm,),
        in_specs=in_specs,
        out_specs=pl.BlockSpec((tm, D_MODEL), lambda i: (i, 0)),
        out_shape=jax.ShapeDtypeStruct((rows, D_MODEL), F32),
        compiler_params=_cparams(1),
        name="layer_tail_glu" if glu else "layer_tail",
    )(*args)


def _inproj_odd_kernel(x_ref, sh_ref, sc_ref, w_ref, wsp_ref, bsp_ref, s_ref, gm_ref):
    h = (_ln(x_ref[...]) * (1.0 + sc_ref[...]) + sh_ref[...]).astype(BF16)
    s_ref[...] = jnp.dot(h, w_ref[:, :S5_W], preferred_element_type=F32)
    u = jnp.dot(h, w_ref[:, S5_W:S5_W + GMLP_W], preferred_element_type=F32)
    v = jnp.dot(h, w_ref[:, S5_W + GMLP_W:], preferred_element_type=F32)
    for ch in range(x_ref.shape[0] // CHUNK):
        rows = slice(ch * CHUNK, (ch + 1) * CHUNK)
        for g in range(GMLP_GROUPS):
            cols = slice(g * GMLP_GC, (g + 1) * GMLP_GC)
            vn = _ln(v[rows, cols]).astype(BF16)
            sp = jnp.dot(wsp_ref[g], vn, preferred_element_type=F32) + bsp_ref[g]
            gm_ref[rows, cols] = u[rows, cols] * sp


def _inproj_odd(x2d, modl, w_bf, wsp_bf, bsp_lanes, *, tm, row_fn):
    rows = x2d.shape[0]
    return pl.pallas_call(
        _inproj_odd_kernel,
        grid=(rows // tm,),
        in_specs=[pl.BlockSpec((tm, D_MODEL), lambda i: (i, 0)), _mod_spec(row_fn, 0), _mod_spec(row_fn, 1),
                  _const_spec((D_MODEL, ODD_IN_W)), _const_spec((GMLP_GROUPS, CHUNK, CHUNK)),
                  _const_spec((GMLP_GROUPS, CHUNK, GMLP_GC))],
        out_specs=[pl.BlockSpec((tm, S5_W), lambda i: (i, 0)), pl.BlockSpec((tm, GMLP_W), lambda i: (i, 0))],
        out_shape=[jax.ShapeDtypeStruct((rows, S5_W), F32), jax.ShapeDtypeStruct((rows, GMLP_W), F32)],
        compiler_params=_cparams(1),
        name="inproj_odd_gmlp",
    )(x2d, modl, modl, w_bf, wsp_bf, bsp_lanes)


def _ln_mod_matmul_kernel(x_ref, sh_ref, sc_ref, w_ref, o_ref):
    h = (_ln(x_ref[...]) * (1.0 + sc_ref[...]) + sh_ref[...]).astype(BF16)
    o_ref[...] = jnp.dot(h, w_ref[...], preferred_element_type=F32)


def _ln_mod_matmul(x2d, modl, w_bf, *, tm, row_fn):
    rows, n = x2d.shape[0], w_bf.shape[1]
    return pl.pallas_call(
        _ln_mod_matmul_kernel,
        grid=(rows // tm,),
        in_specs=[pl.BlockSpec((tm, D_MODEL), lambda i: (i, 0)), _mod_spec(row_fn, 0), _mod_spec(row_fn, 1),
                  _const_spec((D_MODEL, n))],
        out_specs=pl.BlockSpec((tm, n), lambda i: (i, 0)),
        out_shape=jax.ShapeDtypeStruct((rows, n), F32),
        compiler_params=_cparams(1),
        name="ln_mod_matmul",
    )(x2d, modl, modl, w_bf)


def _s5_matrices(lam_re, lam_im, log_dt, b_re, b_im, c_re, c_im, d_skip):
    t_len, g_n, p_n, i_n = S5_T, S5_GROUPS, S5_STATE, S5_GC
    dt = jnp.exp(log_dt)[..., None]
    mag, th = lam_re * dt, lam_im * dt
    er = jnp.exp(mag)
    lbr, lbi = er * jnp.cos(th), er * jnp.sin(th)
    den = lam_re * lam_re + lam_im * lam_im
    nr, ni = lbr - 1.0, lbi
    cr = (nr * lam_re + ni * lam_im) / den
    ci = (ni * lam_re - nr * lam_im) / den
    bbr = cr[..., None] * b_re - ci[..., None] * b_im
    bbi = cr[..., None] * b_im + ci[..., None] * b_re
    tau = jnp.arange(t_len + 1, dtype=F32)[:, None, None, None]
    pm = jnp.exp(tau * mag)
    pr, pi = pm * jnp.cos(tau * th), pm * jnp.sin(tau * th)
    lbr_t = pr[..., None] * bbr - pi[..., None] * bbi
    lbi_t = pr[..., None] * bbi + pi[..., None] * bbr
    kern = (jnp.einsum('dgop,tdgpi->tdgoi', c_re, lbr_t, precision=HIGHEST)
            - jnp.einsum('dgop,tdgpi->tdgoi', c_im, lbi_t, precision=HIGHEST))
    s_idx = np.arange(t_len)[:, None]
    t_idx = np.arange(t_len)[None, :]
    fwd = jnp.where((t_idx >= s_idx)[..., None, None, None], kern[:, 0][np.clip(t_idx - s_idx, 0, None)], 0.0)
    bwd = jnp.where((s_idx >= t_idx)[..., None, None, None], kern[:, 1][np.clip(s_idx - t_idx, 0, None)], 0.0)
    skip = (jnp.asarray(np.eye(t_len), F32)[:, :, None, None, None]
            * (d_skip.reshape(g_n, i_n)[:, :, None] * jnp.asarray(np.eye(i_n), F32))[None, None])
    toep = fwd + bwd + skip
    m_mat = toep.transpose(2, 0, 4, 1, 3).reshape(g_n, t_len * i_n, t_len * i_n)

    def state_in(re, im, order):
        sel = lambda a: a[order].transpose(1, 0, 3, 2).reshape(g_n, t_len * i_n, p_n)
        return sel(re), sel(im)

    rev = np.arange(t_len - 1, -1, -1)
    wf_re, wf_im = state_in(lbr_t[:t_len, 0], lbi_t[:t_len, 0], rev)
    wb_re, wb_im = state_in(lbr_t[:t_len, 1], lbi_t[:t_len, 1], np.arange(t_len))
    w_state = jnp.stack([wf_re, wf_im, wb_re, wb_im], axis=2)
    eye2 = jnp.asarray(np.eye(2), F32)
    w_state = w_state.reshape(S5_PAIRS, 2, t_len * i_n, 4, p_n)
    w_state = jnp.einsum('jgkqp,gh->jgkqhp', w_state, eye2).reshape(S5_PAIRS, 2 * t_len * i_n, 4 * 2 * p_n)

    def state_out(d, powers):
        prd, pid = pr[powers, d], pi[powers, d]
        re = c_re[d][None] * prd[:, :, None, :] - c_im[d][None] * pid[:, :, None, :]
        im = c_re[d][None] * pid[:, :, None, :] + c_im[d][None] * prd[:, :, None, :]
        fl = lambda a: a.transpose(1, 3, 0, 2).reshape(g_n, p_n, t_len * i_n)
        return fl(re), fl(-im)

    vf_re, vf_im = state_out(0, np.arange(1, t_len + 1))
    vb_re, vb_im = state_out(1, np.arange(t_len, 0, -1))
    v_state = jnp.stack([vf_re, vf_im, vb_re, vb_im], axis=1)
    v_state = v_state.reshape(S5_PAIRS, 2, 4, p_n, t_len * i_n)
    v_state = jnp.einsum('jgqpk,gh->jqgphk', v_state, eye2).reshape(S5_PAIRS, 4 * 2 * p_n, 2 * t_len * i_n)
    a_pow = jnp.stack([pr[t_len, 0], pi[t_len, 0], pr[t_len, 1], pi[t_len, 1]], axis=0)
    a_pow = a_pow.reshape(4, g_n * p_n)
    m_pair = m_mat.reshape(S5_PAIRS, 2, t_len * i_n, t_len * i_n)
    return m_pair.astype(BF16), w_state.astype(BF16), v_state.astype(BF16), a_pow


def _s5_state_kernel(u_ref, w_ref, fr_ref, fi_ref, br_ref, bi_ref):
    st = jnp.dot(u_ref[...], w_ref[...], preferred_element_type=F32)
    for q, ref in enumerate((fr_ref, fi_ref, br_ref, bi_ref)):
        ref[...] = st[:, q * LANES:(q + 1) * LANES]


def _s5_scan_kernel(a_ref, fr_ref, fi_ref, br_ref, bi_ref, hfr_ref, hfi_ref, hbr_ref, hbi_ref,
                    *, batch, n_ctx, n_all):
    afr, afi, abr, abi = (a_ref[pl.ds(q, 1), :] for q in range(4))
    zero = jnp.zeros_like(afr)

    def body(k, carry):
        kb = jnp.where(k < n_ctx, n_ctx - 1 - k, n_all + n_ctx - 1 - k)
        out = []
        for b in range(batch):
            hr, hi, gr, gi = carry[4 * b:4 * b + 4]
            rf = pl.ds(b * n_all + k, 1)
            rb = pl.ds(b * n_all + kb, 1)
            hfr_ref[rf, :] = hr
            hfi_ref[rf, :] = hi
            hbr_ref[rb, :] = gr
            hbi_ref[rb, :] = gi
            out += [afr * hr - afi * hi + fr_ref[rf, :], afr * hi + afi * hr + fi_ref[rf, :],
                    abr * gr - abi * gi + br_ref[rb, :], abr * gi + abi * gr + bi_ref[rb, :]]
        return tuple(out)

    lax.fori_loop(0, n_all, body, (zero,) * (4 * batch))


def _s5_out_kernel(u_ref, m_ref, v_ref, hfr_ref, hfi_ref, hbr_ref, hbi_ref, y_ref):
    width = S5_T * S5_GC
    st = jnp.concatenate([r[...].astype(BF16) for r in (hfr_ref, hfi_ref, hbr_ref, hbi_ref)], axis=1)
    y = jnp.dot(st, v_ref[...], preferred_element_type=F32)
    for g in range(2):
        cols = slice(g * width, (g + 1) * width)
        y_ref[:, cols] = y[:, cols] + jnp.dot(u_ref[:, cols], m_ref[g], preferred_element_type=F32)


def _s5_mix(s_lat, s_ctx, mats, *, batch):
    m_pair, w_state, v_state, a_pow = mats
    n_lat = s_lat.shape[0] // batch // S5_T
    n_ctx = s_ctx.shape[0] // batch // S5_T
    n_all = n_lat + n_ctx
    width = S5_T * S5_GC

    def fold(s, n):
        return s.reshape(batch, n, S5_T, S5_GROUPS, S5_GC).transpose(0, 1, 3, 2, 4).reshape(
            batch, n, S5_GROUPS * width)

    u = jnp.concatenate([fold(s_ctx, n_ctx), fold(s_lat, n_lat)], axis=1).astype(BF16)
    rows = batch * n_all
    u = u.reshape(rows, S5_GROUPS * width)
    n_state = S5_GROUPS * S5_STATE
    u_spec = pl.BlockSpec((rows, 2 * width), lambda j: (0, j))
    st_spec = pl.BlockSpec((rows, LANES), lambda j: (0, j))
    st_shape = jax.ShapeDtypeStruct((rows, n_state), F32)
    s_in = pl.pallas_call(
        _s5_state_kernel,
        grid=(S5_PAIRS,),
        in_specs=[u_spec, pl.BlockSpec((None, 2 * width, 4 * LANES), lambda j: (j, 0, 0))],
        out_specs=[st_spec] * 4,
        out_shape=[st_shape] * 4,
        compiler_params=_cparams(1),
        name="s5_chunk_states",
    )(u, w_state)
    full = pl.BlockSpec((rows, S5_SCAN_LANES), lambda i: (0, i))
    h_prev = pl.pallas_call(
        functools.partial(_s5_scan_kernel, batch=batch, n_ctx=n_ctx, n_all=n_all),
        grid=(n_state // S5_SCAN_LANES,),
        in_specs=[pl.BlockSpec((4, S5_SCAN_LANES), lambda i: (0, i))] + [full] * 4,
        out_specs=[full] * 4,
        out_shape=[st_shape] * 4,
        compiler_params=_cparams(1),
        name="s5_chunk_scan",
    )(a_pow, *s_in)
    y = pl.pallas_call(
        _s5_out_kernel,
        grid=(S5_PAIRS,),
        in_specs=[u_spec, pl.BlockSpec((None, 2, width, width), lambda j: (j, 0, 0, 0)),
                  pl.BlockSpec((None, 4 * LANES, 2 * width), lambda j: (j, 0, 0))] + [st_spec] * 4,
        out_specs=u_spec,
        out_shape=jax.ShapeDtypeStruct((rows, S5_GROUPS * width), F32),
        compiler_params=_cparams(1),
        name="s5_output",
    )(u, m_pair, v_state, *h_prev)
    y = y.reshape(batch, n_all, S5_GROUPS, S5_T, S5_GC)[:, n_ctx:]
    return y.transpose(0, 1, 3, 2, 4).reshape(batch * n_lat * S5_T, S5_W)


def kernel(x, c, ctx, c_ctx, w_mod, b_mod, w_out, b_out, ln_mix_g, ln_mix_b, w_ffn1, b_ffn1, w_ffn2,
           b_ffn2, ln_ffn_g, ln_ffn_b, w_in_ab, lam_q1, lam_k1, lam_q2, lam_k2, subln_g, w_in_cd,
           s5_lam_re, s5_lam_im, s5_log_dt, s5_b_re, s5_b_im, s5_c_re, s5_c_im, s5_d, w_glu, b_glu,
           w_sp, b_sp):
    batch, n_lat, d = x.shape
    n_ctx = ctx.shape[1]
    assert d == D_MODEL and n_lat % ROW_TILE == 0 and n_ctx % CHUNK == 0 and batch + 1 <= MOD_ROWS
    mod = _modulation(c, c_ctx, w_mod, b_mod)
    xl = x.reshape(batch * n_lat, d)
    xc = ctx.reshape(batch * n_ctx, d)
    tiles = n_lat // ROW_TILE
    lat_rows = dict(tm=ROW_TILE, row_fn=lambda i: i // tiles)
    ctx_rows = dict(tm=n_ctx, row_fn=lambda i: batch)
    rope_tabs = _rope_tables(n_lat)
    for l in range(DEPTH):
        last = l == DEPTH - 1
        e = l // 2
        modl = mod[l]
        tail_w = (w_out[l].astype(BF16), b_out[l], ln_mix_g[l], ln_mix_b[l], w_ffn1[l].astype(BF16), b_ffn1[l],
                  w_ffn2[l].astype(BF16), b_ffn2[l], ln_ffn_g[l], ln_ffn_b[l])
        if l % 2 == 0:
            w_in = w_in_ab[e].astype(BF16)
            lam_init = 0.8 - 0.6 * math.exp(-0.3 * l)
            lam_vecs = (lam_q1[e], lam_k1[e], lam_q2[e], lam_k2[e])
            n_keys = n_lat + n_ctx
            f, q, k_all, vt_all = _inproj_even(xl, modl, w_in, rope_tabs, None, batch=batch, n_keys=n_keys,
                                               key_row0=0, **lat_rows)
            fc, qc, k_all, vt_all = _inproj_even(xc, modl, w_in, None, (k_all, vt_all), batch=batch, n_keys=n_keys,
                                                 key_row0=n_lat, **ctx_rows)
            attend = functools.partial(_diff_attention, lam_vecs=lam_vecs, subln_g=subln_g[e], lam_init=lam_init,
                                       batch=batch)
            ya = attend(q, k_all, vt_all, tq=ATTN_TQ, key_row0=0, n_keys=n_keys)
            yf = _fourier_mix(f, batch=batch)
            xl = _layer_tail(xl, yf, ya, modl, None, *tail_w, **lat_rows)
            if not last:
                yac = attend(qc, k_all, vt_all, tq=n_ctx, key_row0=n_lat, n_keys=n_ctx)
                yfc = _fourier_mix(fc, batch=batch)
                xc = _layer_tail(xc, yfc, yac, modl, None, *tail_w, **ctx_rows)
        else:
            assert last, "the S5 / gMLP layer is only implemented as the final layer"
            w_in = w_in_cd[e].astype(BF16)
            bsp_lanes = jnp.broadcast_to(b_sp[e][:, :, None], (GMLP_GROUPS, CHUNK, GMLP_GC))
            s_lat, gm = _inproj_odd(xl, modl, w_in, w_sp[e].astype(BF16), bsp_lanes, **lat_rows)
            s_ctx = _ln_mod_matmul(xc, modl, w_in[:, :S5_W], **ctx_rows)
            mats = _s5_matrices(s5_lam_re[e], s5_lam_im[e], s5_log_dt[e], s5_b_re[e], s5_b_im[e], s5_c_re[e],
                                s5_c_im[e], s5_d[e])
            ys = _s5_mix(s_lat, s_ctx, mats, batch=batch)
            xl = _layer_tail(xl, ys, gm, modl, (w_glu[e].astype(BF16), b_glu[e]), *tail_w, **lat_rows)
    return xl.reshape(batch, n_lat, d)
```

```python
import functools
import math

import numpy as np
import jax
import jax.numpy as jnp
from jax import lax
from jax.experimental import pallas as pl
from jax.experimental.pallas import tpu as pltpu

D_MODEL = 1024
DEPTH = 2
GRID_W = 64
FOURIER_W = D_MODEL // 4
FOURIER_GC = 64
DIFF_HD = 64
DIFF_VD = 2 * DIFF_HD
DIFF_HEADS = (D_MODEL - FOURIER_W) // DIFF_VD
QK_W = DIFF_HEADS * 2 * DIFF_HD
V_W = DIFF_HEADS * DIFF_VD
EVEN_IN_W = FOURIER_W + 2 * QK_W + V_W
DIFF_SCALE = DIFF_HD ** -0.5
ROPE_BASE = 10000.0
ROPE_FREQS = DIFF_HD // 4
S5_W = D_MODEL // 2
S5_GC = 16
S5_GROUPS = S5_W // S5_GC
S5_STATE = 64
GMLP_W = D_MODEL // 2
GMLP_GC = 128
GMLP_GROUPS = GMLP_W // GMLP_GC
CHUNK = 128
ODD_IN_W = S5_W + 2 * GMLP_W
FFN_W = 4 * D_MODEL
LN_EPS = 1e-5
ALPHA = (2 * DEPTH) ** 0.25

F32 = jnp.float32
BF16 = jnp.bfloat16
HIGHEST = lax.Precision.HIGHEST

LANES = 128
MOD_ROWS = 8
ROW_TILE = 512
ATTN_TQ = 512
ATTN_TK = 256
ATTN_SUM_ROWS = 16
FFT_BATCH = 4
S5_T = 16
S5_PAIRS = S5_GROUPS // 2
FFN_CHUNK = 1024
VMEM_LIMIT = 56 * 2 ** 20
Q_SCALE = DIFF_SCALE * math.log2(math.e)


def _cparams(n_axes, vmem=VMEM_LIMIT):
    return pltpu.CompilerParams(dimension_semantics=("arbitrary",) * n_axes, vmem_limit_bytes=vmem)


def _const_spec(shape):
    zeros = (0,) * len(shape)
    return pl.BlockSpec(shape, lambda *_: zeros, pipeline_mode=pl.Buffered(1))


def _ln(x):
    xc = x - jnp.mean(x, -1, keepdims=True)
    var = jnp.mean(xc * xc, -1, keepdims=True)
    return xc * lax.rsqrt(var + LN_EPS)


def _mm(a, b):
    return jnp.dot(a.astype(BF16), b.astype(BF16), preferred_element_type=F32)


def _mod_spec(row_fn, which):
    return pl.BlockSpec((None, 1, D_MODEL), lambda i: (row_fn(i) * 6 + which, 0, 0))


def _mod_kernel(c_ref, w_ref, b_ref, o_ref):
    c = c_ref[...]
    a = c * jax.nn.sigmoid(c)
    o_ref[...] = jnp.dot(a, w_ref[...], preferred_element_type=F32, precision=HIGHEST) + b_ref[...]


def _modulation(c, c_ctx, w_mod, b_mod):
    batch = c.shape[0]
    rows = jnp.concatenate([c, c_ctx[None], jnp.zeros((MOD_ROWS - batch - 1, D_MODEL), F32)], axis=0)
    out = pl.pallas_call(
        _mod_kernel,
        grid=(DEPTH, 6),
        in_specs=[
            pl.BlockSpec((MOD_ROWS, D_MODEL), lambda l, j: (0, 0)),
            pl.BlockSpec((None, D_MODEL, D_MODEL), lambda l, j: (l, 0, j)),
            pl.BlockSpec((None, 1, D_MODEL), lambda l, j: (l, 0, j)),
        ],
        out_specs=pl.BlockSpec((None, MOD_ROWS, D_MODEL), lambda l, j: (l, 0, j)),
        out_shape=jax.ShapeDtypeStruct((DEPTH, MOD_ROWS, 6 * D_MODEL), F32),
        compiler_params=_cparams(2),
        name="modulation",
    )(rows, w_mod, b_mod.reshape(DEPTH, 1, 6 * D_MODEL))
    return out.reshape(DEPTH, MOD_ROWS * 6, 1, D_MODEL)


def _inproj_even_kernel(*refs, rope):
    if rope:
        x_ref, sh_ref, sc_ref, w_ref, cos_ref, sin_ref, f_ref, q_ref, k_ref, vt_ref = refs
    else:
        x_ref, sh_ref, sc_ref, w_ref, _, _, f_ref, q_ref, k_ref, vt_ref = refs
    h = (_ln(x_ref[...]) * (1.0 + sc_ref[...]) + sh_ref[...]).astype(BF16)
    f_ref[...] = jnp.dot(h, w_ref[:, :FOURIER_W], preferred_element_type=F32)
    width = 2 * LANES
    if rope:
        cos = cos_ref[...]
        sin = sin_ref[...]
        lane = lax.broadcasted_iota(jnp.int32, cos.shape, 1)
        first_half = (lane & ROPE_FREQS) == 0
    for j in range(QK_W // width):
        for base, o_ref, scale in ((FOURIER_W, q_ref, Q_SCALE), (FOURIER_W + QK_W, k_ref, None)):
            t = jnp.dot(h, w_ref[:, base + j * width:base + (j + 1) * width], preferred_element_type=F32)
            if rope:
                partner = jnp.where(first_half, pltpu.roll(t, width - ROPE_FREQS, 1),
                                    pltpu.roll(t, ROPE_FREQS, 1))
                t = t * cos + partner * sin
            if scale is not None:
                t = t * scale
            o_ref[:, j * width:(j + 1) * width] = t.astype(BF16)
    v0 = FOURIER_W + 2 * QK_W
    for j in range(V_W // width):
        v = jnp.dot(h, w_ref[:, v0 + j * width:v0 + (j + 1) * width], preferred_element_type=F32)
        tk = vt_ref.shape[2]
        for c in range(vt_ref.shape[0]):
            vt_ref[c, j * width:(j + 1) * width, :] = v[c * tk:(c + 1) * tk].T.astype(BF16)


def _inproj_even(x2d, modl, w_bf, rope_tabs, kv_all, *, batch, n_keys, key_row0, tm, row_fn):
    rows = x2d.shape[0]
    rope = rope_tabs is not None
    tiles_per_batch = rows // batch // tm
    blk0 = key_row0 // tm
    in_specs = [
        pl.BlockSpec((tm, D_MODEL), lambda i: (i, 0)),
        _mod_spec(row_fn, 0),
        _mod_spec(row_fn, 1),
        _const_spec((D_MODEL, EVEN_IN_W)),
    ]
    args = [x2d, modl, modl, w_bf]
    if rope:
        in_specs += [pl.BlockSpec((tm, 2 * LANES), lambda i: (i % tiles_per_batch, 0))] * 2
        args += list(rope_tabs)
        aliases = {}
    else:
        in_specs += [pl.BlockSpec(memory_space=pl.ANY)] * 2
        args += list(kv_all)
        aliases = {4: 2, 5: 3}
    kv_idx = lambda i: (i // tiles_per_batch, blk0 + i % tiles_per_batch, 0)
    return pl.pallas_call(
        functools.partial(_inproj_even_kernel, rope=rope),
        grid=(rows // tm,),
        in_specs=in_specs,
        out_specs=[
            pl.BlockSpec((tm, FOURIER_W), lambda i: (i, 0)),
            pl.BlockSpec((tm, QK_W), lambda i: (i, 0)),
            pl.BlockSpec((None, tm, QK_W), kv_idx),
            pl.BlockSpec((None, tm // ATTN_TK, V_W, ATTN_TK), lambda i: kv_idx(i) + (0,)),
        ],
        out_shape=[
            jax.ShapeDtypeStruct((rows, FOURIER_W), F32),
            jax.ShapeDtypeStruct((rows, QK_W), BF16),
            jax.ShapeDtypeStruct((batch, n_keys, QK_W), BF16),
            jax.ShapeDtypeStruct((batch, n_keys // ATTN_TK, V_W, ATTN_TK), BF16),
        ],
        input_output_aliases=aliases,
        compiler_params=_cparams(1),
        name="inproj_even_rope" if rope else "inproj_even",
    )(*args)


def _rope_tables(n):
    rows = n // GRID_W
    row = jnp.repeat(jnp.arange(rows, dtype=F32), GRID_W)
    col = jnp.tile(jnp.arange(GRID_W, dtype=F32), rows)
    inv = jnp.power(ROPE_BASE, -jnp.arange(ROPE_FREQS, dtype=F32) / ROPE_FREQS)
    ang = jnp.stack([row[:, None] * inv, col[:, None] * inv], axis=1)
    cos, sin = jnp.cos(ang), jnp.sin(ang)
    cos_map = jnp.concatenate([cos, cos], axis=-1).reshape(n, 4 * ROPE_FREQS)
    sin_map = jnp.concatenate([-sin, sin], axis=-1).reshape(n, 4 * ROPE_FREQS)
    reps = 2 * LANES // (4 * ROPE_FREQS)
    return jnp.tile(cos_map, (1, reps)), jnp.tile(sin_map, (1, reps))


def _attn_kernel(lq1, lk1, lq2, lk2, g_ref, q_ref, k_ref, vt_ref, o_ref, acc_ref, sa_ref, sb_ref, ma_ref, mb_ref,
                 *, lam_init):
    tq = q_ref.shape[0]
    n_chunks, _, tk = vt_ref.shape
    q = q_ref[...]
    lane = lax.broadcasted_iota(jnp.int32, q.shape, 1)
    zero = jnp.zeros_like(q)
    q_cat = jnp.concatenate([jnp.where(lane < DIFF_HD, q, zero), jnp.where(lane >= DIFF_HD, q, zero)], axis=0)
    nt_dims = (((1,), (1,)), ((), ()))

    def scores(c, s_ref, mc_ref):
        kb = k_ref[pl.ds(pl.multiple_of(c * tk, tk), tk), :]
        s = lax.dot_general(kb, q_cat, nt_dims, preferred_element_type=F32)
        s_ref[...] = s
        mc_ref[...] = jnp.max(s, axis=0, keepdims=True)

    ones_rows = (lax.broadcasted_iota(jnp.int32, (ATTN_SUM_ROWS, tk), 0) == 0).astype(BF16)

    def softmax_pv(c, s_ref, mc_ref, m_prev):
        m = jnp.maximum(m_prev, mc_ref[...])
        alpha = jnp.exp2(m_prev - m)
        p = jnp.exp2(s_ref[...] - m).astype(BF16)
        vt1 = jnp.concatenate([vt_ref[c], ones_rows], axis=0)
        acc_ref[...] = acc_ref[...] * alpha + jnp.dot(vt1, p, preferred_element_type=F32)
        return m

    acc_ref[...] = jnp.zeros_like(acc_ref)
    scores(0, sa_ref, ma_ref)

    def body(j, m):
        c = 2 * j
        scores(c + 1, sb_ref, mb_ref)
        m = softmax_pv(c, sa_ref, ma_ref, m)
        scores(c + 2, sa_ref, ma_ref)
        return softmax_pv(c + 1, sb_ref, mb_ref, m)

    m = lax.fori_loop(0, (n_chunks - 1) // 2, body, jnp.full((1, 2 * tq), -jnp.inf, F32))
    softmax_pv(n_chunks - 1, sa_ref, ma_ref, m)

    lam = (jnp.exp(jnp.sum(lq1[...] * lk1[...], keepdims=True))
           - jnp.exp(jnp.sum(lq2[...] * lk2[...], keepdims=True)) + lam_init)
    on = acc_ref[:DIFF_VD, :] / acc_ref[DIFF_VD:DIFF_VD + 1, :]
    o = on[:, :tq] - lam * on[:, tq:]
    o = o * lax.rsqrt(jnp.mean(o * o, axis=0, keepdims=True) + LN_EPS)
    o_ref[...] = o.T * (g_ref[...] * (1.0 - lam_init))


def _diff_attention(q, k_all, vt_all, lam_vecs, subln_g, lam_init, *, batch, tq, key_row0, n_keys):
    rows = q.shape[0]
    nq = rows // batch // tq
    tk = vt_all.shape[3]
    n_chunks = n_keys // tk
    assert n_chunks % 2 == 1 and key_row0 % n_keys == 0
    kblk = key_row0 // n_keys
    vec_spec = pl.BlockSpec((1, DIFF_HD), lambda b, h, i: (0, 0))
    in_specs = [vec_spec] * 4 + [
        pl.BlockSpec((1, DIFF_VD), lambda b, h, i: (0, 0)),
        pl.BlockSpec((tq, DIFF_VD), lambda b, h, i: (b * nq + i, h)),
        pl.BlockSpec((None, n_keys, DIFF_VD), lambda b, h, i: (b, kblk, h)),
        pl.BlockSpec((None, n_chunks, DIFF_VD, tk), lambda b, h, i: (b, kblk, h, 0)),
    ]
    args = [v.reshape(1, DIFF_HD) for v in lam_vecs] + [subln_g.reshape(1, DIFF_VD), q, k_all, vt_all]
    return pl.pallas_call(
        functools.partial(_attn_kernel, lam_init=lam_init),
        grid=(batch, DIFF_HEADS, nq),
        in_specs=in_specs,
        out_specs=pl.BlockSpec((tq, DIFF_VD), lambda b, h, i: (b * nq + i, h)),
        out_shape=jax.ShapeDtypeStruct((rows, V_W), F32),
        scratch_shapes=[pltpu.VMEM((DIFF_VD + ATTN_SUM_ROWS, 2 * tq), F32), pltpu.VMEM((tk, 2 * tq), F32),
                        pltpu.VMEM((tk, 2 * tq), F32), pltpu.VMEM((1, 2 * tq), F32), pltpu.VMEM((1, 2 * tq), F32)],
        compiler_params=_cparams(3),
        name="diff_attention" if n_chunks > 1 else "diff_attention_ctx",
    )(*args)


def _dft_cos_sin(n):
    idx = np.outer(np.arange(n), np.arange(n)) % n
    ang = 2.0 * np.pi * idx / n
    return np.cos(ang), np.sin(ang)


def _channel_dft():
    c, s = _dft_cos_sin(FOURIER_GC)
    eye = np.eye(LANES // FOURIER_GC)
    return jnp.asarray(np.kron(eye, c), F32).astype(BF16), jnp.asarray(np.kron(eye, s), F32).astype(BF16)


def _fourier_dense_kernel(x_ref, cl_ref, sl_ref, bdc_ref, bds_ref, o_ref, *, norm):
    x = x_ref[...]
    pr = _mm(cl_ref[...], x)
    pi = -_mm(sl_ref[...], x)
    out = _mm(pr, bdc_ref[...]) + _mm(pi, bds_ref[...])
    o_ref[...] = out * norm


def _fourier_fft_kernel(x_ref, f1_ref, twc_ref, tws_ref, f3_ref, bdc_ref, bds_ref, o_ref, ur_ref, ui_ref,
                        *, n1, norm):
    n2 = LANES

    nb = FFT_BATCH
    lanes = lambda j: slice(j * LANES, (j + 1) * LANES)

    def stage1(ib, carry):
        i0 = ib * nb
        xs = jnp.concatenate([x_ref[pl.ds(i0 + j, n2, stride=n1), :] for j in range(nb)], axis=1)
        t = _mm(f1_ref[...], xs)
        for j in range(nb):
            tr, ti = t[:n2, lanes(j)], t[n2:, lanes(j)]
            r0 = pl.multiple_of((i0 + j) * n2, n2)
            c = twc_ref[pl.ds(r0, n2), :]
            s = tws_ref[pl.ds(r0, n2), :]
            ur_ref[pl.ds(r0, n2), :] = tr * c + ti * s
            ui_ref[pl.ds(r0, n2), :] = ti * c - tr * s
        return carry

    lax.fori_loop(0, n1 // nb, stage1, 0)

    def stage2(kb, carry):
        k0 = kb * nb
        u = jnp.concatenate(
            [jnp.concatenate([ur_ref[pl.ds(k0 + j, n1, stride=n2), :], ui_ref[pl.ds(k0 + j, n1, stride=n2), :]],
                             axis=0) for j in range(nb)], axis=1)
        p = _mm(f3_ref[...], u)
        for j in range(nb):
            ur_ref[pl.ds(k0 + j, n1, stride=n2), :] = p[:n1, lanes(j)]
            ui_ref[pl.ds(k0 + j, n1, stride=n2), :] = p[n1:, lanes(j)]
        return carry

    lax.fori_loop(0, n2 // nb, stage2, 0)

    rows = ROW_TILE

    def stage3(r, carry):
        r0 = pl.multiple_of(r * rows, rows)
        out = _mm(ur_ref[pl.ds(r0, rows), :], bdc_ref[...]) + _mm(ui_ref[pl.ds(r0, rows), :], bds_ref[...])
        o_ref[pl.ds(r0, rows), :] = out * norm
        return carry

    lax.fori_loop(0, (n1 * n2) // rows, stage3, 0)


def _fourier_mix(f2d, *, batch):
    rows = f2d.shape[0]
    n = rows // batch
    norm = 1.0 / math.sqrt(n * FOURIER_GC)
    bdc, bds = _channel_dft()
    blk = pl.BlockSpec((n, LANES), lambda b, j: (b, j))
    grid = (batch, FOURIER_W // LANES)
    out_shape = jax.ShapeDtypeStruct((rows, FOURIER_W), F32)
    mat = _const_spec((LANES, LANES))
    if n <= ROW_TILE:
        c, s = _dft_cos_sin(n)
        return pl.pallas_call(
            functools.partial(_fourier_dense_kernel, norm=norm),
            grid=grid,
            in_specs=[blk, _const_spec((n, n)), _const_spec((n, n)), mat, mat],
            out_specs=blk,
            out_shape=out_shape,
            compiler_params=_cparams(2),
            name="fourier_dense",
        )(f2d, jnp.asarray(c, F32).astype(BF16), jnp.asarray(s, F32).astype(BF16), bdc, bds)
    n2 = LANES
    n1 = n // n2
    c2, s2 = _dft_cos_sin(n2)
    f1 = jnp.asarray(np.concatenate([c2, -s2], axis=0), F32).astype(BF16)
    c1, s1 = _dft_cos_sin(n1)
    f3 = jnp.asarray(np.block([[c1, s1], [-s1, c1]]), F32).astype(BF16)
    tw_idx = np.outer(np.arange(n1), np.arange(n2)).reshape(-1)
    tw_ang = 2.0 * np.pi * tw_idx / n
    twc = jnp.broadcast_to(jnp.asarray(np.cos(tw_ang), F32)[:, None], (n, LANES))
    tws = jnp.broadcast_to(jnp.asarray(np.sin(tw_ang), F32)[:, None], (n, LANES))
    return pl.pallas_call(
        functools.partial(_fourier_fft_kernel, n1=n1, norm=norm),
        grid=grid,
        in_specs=[blk, _const_spec((2 * n2, n2)), _const_spec((n, LANES)), _const_spec((n, LANES)),
                  _const_spec((2 * n1, 2 * n1)), mat, mat],
        out_specs=blk,
        out_shape=out_shape,
        scratch_shapes=[pltpu.VMEM((n, LANES), F32), pltpu.VMEM((n, LANES), F32)],
        compiler_params=_cparams(2),
        name="fourier_fft",
    )(f2d, f1, twc, tws, f3, bdc, bds)


def _gelu_tanh(x):
    return 0.5 * x * (1.0 + jnp.tanh(math.sqrt(2.0 / math.pi) * (x + 0.044715 * (x * x * x))))


def _tail_kernel(*refs, glu, wa):
    if glu:
        x_ref, ya_ref, yb_ref, wg_ref, bg_ref, *rest = refs
    else:
        x_ref, ya_ref, yb_ref, *rest = refs
    (wo_ref, bo_ref, g1_ref, lg1_ref, lb1_ref, sh2_ref, sc2_ref, g2_ref,
     w1_ref, b1_ref, w2_ref, b2_ref, lg2_ref, lb2_ref, o_ref) = rest
    ya = ya_ref[...]
    if glu:
        g = _gelu_tanh(ya)
        ya = g * jax.nn.sigmoid(jnp.dot(g.astype(BF16), wg_ref[...], preferred_element_type=F32) + bg_ref[...])
    y = (jnp.dot(ya.astype(BF16), wo_ref[:wa, :], preferred_element_type=F32)
         + jnp.dot(yb_ref[...].astype(BF16), wo_ref[wa:, :], preferred_element_type=F32) + bo_ref[...])
    x1 = _ln(ALPHA * x_ref[...] + g1_ref[...] * y) * lg1_ref[...] + lb1_ref[...]
    h = (_ln(x1) * (1.0 + sc2_ref[...]) + sh2_ref[...]).astype(BF16)
    y2 = jnp.zeros_like(x1)
    for c in range(FFN_W // FFN_CHUNK):
        cols = slice(c * FFN_CHUNK, (c + 1) * FFN_CHUNK)
        a = jnp.maximum(jnp.dot(h, w1_ref[:, cols], preferred_element_type=F32) + b1_ref[:, cols], 0.0)
        y2 = y2 + jnp.dot((a * a).astype(BF16), w2_ref[cols, :], preferred_element_type=F32)
    y2 = y2 + b2_ref[...]
    o_ref[...] = _ln(ALPHA * x1 + g2_ref[...] * y2) * lg2_ref[...] + lb2_ref[...]


def _layer_tail(x2d, ya, yb, modl, glu_params, w_out, b_out, lg1, lb1, w1, b1, w2, b2, lg2, lb2, *, tm, row_fn):
    rows = x2d.shape[0]
    wa, wb = ya.shape[-1], yb.shape[1]
    glu = glu_params is not None
    row = lambda v: v.reshape(1, -1)
    vec = _const_spec((1, D_MODEL))
    if ya.ndim == 3:
        tiles_per_batch = rows // ya.shape[0] // tm
        ya_spec = pl.BlockSpec((None, tm, wa), lambda i: (i // tiles_per_batch, i % tiles_per_batch, 0))
    else:
        ya_spec = pl.BlockSpec((tm, wa), lambda i: (i, 0))
    in_specs = [pl.BlockSpec((tm, D_MODEL), lambda i: (i, 0)), ya_spec,
                pl.BlockSpec((tm, wb), lambda i: (i, 0))]
    args = [x2d, ya, yb]
    if glu:
        in_specs += [_const_spec((wa, wa)), _const_spec((1, wa))]
        args += [glu_params[0], row(glu_params[1])]
    in_specs += [_const_spec((D_MODEL, D_MODEL)), vec, _mod_spec(row_fn, 2), vec, vec,
                 _mod_spec(row_fn, 3), _mod_spec(row_fn, 4), _mod_spec(row_fn, 5),
                 _const_spec((D_MODEL, FFN_W)), _const_spec((1, FFN_W)), _const_spec((FFN_W, D_MODEL)), vec, vec, vec]
    args += [w_out, row(b_out), modl, row(lg1), row(lb1), modl, modl, modl,
             w1, row(b1), w2, row(b2), row(lg2), row(lb2)]
    return pl.pallas_call(
        functools.partial(_tail_kernel, glu=glu, wa=wa),
        grid=(rows // tm,),
        in_specs=in_specs,
        out_specs=pl.BlockSpec((tm, D_MODEL), lambda i: (i, 0)),
        out_shape=jax.ShapeDtypeStruct((rows, D_MODEL), F32),
        compiler_params=_cparams(1),
        name="layer_tail_glu" if glu else "layer_tail",
    )(*args)


def _inproj_odd_kernel(x_ref, sh_ref, sc_ref, w_ref, wsp_ref, bsp_ref, s_ref, gm_ref):
    h = (_ln(x_ref[...]) * (1.0 + sc_ref[...]) + sh_ref[...]).astype(BF16)
    s_ref[...] = jnp.dot(h, w_ref[:, :S5_W], preferred_element_type=F32)
    u = jnp.dot(h, w_ref[:, S5_W:S5_W + GMLP_W], preferred_element_type=F32)
    v = jnp.dot(h, w_ref[:, S5_W + GMLP_W:], preferred_element_type=F32)
    for ch in range(x_ref.shape[0] // CHUNK):
        rows = slice(ch * CHUNK, (ch + 1) * CHUNK)
        for g in range(GMLP_GROUPS):
            cols = slice(g * GMLP_GC, (g + 1) * GMLP_GC)
            vn = _ln(v[rows, cols]).astype(BF16)
            sp = jnp.dot(wsp_ref[g], vn, preferred_element_type=F32) + bsp_ref[g]
            gm_ref[rows, cols] = u[rows, cols] * sp


def _inproj_odd(x2d, modl, w_bf, wsp_bf, bsp_lanes, *, batch, n_rows, tm, row_fn):
    rows = x2d.shape[0]
    tiles_per_batch = rows // batch // tm
    return pl.pallas_call(
        _inproj_odd_kernel,
        grid=(rows // tm,),
        in_specs=[pl.BlockSpec((tm, D_MODEL), lambda i: (i, 0)), _mod_spec(row_fn, 0), _mod_spec(row_fn, 1),
                  _const_spec((D_MODEL, ODD_IN_W)), _const_spec((GMLP_GROUPS, CHUNK, CHUNK)),
                  _const_spec((GMLP_GROUPS, CHUNK, GMLP_GC))],
        out_specs=[pl.BlockSpec((None, tm, S5_W), lambda i: (i // tiles_per_batch, i % tiles_per_batch, 0)),
                   pl.BlockSpec((tm, GMLP_W), lambda i: (i, 0))],
        out_shape=[jax.ShapeDtypeStruct((batch, n_rows, S5_W), F32), jax.ShapeDtypeStruct((rows, GMLP_W), F32)],
        compiler_params=_cparams(1),
        name="inproj_odd_gmlp",
    )(x2d, modl, modl, w_bf, wsp_bf, bsp_lanes)


def _ln_mod_matmul_kernel(x_ref, sh_ref, sc_ref, w_ref, _, o_ref):
    h = (_ln(x_ref[...]) * (1.0 + sc_ref[...]) + sh_ref[...]).astype(BF16)
    o_ref[...] = jnp.dot(h, w_ref[...], preferred_element_type=F32)


def _ln_mod_matmul_into(x2d, modl, w_bf, dst, *, batch, row0, tm, row_fn):
    rows, n = x2d.shape[0], w_bf.shape[1]
    tiles_per_batch = rows // batch // tm
    blk0 = row0 // tm
    return pl.pallas_call(
        _ln_mod_matmul_kernel,
        grid=(rows // tm,),
        in_specs=[pl.BlockSpec((tm, D_MODEL), lambda i: (i, 0)), _mod_spec(row_fn, 0), _mod_spec(row_fn, 1),
                  _const_spec((D_MODEL, n)), pl.BlockSpec(memory_space=pl.ANY)],
        out_specs=pl.BlockSpec((None, tm, n), lambda i: (i // tiles_per_batch, blk0 + i % tiles_per_batch, 0)),
        out_shape=jax.ShapeDtypeStruct(dst.shape, dst.dtype),
        input_output_aliases={4: 0},
        compiler_params=_cparams(1),
        name="ln_mod_matmul",
    )(x2d, modl, modl, w_bf, dst)


def _s5_matrices(lam_re, lam_im, log_dt, b_re, b_im, c_re, c_im, d_skip):
    t_len, g_n, p_n, i_n = S5_T, S5_GROUPS, S5_STATE, S5_GC
    dt = jnp.exp(log_dt)[..., None]
    mag, th = lam_re * dt, lam_im * dt
    er = jnp.exp(mag)
    lbr, lbi = er * jnp.cos(th), er * jnp.sin(th)
    den = lam_re * lam_re + lam_im * lam_im
    nr, ni = lbr - 1.0, lbi
    cr = (nr * lam_re + ni * lam_im) / den
    ci = (ni * lam_re - nr * lam_im) / den
    bbr = cr[..., None] * b_re - ci[..., None] * b_im
    bbi = cr[..., None] * b_im + ci[..., None] * b_re
    tau = jnp.arange(t_len + 1, dtype=F32)[:, None, None, None]
    pm = jnp.exp(tau * mag)
    pr, pi = pm * jnp.cos(tau * th), pm * jnp.sin(tau * th)
    lbr_t = pr[..., None] * bbr - pi[..., None] * bbi
    lbi_t = pr[..., None] * bbi + pi[..., None] * bbr
    kern = (jnp.einsum('dgop,tdgpi->tdgoi', c_re, lbr_t, precision=HIGHEST)
            - jnp.einsum('dgop,tdgpi->tdgoi', c_im, lbi_t, precision=HIGHEST))
    s_idx = np.arange(t_len)[:, None]
    t_idx = np.arange(t_len)[None, :]
    fwd = jnp.where((t_idx >= s_idx)[..., None, None, None], kern[:, 0][np.clip(t_idx - s_idx, 0, None)], 0.0)
    bwd = jnp.where((s_idx >= t_idx)[..., None, None, None], kern[:, 1][np.clip(s_idx - t_idx, 0, None)], 0.0)
    skip = (jnp.asarray(np.eye(t_len), F32)[:, :, None, None, None]
            * (d_skip.reshape(g_n, i_n)[:, :, None] * jnp.asarray(np.eye(i_n), F32))[None, None])
    toep = fwd + bwd + skip
    m_mat = toep.transpose(2, 0, 4, 1, 3).reshape(g_n, t_len * i_n, t_len * i_n)

    def state_in(re, im, order):
        sel = lambda a: a[order].transpose(1, 0, 3, 2).reshape(g_n, t_len * i_n, p_n)
        return sel(re), sel(im)

    rev = np.arange(t_len - 1, -1, -1)
    wf_re, wf_im = state_in(lbr_t[:t_len, 0], lbi_t[:t_len, 0], rev)
    wb_re, wb_im = state_in(lbr_t[:t_len, 1], lbi_t[:t_len, 1], np.arange(t_len))
    w_state = jnp.stack([wf_re, wf_im, wb_re, wb_im], axis=2)
    eye2 = jnp.asarray(np.eye(2), F32)
    w_state = w_state.reshape(S5_PAIRS, 2, t_len * i_n, 4, p_n)
    w_state = jnp.einsum('jgkqp,gh->jgkqhp', w_state, eye2).reshape(S5_PAIRS, 2 * t_len * i_n, 4 * 2 * p_n)

    def state_out(d, powers):
        prd, pid = pr[powers, d], pi[powers, d]
        re = c_re[d][None] * prd[:, :, None, :] - c_im[d][None] * pid[:, :, None, :]
        im = c_re[d][None] * pid[:, :, None, :] + c_im[d][None] * prd[:, :, None, :]
        fl = lambda a: a.transpose(1, 3, 0, 2).reshape(g_n, p_n, t_len * i_n)
        return fl(re), fl(-im)

    vf_re, vf_im = state_out(0, np.arange(1, t_len + 1))
    vb_re, vb_im = state_out(1, np.arange(t_len, 0, -1))
    v_state = jnp.stack([vf_re, vf_im, vb_re, vb_im], axis=1)
    v_state = v_state.reshape(S5_PAIRS, 2, 4, p_n, t_len * i_n)
    v_state = jnp.einsum('jgqpk,gh->jqgphk', v_state, eye2).reshape(S5_PAIRS, 4 * 2 * p_n, 2 * t_len * i_n)
    a_pow = jnp.stack([pr[t_len, 0], pi[t_len, 0], pr[t_len, 1], pi[t_len, 1]], axis=0)
    a_pow = a_pow.reshape(4, g_n * p_n)
    m_pair = m_mat.reshape(S5_PAIRS, 2, t_len * i_n, t_len * i_n)
    return m_pair.astype(BF16), w_state.astype(BF16), v_state.astype(BF16), a_pow


def _s5_fold_perm():
    n_t, n_g = LANES // S5_GC, LANES // S5_GC
    src = np.arange(n_t * LANES)
    t_lo, g, i = src // LANES, (src % LANES) // S5_GC, src % S5_GC
    perm = np.zeros((n_t * LANES, n_g * LANES), np.float32)
    perm[src, g * LANES + t_lo * S5_GC + i] = 1.0
    return perm


def _s5_kernel(s_ref, pb_ref, pbt_ref, m_ref, w_ref, v_ref, a_ref, y_ref, u_scr, st_scr, yp_scr, *, n_lat, n_ctx):
    n_all = n_lat + n_ctx
    n_t = LANES // S5_GC
    width = S5_T * S5_GC
    blk = lambda j: slice(j * LANES, (j + 1) * LANES)
    for t_hi in range(S5_T // n_t):
        x = jnp.concatenate([s_ref[pl.ds(n_t * t_hi + t_lo, n_all, stride=S5_T), :].astype(BF16)
                             for t_lo in range(n_t)], axis=1)
        xp = jnp.dot(x, pb_ref[...], preferred_element_type=F32).astype(BF16)
        for g in range(n_t):
            u_scr[g, :, blk(t_hi)] = xp[:, blk(g)]

    n_pairs = w_ref.shape[0]
    for pr in range(n_pairs):
        u_pair = jnp.concatenate([u_scr[2 * pr], u_scr[2 * pr + 1]], axis=1)
        st = jnp.dot(u_pair, w_ref[pr], preferred_element_type=F32)
        for q in range(4):
            st_scr[q, :, blk(pr)] = st[:, blk(q)]

    afr, afi, abr, abi = (a_ref[pl.ds(q, 1), :] for q in range(4))

    def scan_step(k, carry):
        hr, hi, gr, gi = carry
        rf = pl.ds(jnp.where(k < n_ctx, k + n_lat, k - n_ctx), 1)
        rb = pl.ds(n_all - 1 - k, 1)
        sfr, sfi, sbr, sbi = st_scr[0, rf, :], st_scr[1, rf, :], st_scr[2, rb, :], st_scr[3, rb, :]
        st_scr[0, rf, :] = hr
        st_scr[1, rf, :] = hi
        st_scr[2, rb, :] = gr
        st_scr[3, rb, :] = gi
        return (afr * hr - afi * hi + sfr, afr * hi + afi * hr + sfi,
                abr * gr - abi * gi + sbr, abr * gi + abi * gr + sbi)

    lax.fori_loop(0, n_all, scan_step, (jnp.zeros_like(afr),) * 4)

    for pr in range(n_pairs):
        hst = jnp.concatenate([st_scr[q, :, blk(pr)].astype(BF16) for q in range(4)], axis=1)
        y_pair = jnp.dot(hst, v_ref[pr], preferred_element_type=F32)
        for g in range(2):
            y_g = (y_pair[:, g * width:(g + 1) * width]
                   + jnp.dot(u_scr[2 * pr + g], m_ref[pr, g], preferred_element_type=F32))
            for t_hi in range(S5_T // n_t):
                yp_scr[t_hi, :, blk(2 * pr + g)] = y_g[:, blk(t_hi)].astype(BF16)

    for t_hi in range(S5_T // n_t):
        z = jnp.dot(yp_scr[t_hi], pbt_ref[...], preferred_element_type=F32)
        for t_lo in range(n_t):
            y_ref[pl.ds(n_t * t_hi + t_lo, n_all, stride=S5_T), :] = z[:, blk(t_lo)]


def _s5_mix(s_all, mats, *, n_lat_rows):
    m_pair, w_state, v_state, a_pow = mats
    batch, n_rows, _ = s_all.shape
    n_lat = n_lat_rows // S5_T
    n_all = n_rows // S5_T
    width = S5_T * S5_GC
    n_blocks = S5_W // LANES
    pairs = S5_PAIRS // n_blocks
    perm = _s5_fold_perm()
    rows_spec = pl.BlockSpec((None, n_rows, LANES), lambda b, v: (b, 0, v))
    return pl.pallas_call(
        functools.partial(_s5_kernel, n_lat=n_lat, n_ctx=n_all - n_lat),
        grid=(batch, n_blocks),
        in_specs=[rows_spec, _const_spec(perm.shape), _const_spec(perm.shape),
                  pl.BlockSpec((pairs, 2, width, width), lambda b, v: (v, 0, 0, 0)),
                  pl.BlockSpec((pairs, 2 * width, 4 * LANES), lambda b, v: (v, 0, 0)),
                  pl.BlockSpec((pairs, 4 * LANES, 2 * width), lambda b, v: (v, 0, 0)),
                  pl.BlockSpec((4, pairs * LANES), lambda b, v: (0, v))],
        out_specs=rows_spec,
        out_shape=jax.ShapeDtypeStruct(s_all.shape, F32),
        scratch_shapes=[pltpu.VMEM((2 * pairs, n_all, width), BF16), pltpu.VMEM((4, n_all, pairs * LANES), F32),
                        pltpu.VMEM((S5_T * S5_GC // LANES, n_all, 2 * pairs * LANES), BF16)],
        compiler_params=_cparams(2),
        name="s5_mix",
    )(s_all, jnp.asarray(perm, BF16), jnp.asarray(perm.T, BF16), m_pair, w_state, v_state, a_pow)


def kernel(x, c, ctx, c_ctx, w_mod, b_mod, w_out, b_out, ln_mix_g, ln_mix_b, w_ffn1, b_ffn1, w_ffn2,
           b_ffn2, ln_ffn_g, ln_ffn_b, w_in_ab, lam_q1, lam_k1, lam_q2, lam_k2, subln_g, w_in_cd,
           s5_lam_re, s5_lam_im, s5_log_dt, s5_b_re, s5_b_im, s5_c_re, s5_c_im, s5_d, w_glu, b_glu,
           w_sp, b_sp):
    batch, n_lat, d = x.shape
    n_ctx = ctx.shape[1]
    assert d == D_MODEL and n_lat % ROW_TILE == 0 and n_ctx % CHUNK == 0 and batch + 1 <= MOD_ROWS
    mod = _modulation(c, c_ctx, w_mod, b_mod)
    xl = x.reshape(batch * n_lat, d)
    xc = ctx.reshape(batch * n_ctx, d)
    tiles = n_lat // ROW_TILE
    lat_rows = dict(tm=ROW_TILE, row_fn=lambda i: i // tiles)
    ctx_rows = dict(tm=n_ctx, row_fn=lambda i: batch)
    rope_tabs = _rope_tables(n_lat)
    for l in range(DEPTH):
        last = l == DEPTH - 1
        e = l // 2
        modl = mod[l]
        tail_w = (w_out[l].astype(BF16), b_out[l], ln_mix_g[l], ln_mix_b[l], w_ffn1[l].astype(BF16), b_ffn1[l],
                  w_ffn2[l].astype(BF16), b_ffn2[l], ln_ffn_g[l], ln_ffn_b[l])
        if l % 2 == 0:
            w_in = w_in_ab[e].astype(BF16)
            lam_init = 0.8 - 0.6 * math.exp(-0.3 * l)
            lam_vecs = (lam_q1[e], lam_k1[e], lam_q2[e], lam_k2[e])
            n_keys = n_lat + n_ctx
            f, q, k_all, vt_all = _inproj_even(xl, modl, w_in, rope_tabs, None, batch=batch, n_keys=n_keys,
                                               key_row0=0, **lat_rows)
            fc, qc, k_all, vt_all = _inproj_even(xc, modl, w_in, None, (k_all, vt_all), batch=batch, n_keys=n_keys,
                                                 key_row0=n_lat, **ctx_rows)
            attend = functools.partial(_diff_attention, lam_vecs=lam_vecs, subln_g=subln_g[e], lam_init=lam_init,
                                       batch=batch)
            ya = attend(q, k_all, vt_all, tq=ATTN_TQ, key_row0=0, n_keys=n_keys)
            yf = _fourier_mix(f, batch=batch)
            xl = _layer_tail(xl, yf, ya, modl, None, *tail_w, **lat_rows)
            if not last:
                yac = attend(qc, k_all, vt_all, tq=n_ctx, key_row0=n_lat, n_keys=n_ctx)
                yfc = _fourier_mix(fc, batch=batch)
                xc = _layer_tail(xc, yfc, yac, modl, None, *tail_w, **ctx_rows)
        else:
            assert last, "the S5 / gMLP layer is only implemented as the final layer"
            w_in = w_in_cd[e].astype(BF16)
            bsp_lanes = jnp.broadcast_to(b_sp[e][:, :, None], (GMLP_GROUPS, CHUNK, GMLP_GC))
            s_all, gm = _inproj_odd(xl, modl, w_in, w_sp[e].astype(BF16), bsp_lanes, batch=batch,
                                    n_rows=n_lat + n_ctx, **lat_rows)
            s_all = _ln_mod_matmul_into(xc, modl, w_in[:, :S5_W], s_all, batch=batch, row0=n_lat, **ctx_rows)
            mats = _s5_matrices(s5_lam_re[e], s5_lam_im[e], s5_log_dt[e], s5_b_re[e], s5_b_im[e], s5_c_re[e],
                                s5_c_im[e], s5_d[e])
            ys = _s5_mix(s_all, mats, n_lat_rows=n_lat)
            xl = _layer_tail(xl, ys, gm, modl, (w_glu[e].astype(BF16), b_glu[e]), *tail_w, **lat_rows)
    return xl.reshape(batch, n_lat, d)
```

```python
import functools
import math

import numpy as np
import jax
import jax.numpy as jnp
from jax import lax
from jax.experimental import pallas as pl
from jax.experimental.pallas import tpu as pltpu

D_MODEL = 1024
DEPTH = 2
GRID_W = 64
FOURIER_W = D_MODEL // 4
FOURIER_GC = 64
DIFF_HD = 64
DIFF_VD = 2 * DIFF_HD
DIFF_HEADS = (D_MODEL - FOURIER_W) // DIFF_VD
QK_W = DIFF_HEADS * 2 * DIFF_HD
V_W = DIFF_HEADS * DIFF_VD
EVEN_IN_W = FOURIER_W + 2 * QK_W + V_W
DIFF_SCALE = DIFF_HD ** -0.5
ROPE_BASE = 10000.0
ROPE_FREQS = DIFF_HD // 4
S5_W = D_MODEL // 2
S5_GC = 16
S5_GROUPS = S5_W // S5_GC
S5_STATE = 64
GMLP_W = D_MODEL // 2
GMLP_GC = 128
GMLP_GROUPS = GMLP_W // GMLP_GC
CHUNK = 128
ODD_IN_W = S5_W + 2 * GMLP_W
FFN_W = 4 * D_MODEL
LN_EPS = 1e-5
ALPHA = (2 * DEPTH) ** 0.25

F32 = jnp.float32
BF16 = jnp.bfloat16
HIGHEST = lax.Precision.HIGHEST

LANES = 128
MOD_ROWS = 8
ROW_TILE = 512
ATTN_TQ = 1024
ATTN_TK = 256
ATTN_GROUP = 3
ATTN_SUM_ROWS = 16
FFT_BATCH = 4
S5_T = 16
S5_PAIRS = S5_GROUPS // 2
FFN_CHUNK = 1024
VMEM_LIMIT = 56 * 2 ** 20
Q_SCALE = DIFF_SCALE * math.log2(math.e)


def _cparams(n_axes, vmem=VMEM_LIMIT):
    return pltpu.CompilerParams(dimension_semantics=("arbitrary",) * n_axes, vmem_limit_bytes=vmem)


def _const_spec(shape):
    zeros = (0,) * len(shape)
    return pl.BlockSpec(shape, lambda *_: zeros, pipeline_mode=pl.Buffered(1))


def _ln(x):
    xc = x - jnp.mean(x, -1, keepdims=True)
    var = jnp.mean(xc * xc, -1, keepdims=True)
    return xc * lax.rsqrt(var + LN_EPS)


def _mm(a, b):
    return jnp.dot(a.astype(BF16), b.astype(BF16), preferred_element_type=F32)


def _mod_spec(row_fn, which):
    return pl.BlockSpec((None, 1, D_MODEL), lambda i: (row_fn(i) * 6 + which, 0, 0))


def _mod_kernel(c_ref, w_ref, b_ref, o_ref):
    c = c_ref[...]
    a = c * jax.nn.sigmoid(c)
    o_ref[...] = jnp.dot(a, w_ref[...], preferred_element_type=F32, precision=HIGHEST) + b_ref[...]


def _modulation(c, c_ctx, w_mod, b_mod):
    batch = c.shape[0]
    rows = jnp.concatenate([c, c_ctx[None], jnp.zeros((MOD_ROWS - batch - 1, D_MODEL), F32)], axis=0)
    out = pl.pallas_call(
        _mod_kernel,
        grid=(DEPTH, 6),
        in_specs=[
            pl.BlockSpec((MOD_ROWS, D_MODEL), lambda l, j: (0, 0)),
            pl.BlockSpec((None, D_MODEL, D_MODEL), lambda l, j: (l, 0, j)),
            pl.BlockSpec((None, 1, D_MODEL), lambda l, j: (l, 0, j)),
        ],
        out_specs=pl.BlockSpec((None, MOD_ROWS, D_MODEL), lambda l, j: (l, 0, j)),
        out_shape=jax.ShapeDtypeStruct((DEPTH, MOD_ROWS, 6 * D_MODEL), F32),
        compiler_params=_cparams(2),
        name="modulation",
    )(rows, w_mod, b_mod.reshape(DEPTH, 1, 6 * D_MODEL))
    return out.reshape(DEPTH, MOD_ROWS * 6, 1, D_MODEL)


def _inproj_even_kernel(*refs, rope):
    if rope:
        x_ref, sh_ref, sc_ref, w_ref, cos_ref, sin_ref, f_ref, q_ref, k_ref, vt_ref = refs
    else:
        x_ref, sh_ref, sc_ref, w_ref, _, _, f_ref, q_ref, k_ref, vt_ref = refs
    h = (_ln(x_ref[...]) * (1.0 + sc_ref[...]) + sh_ref[...]).astype(BF16)
    f_ref[...] = jnp.dot(h, w_ref[:, :FOURIER_W], preferred_element_type=F32)
    width = 2 * LANES
    if rope:
        cos = cos_ref[...]
        sin = sin_ref[...]
        lane = lax.broadcasted_iota(jnp.int32, cos.shape, 1)
        first_half = (lane & ROPE_FREQS) == 0
    for j in range(QK_W // width):
        for base, o_ref, scale in ((FOURIER_W, q_ref, Q_SCALE), (FOURIER_W + QK_W, k_ref, None)):
            t = jnp.dot(h, w_ref[:, base + j * width:base + (j + 1) * width], preferred_element_type=F32)
            if rope:
                partner = jnp.where(first_half, pltpu.roll(t, width - ROPE_FREQS, 1),
                                    pltpu.roll(t, ROPE_FREQS, 1))
                t = t * cos + partner * sin
            if scale is not None:
                t = t * scale
            o_ref[:, j * width:(j + 1) * width] = t.astype(BF16)
    v0 = FOURIER_W + 2 * QK_W
    for j in range(V_W // width):
        v = jnp.dot(h, w_ref[:, v0 + j * width:v0 + (j + 1) * width], preferred_element_type=F32)
        tk = vt_ref.shape[2]
        for c in range(vt_ref.shape[0]):
            vt_ref[c, j * width:(j + 1) * width, :] = v[c * tk:(c + 1) * tk].T.astype(BF16)


def _inproj_even(x2d, modl, w_bf, rope_tabs, kv_all, *, batch, n_keys, key_row0, tm, row_fn):
    rows = x2d.shape[0]
    rope = rope_tabs is not None
    tiles_per_batch = rows // batch // tm
    blk0 = key_row0 // tm
    in_specs = [
        pl.BlockSpec((tm, D_MODEL), lambda i: (i, 0)),
        _mod_spec(row_fn, 0),
        _mod_spec(row_fn, 1),
        _const_spec((D_MODEL, EVEN_IN_W)),
    ]
    args = [x2d, modl, modl, w_bf]
    if rope:
        in_specs += [pl.BlockSpec((tm, 2 * LANES), lambda i: (i % tiles_per_batch, 0))] * 2
        args += list(rope_tabs)
        aliases = {}
    else:
        in_specs += [pl.BlockSpec(memory_space=pl.ANY)] * 2
        args += list(kv_all)
        aliases = {4: 2, 5: 3}
    kv_idx = lambda i: (i // tiles_per_batch, blk0 + i % tiles_per_batch, 0)
    return pl.pallas_call(
        functools.partial(_inproj_even_kernel, rope=rope),
        grid=(rows // tm,),
        in_specs=in_specs,
        out_specs=[
            pl.BlockSpec((tm, FOURIER_W), lambda i: (i, 0)),
            pl.BlockSpec((tm, QK_W), lambda i: (i, 0)),
            pl.BlockSpec((None, tm, QK_W), kv_idx),
            pl.BlockSpec((None, tm // ATTN_TK, V_W, ATTN_TK), lambda i: kv_idx(i) + (0,)),
        ],
        out_shape=[
            jax.ShapeDtypeStruct((rows, FOURIER_W), F32),
            jax.ShapeDtypeStruct((rows, QK_W), BF16),
            jax.ShapeDtypeStruct((batch, n_keys, QK_W), BF16),
            jax.ShapeDtypeStruct((batch, n_keys // ATTN_TK, V_W, ATTN_TK), BF16),
        ],
        input_output_aliases=aliases,
        compiler_params=_cparams(1),
        name="inproj_even_rope" if rope else "inproj_even",
    )(*args)


def _rope_tables(n):
    rows = n // GRID_W
    row = jnp.repeat(jnp.arange(rows, dtype=F32), GRID_W)
    col = jnp.tile(jnp.arange(GRID_W, dtype=F32), rows)
    inv = jnp.power(ROPE_BASE, -jnp.arange(ROPE_FREQS, dtype=F32) / ROPE_FREQS)
    ang = jnp.stack([row[:, None] * inv, col[:, None] * inv], axis=1)
    cos, sin = jnp.cos(ang), jnp.sin(ang)
    cos_map = jnp.concatenate([cos, cos], axis=-1).reshape(n, 4 * ROPE_FREQS)
    sin_map = jnp.concatenate([-sin, sin], axis=-1).reshape(n, 4 * ROPE_FREQS)
    reps = 2 * LANES // (4 * ROPE_FREQS)
    return jnp.tile(cos_map, (1, reps)), jnp.tile(sin_map, (1, reps))


def _attn_kernel(lq1, lk1, lq2, lk2, g_ref, q_ref, k_ref, vt_ref, o_ref, acc_ref, *scratch, lam_init):
    tq = q_ref.shape[0]
    n_chunks, _, tk = vt_ref.shape
    q = q_ref[...]
    lane = lax.broadcasted_iota(jnp.int32, q.shape, 1)
    zero = jnp.zeros_like(q)
    q_cat = jnp.concatenate([jnp.where(lane < DIFF_HD, q, zero), jnp.where(lane >= DIFF_HD, q, zero)], axis=0)
    nt_dims = (((1,), (1,)), ((), ()))

    grp = len(scratch) // 4
    s_refs, mc_refs, p_refs, al_refs = (scratch[i * grp:(i + 1) * grp] for i in range(4))
    n_groups = n_chunks // grp

    def scores(c, r):
        kb = k_ref[pl.ds(pl.multiple_of(c * tk, tk), tk), :]
        s = lax.dot_general(kb, q_cat, nt_dims, preferred_element_type=F32)
        s_refs[r][...] = s
        mc_refs[r][...] = jnp.max(s, axis=0, keepdims=True)

    def softmax(r, m_prev):
        m = jnp.maximum(m_prev, mc_refs[r][...])
        al_refs[r][...] = jnp.exp2(m_prev - m)
        p_refs[r][...] = jnp.exp2(s_refs[r][...] - m).astype(BF16)
        return m

    ones_rows = (lax.broadcasted_iota(jnp.int32, (ATTN_SUM_ROWS, tk), 0) == 0).astype(BF16)

    def values(c, r):
        vt1 = jnp.concatenate([vt_ref[c], ones_rows], axis=0)
        acc_ref[...] = acc_ref[...] * al_refs[r][...] + jnp.dot(vt1, p_refs[r][...], preferred_element_type=F32)

    acc_ref[...] = jnp.zeros_like(acc_ref)
    m = jnp.full((1, 2 * tq), -jnp.inf, F32)
    for r in range(grp):
        scores(r, r)
    for r in range(grp):
        m = softmax(r, m)
        if n_groups > 1:
            scores(grp + r, r)

    def body(g, m):
        for r in range(grp):
            values((g - 1) * grp + r, r)
            m = softmax(r, m)
            scores((g + 1) * grp + r, r)
        return m

    if n_groups > 2:
        m = lax.fori_loop(1, n_groups - 1, body, m)
    if n_groups > 1:
        for r in range(grp):
            values((n_groups - 2) * grp + r, r)
            m = softmax(r, m)
    for r in range(grp):
        values((n_groups - 1) * grp + r, r)

    lam = (jnp.exp(jnp.sum(lq1[...] * lk1[...], keepdims=True))
           - jnp.exp(jnp.sum(lq2[...] * lk2[...], keepdims=True)) + lam_init)
    on = acc_ref[:DIFF_VD, :] / acc_ref[DIFF_VD:DIFF_VD + 1, :]
    o = on[:, :tq] - lam * on[:, tq:]
    o = o * lax.rsqrt(jnp.mean(o * o, axis=0, keepdims=True) + LN_EPS)
    o_ref[...] = o.T * (g_ref[...] * (1.0 - lam_init))


def _diff_attention(q, k_all, vt_all, lam_vecs, subln_g, lam_init, *, batch, tq, key_row0, n_keys):
    rows = q.shape[0]
    nq = rows // batch // tq
    tk = vt_all.shape[3]
    n_chunks = n_keys // tk
    assert key_row0 % n_keys == 0
    kblk = key_row0 // n_keys
    grp = ATTN_GROUP if n_chunks % ATTN_GROUP == 0 else 1
    cols = 2 * tq
    slot_shapes = [((tk, cols), F32), ((1, cols), F32), ((tk, cols), BF16), ((1, cols), F32)]
    vec_spec = pl.BlockSpec((1, DIFF_HD), lambda b, h, i: (0, 0))
    in_specs = [vec_spec] * 4 + [
        pl.BlockSpec((1, DIFF_VD), lambda b, h, i: (0, 0)),
        pl.BlockSpec((tq, DIFF_VD), lambda b, h, i: (b * nq + i, h)),
        pl.BlockSpec((None, n_keys, DIFF_VD), lambda b, h, i: (b, kblk, h)),
        pl.BlockSpec((None, n_chunks, DIFF_VD, tk), lambda b, h, i: (b, kblk, h, 0)),
    ]
    args = [v.reshape(1, DIFF_HD) for v in lam_vecs] + [subln_g.reshape(1, DIFF_VD), q, k_all, vt_all]
    return pl.pallas_call(
        functools.partial(_attn_kernel, lam_init=lam_init),
        grid=(batch, DIFF_HEADS, nq),
        in_specs=in_specs,
        out_specs=pl.BlockSpec((tq, DIFF_VD), lambda b, h, i: (b * nq + i, h)),
        out_shape=jax.ShapeDtypeStruct((rows, V_W), F32),
        scratch_shapes=[pltpu.VMEM((DIFF_VD + ATTN_SUM_ROWS, cols), F32)]
        + [pltpu.VMEM(shape, dtype) for shape, dtype in slot_shapes for _ in range(grp)],
        compiler_params=_cparams(3),
        name="diff_attention" if n_chunks > 1 else "diff_attention_ctx",
    )(*args)


def _dft_cos_sin(n):
    idx = np.outer(np.arange(n), np.arange(n)) % n
    ang = 2.0 * np.pi * idx / n
    return np.cos(ang), np.sin(ang)


def _channel_dft():
    c, s = _dft_cos_sin(FOURIER_GC)
    eye = np.eye(LANES // FOURIER_GC)
    return jnp.asarray(np.kron(eye, c), F32).astype(BF16), jnp.asarray(np.kron(eye, s), F32).astype(BF16)


def _fourier_dense_kernel(x_ref, cl_ref, sl_ref, bdc_ref, bds_ref, o_ref, *, norm):
    x = x_ref[...]
    pr = _mm(cl_ref[...], x)
    pi = -_mm(sl_ref[...], x)
    out = _mm(pr, bdc_ref[...]) + _mm(pi, bds_ref[...])
    o_ref[...] = out * norm


def _fourier_fft_kernel(x_ref, f1_ref, twc_ref, tws_ref, f3_ref, bdc_ref, bds_ref, o_ref, ur_ref, ui_ref,
                        *, n1, norm):
    n2 = LANES

    nb = FFT_BATCH
    lanes = lambda j: slice(j * LANES, (j + 1) * LANES)

    def stage1(ib, carry):
        i0 = ib * nb
        xs = jnp.concatenate([x_ref[pl.ds(i0 + j, n2, stride=n1), :] for j in range(nb)], axis=1)
        t = _mm(f1_ref[...], xs)
        for j in range(nb):
            tr, ti = t[:n2, lanes(j)], t[n2:, lanes(j)]
            r0 = pl.multiple_of((i0 + j) * n2, n2)
            c = twc_ref[pl.ds(r0, n2), :]
            s = tws_ref[pl.ds(r0, n2), :]
            ur_ref[pl.ds(r0, n2), :] = tr * c + ti * s
            ui_ref[pl.ds(r0, n2), :] = ti * c - tr * s
        return carry

    lax.fori_loop(0, n1 // nb, stage1, 0)

    def stage2(kb, carry):
        k0 = kb * nb
        u = jnp.concatenate(
            [jnp.concatenate([ur_ref[pl.ds(k0 + j, n1, stride=n2), :], ui_ref[pl.ds(k0 + j, n1, stride=n2), :]],
                             axis=0) for j in range(nb)], axis=1)
        p = _mm(f3_ref[...], u)
        for j in range(nb):
            ur_ref[pl.ds(k0 + j, n1, stride=n2), :] = p[:n1, lanes(j)]
            ui_ref[pl.ds(k0 + j, n1, stride=n2), :] = p[n1:, lanes(j)]
        return carry

    lax.fori_loop(0, n2 // nb, stage2, 0)

    rows = ROW_TILE

    def stage3(r, carry):
        r0 = pl.multiple_of(r * rows, rows)
        out = _mm(ur_ref[pl.ds(r0, rows), :], bdc_ref[...]) + _mm(ui_ref[pl.ds(r0, rows), :], bds_ref[...])
        o_ref[pl.ds(r0, rows), :] = out * norm
        return carry

    lax.fori_loop(0, (n1 * n2) // rows, stage3, 0)


def _fourier_mix(f2d, *, batch):
    rows = f2d.shape[0]
    n = rows // batch
    norm = 1.0 / math.sqrt(n * FOURIER_GC)
    bdc, bds = _channel_dft()
    blk = pl.BlockSpec((n, LANES), lambda b, j: (b, j))
    grid = (batch, FOURIER_W // LANES)
    out_shape = jax.ShapeDtypeStruct((rows, FOURIER_W), F32)
    mat = _const_spec((LANES, LANES))
    if n <= ROW_TILE:
        c, s = _dft_cos_sin(n)
        return pl.pallas_call(
            functools.partial(_fourier_dense_kernel, norm=norm),
            grid=grid,
            in_specs=[blk, _const_spec((n, n)), _const_spec((n, n)), mat, mat],
            out_specs=blk,
            out_shape=out_shape,
            compiler_params=_cparams(2),
            name="fourier_dense",
        )(f2d, jnp.asarray(c, F32).astype(BF16), jnp.asarray(s, F32).astype(BF16), bdc, bds)
    n2 = LANES
    n1 = n // n2
    c2, s2 = _dft_cos_sin(n2)
    f1 = jnp.asarray(np.concatenate([c2, -s2], axis=0), F32).astype(BF16)
    c1, s1 = _dft_cos_sin(n1)
    f3 = jnp.asarray(np.block([[c1, s1], [-s1, c1]]), F32).astype(BF16)
    tw_idx = np.outer(np.arange(n1), np.arange(n2)).reshape(-1)
    tw_ang = 2.0 * np.pi * tw_idx / n
    twc = jnp.broadcast_to(jnp.asarray(np.cos(tw_ang), F32)[:, None], (n, LANES))
    tws = jnp.broadcast_to(jnp.asarray(np.sin(tw_ang), F32)[:, None], (n, LANES))
    return pl.pallas_call(
        functools.partial(_fourier_fft_kernel, n1=n1, norm=norm),
        grid=grid,
        in_specs=[blk, _const_spec((2 * n2, n2)), _const_spec((n, LANES)), _const_spec((n, LANES)),
                  _const_spec((2 * n1, 2 * n1)), mat, mat],
        out_specs=blk,
        out_shape=out_shape,
        scratch_shapes=[pltpu.VMEM((n, LANES), F32), pltpu.VMEM((n, LANES), F32)],
        compiler_params=_cparams(2),
        name="fourier_fft",
    )(f2d, f1, twc, tws, f3, bdc, bds)


def _gelu_tanh(x):
    return 0.5 * x * (1.0 + jnp.tanh(math.sqrt(2.0 / math.pi) * (x + 0.044715 * (x * x * x))))


def _tail_kernel(*refs, glu, wa):
    if glu:
        x_ref, ya_ref, yb_ref, wg_ref, bg_ref, *rest = refs
    else:
        x_ref, ya_ref, yb_ref, *rest = refs
    (wo_ref, bo_ref, g1_ref, lg1_ref, lb1_ref, sh2_ref, sc2_ref, g2_ref,
     w1_ref, b1_ref, w2_ref, b2_ref, lg2_ref, lb2_ref, o_ref) = rest
    ya = ya_ref[...]
    if glu:
        g = _gelu_tanh(ya)
        ya = g * jax.nn.sigmoid(jnp.dot(g.astype(BF16), wg_ref[...], preferred_element_type=F32) + bg_ref[...])
    y = (jnp.dot(ya.astype(BF16), wo_ref[:wa, :], preferred_element_type=F32)
         + jnp.dot(yb_ref[...].astype(BF16), wo_ref[wa:, :], preferred_element_type=F32) + bo_ref[...])
    x1 = _ln(ALPHA * x_ref[...] + g1_ref[...] * y) * lg1_ref[...] + lb1_ref[...]
    h = (_ln(x1) * (1.0 + sc2_ref[...]) + sh2_ref[...]).astype(BF16)
    y2 = jnp.zeros_like(x1)
    for c in range(FFN_W // FFN_CHUNK):
        cols = slice(c * FFN_CHUNK, (c + 1) * FFN_CHUNK)
        a = jnp.maximum(jnp.dot(h, w1_ref[:, cols], preferred_element_type=F32) + b1_ref[:, cols], 0.0)
        y2 = y2 + jnp.dot((a * a).astype(BF16), w2_ref[cols, :], preferred_element_type=F32)
    y2 = y2 + b2_ref[...]
    o_ref[...] = _ln(ALPHA * x1 + g2_ref[...] * y2) * lg2_ref[...] + lb2_ref[...]


def _layer_tail(x2d, ya, yb, modl, glu_params, w_out, b_out, lg1, lb1, w1, b1, w2, b2, lg2, lb2, *, tm, row_fn):
    rows = x2d.shape[0]
    wa, wb = ya.shape[-1], yb.shape[1]
    glu = glu_params is not None
    row = lambda v: v.reshape(1, -1)
    vec = _const_spec((1, D_MODEL))
    if ya.ndim == 3:
        tiles_per_batch = rows // ya.shape[0] // tm
        ya_spec = pl.BlockSpec((None, tm, wa), lambda i: (i // tiles_per_batch, i % tiles_per_batch, 0))
    else:
        ya_spec = pl.BlockSpec((tm, wa), lambda i: (i, 0))
    in_specs = [pl.BlockSpec((tm, D_MODEL), lambda i: (i, 0)), ya_spec,
                pl.BlockSpec((tm, wb), lambda i: (i, 0))]
    args = [x2d, ya, yb]
    if glu:
        in_specs += [_const_spec((wa, wa)), _const_spec((1, wa))]
        args += [glu_params[0], row(glu_params[1])]
    in_specs += [_const_spec((D_MODEL, D_MODEL)), vec, _mod_spec(row_fn, 2), vec, vec,
                 _mod_spec(row_fn, 3), _mod_spec(row_fn, 4), _mod_spec(row_fn, 5),
                 _const_spec((D_MODEL, FFN_W)), _const_spec((1, FFN_W)), _const_spec((FFN_W, D_MODEL)), vec, vec, vec]
    args += [w_out, row(b_out), modl, row(lg1), row(lb1), modl, modl, modl,
             w1, row(b1), w2, row(b2), row(lg2), row(lb2)]
    return pl.pallas_call(
        functools.partial(_tail_kernel, glu=glu, wa=wa),
        grid=(rows // tm,),
        in_specs=in_specs,
        out_specs=pl.BlockSpec((tm, D_MODEL), lambda i: (i, 0)),
        out_shape=jax.ShapeDtypeStruct((rows, D_MODEL), F32),
        compiler_params=_cparams(1),
        name="layer_tail_glu" if glu else "layer_tail",
    )(*args)


def _inproj_odd_kernel(x_ref, sh_ref, sc_ref, w_ref, wsp_ref, bsp_ref, s_ref, gm_ref):
    h = (_ln(x_ref[...]) * (1.0 + sc_ref[...]) + sh_ref[...]).astype(BF16)
    s_ref[...] = jnp.dot(h, w_ref[:, :S5_W], preferred_element_type=F32)
    u = jnp.dot(h, w_ref[:, S5_W:S5_W + GMLP_W], preferred_element_type=F32)
    v = jnp.dot(h, w_ref[:, S5_W + GMLP_W:], preferred_element_type=F32)
    for ch in range(x_ref.shape[0] // CHUNK):
        rows = slice(ch * CHUNK, (ch + 1) * CHUNK)
        for g in range(GMLP_GROUPS):
            cols = slice(g * GMLP_GC, (g + 1) * GMLP_GC)
            vn = _ln(v[rows, cols]).astype(BF16)
            sp = jnp.dot(wsp_ref[g], vn, preferred_element_type=F32) + bsp_ref[g]
            gm_ref[rows, cols] = u[rows, cols] * sp


def _inproj_odd(x2d, modl, w_bf, wsp_bf, bsp_lanes, *, batch, n_rows, tm, row_fn):
    rows = x2d.shape[0]
    tiles_per_batch = rows // batch // tm
    return pl.pallas_call(
        _inproj_odd_kernel,
        grid=(rows // tm,),
        in_specs=[pl.BlockSpec((tm, D_MODEL), lambda i: (i, 0)), _mod_spec(row_fn, 0), _mod_spec(row_fn, 1),
                  _const_spec((D_MODEL, ODD_IN_W)), _const_spec((GMLP_GROUPS, CHUNK, CHUNK)),
                  _const_spec((GMLP_GROUPS, CHUNK, GMLP_GC))],
        out_specs=[pl.BlockSpec((None, tm, S5_W), lambda i: (i // tiles_per_batch, i % tiles_per_batch, 0)),
                   pl.BlockSpec((tm, GMLP_W), lambda i: (i, 0))],
        out_shape=[jax.ShapeDtypeStruct((batch, n_rows, S5_W), F32), jax.ShapeDtypeStruct((rows, GMLP_W), F32)],
        compiler_params=_cparams(1),
        name="inproj_odd_gmlp",
    )(x2d, modl, modl, w_bf, wsp_bf, bsp_lanes)


def _ln_mod_matmul_kernel(x_ref, sh_ref, sc_ref, w_ref, _, o_ref):
    h = (_ln(x_ref[...]) * (1.0 + sc_ref[...]) + sh_ref[...]).astype(BF16)
    o_ref[...] = jnp.dot(h, w_ref[...], preferred_element_type=F32)


def _ln_mod_matmul_into(x2d, modl, w_bf, dst, *, batch, row0, tm, row_fn):
    rows, n = x2d.shape[0], w_bf.shape[1]
    tiles_per_batch = rows // batch // tm
    blk0 = row0 // tm
    return pl.pallas_call(
        _ln_mod_matmul_kernel,
        grid=(rows // tm,),
        in_specs=[pl.BlockSpec((tm, D_MODEL), lambda i: (i, 0)), _mod_spec(row_fn, 0), _mod_spec(row_fn, 1),
                  _const_spec((D_MODEL, n)), pl.BlockSpec(memory_space=pl.ANY)],
        out_specs=pl.BlockSpec((None, tm, n), lambda i: (i // tiles_per_batch, blk0 + i % tiles_per_batch, 0)),
        out_shape=jax.ShapeDtypeStruct(dst.shape, dst.dtype),
        input_output_aliases={4: 0},
        compiler_params=_cparams(1),
        name="ln_mod_matmul",
    )(x2d, modl, modl, w_bf, dst)


def _s5_prep_kernel(pw_ref, bb_ref, cc_ref, dw_ref, w_ref, vt_ref, m_ref):
    t_len, gc = S5_T, S5_GC
    width = t_len * gc
    nt_dims = (((1,), (1,)), ((), ()))
    cols = lambda q: slice(q * LANES, (q + 1) * LANES)
    lane = lax.broadcasted_iota(jnp.int32, (gc, width), 1)
    for g in range(2):
        wide = []
        for d in range(2):
            pr, pi = pw_ref[g, d, 0], pw_ref[g, d, 1]
            bbr, bbi = bb_ref[g, d, 0], bb_ref[g, d, 1]
            cr, ci = cc_ref[g, d, 0], cc_ref[g, d, 1]

            def power(xr, xi, t, pr=pr, pi=pi):
                return pr[t:t + 1] * xr - pi[t:t + 1] * xi, pr[t:t + 1] * xi + pi[t:t + 1] * xr

            lc = [power(cr, ci, t) for t in range(t_len + 1)]
            for s in range(t_len):
                blk = slice(g * width + s * gc, g * width + (s + 1) * gc)
                lbr, lbi = power(bbr, bbi, t_len - 1 - s if d == 0 else s)
                w_ref[blk, cols(2 * d)] = lbr.astype(BF16)
                w_ref[blk, cols(2 * d + 1)] = lbi.astype(BF16)
                lcr, lci = lc[s + 1 if d == 0 else t_len - s]
                vt_ref[blk, cols(2 * d)] = lcr.astype(BF16)
                vt_ref[blk, cols(2 * d + 1)] = (-lci).astype(BF16)
            order = range(t_len) if d == 0 else range(t_len - 1, -1, -1)
            lcr_all = jnp.concatenate([lc[t][0] for t in order], axis=0)
            lci_all = jnp.concatenate([lc[t][1] for t in order], axis=0)
            wide.append(lax.dot_general(bbr, lcr_all, nt_dims, preferred_element_type=F32, precision=HIGHEST)
                        - lax.dot_general(bbi, lci_all, nt_dims, preferred_element_type=F32, precision=HIGHEST))
        kf = wide[0] + dw_ref[g]
        kb = wide[1]
        for s in range(t_len):
            fwd = jnp.where(lane >= s * gc, pltpu.roll(kf, s * gc, 1), 0.0)
            bwd = jnp.where(lane < (s + 1) * gc, pltpu.roll(kb, (s + 1) * gc % width, 1), 0.0)
            m_ref[g, s * gc:(s + 1) * gc, :] = (fwd + bwd).astype(BF16)


def _s5_matrices(lam_re, lam_im, log_dt, b_re, b_im, c_re, c_im, d_skip):
    t_len, g_n, p_n, i_n = S5_T, S5_GROUPS, S5_STATE, S5_GC
    tau_rows = -(-(t_len + 1) // 8) * 8
    dt = jnp.exp(log_dt)[..., None]
    mag, th = lam_re * dt, lam_im * dt
    er = jnp.exp(mag)
    lbr, lbi = er * jnp.cos(th), er * jnp.sin(th)
    den = lam_re * lam_re + lam_im * lam_im
    nr, ni = lbr - 1.0, lbi
    cr = (nr * lam_re + ni * lam_im) / den
    ci = (ni * lam_re - nr * lam_im) / den
    tau = jnp.arange(tau_rows, dtype=F32)[:, None]
    pm = jnp.exp(tau * mag[:, :, None, :])
    ang = tau * th[:, :, None, :]
    pw = jnp.stack([pm * jnp.cos(ang), pm * jnp.sin(ang)], axis=2)
    to_ip = lambda a: jnp.swapaxes(a, -1, -2)
    bb = jnp.stack([cr[..., None, :] * to_ip(b_re) - ci[..., None, :] * to_ip(b_im),
                    cr[..., None, :] * to_ip(b_im) + ci[..., None, :] * to_ip(b_re)], axis=2)
    cc = jnp.stack([c_re, c_im], axis=2)

    def per_group(a):
        a = jnp.swapaxes(a, 0, 1)
        a = a.reshape((g_n // 2, 2) + a.shape[1:])
        zero = jnp.zeros_like(a[:, 0])
        both = jnp.stack([jnp.concatenate([a[:, 0], zero], axis=-1), jnp.concatenate([zero, a[:, 1]], axis=-1)], axis=1)
        return both.reshape((g_n,) + both.shape[2:])

    width = t_len * i_n
    d_wide = jnp.pad(d_skip.reshape(g_n, i_n)[:, None, :] * jnp.asarray(np.eye(i_n), F32),
                     ((0, 0), (0, 0), (0, width - i_n)))
    pair = lambda shape: pl.BlockSpec((2,) + shape, lambda j: (j,) + (0,) * len(shape))
    state_spec = pl.BlockSpec((None, 2 * width, 4 * LANES), lambda j: (j, 0, 0))
    state_shape = jax.ShapeDtypeStruct((S5_PAIRS, 2 * width, 4 * LANES), BF16)
    w_state, vt_state, m_pair = pl.pallas_call(
        _s5_prep_kernel,
        grid=(S5_PAIRS,),
        in_specs=[pair((2, 2, tau_rows, 2 * p_n)), pair((2, 2, i_n, 2 * p_n)), pair((2, 2, i_n, 2 * p_n)),
                  pair((i_n, width))],
        out_specs=[state_spec, state_spec, pl.BlockSpec((None, 2, width, width), lambda j: (j, 0, 0, 0))],
        out_shape=[state_shape, state_shape, jax.ShapeDtypeStruct((S5_PAIRS, 2, width, width), BF16)],
        compiler_params=_cparams(1),
        name="s5_prep",
    )(per_group(pw), per_group(bb), per_group(cc), d_wide)
    a_pow = jnp.stack([pw[0, :, 0, t_len], pw[0, :, 1, t_len], pw[1, :, 0, t_len], pw[1, :, 1, t_len]], axis=0)
    return m_pair, w_state, vt_state, a_pow.reshape(4, g_n * p_n)


def _s5_fold_perm():
    n_t, n_g = LANES // S5_GC, LANES // S5_GC
    src = np.arange(n_t * LANES)
    t_lo, g, i = src // LANES, (src % LANES) // S5_GC, src % S5_GC
    perm = np.zeros((n_t * LANES, n_g * LANES), np.float32)
    perm[src, g * LANES + t_lo * S5_GC + i] = 1.0
    return perm


def _s5_kernel(s_ref, pb_ref, pbt_ref, m_ref, w_ref, v_ref, a_ref, y_ref, u_scr, st_scr, yp_scr, *, n_lat, n_ctx):
    n_all = n_lat + n_ctx
    n_t = LANES // S5_GC
    width = S5_T * S5_GC
    blk = lambda j: slice(j * LANES, (j + 1) * LANES)
    for t_hi in range(S5_T // n_t):
        x = jnp.concatenate([s_ref[pl.ds(n_t * t_hi + t_lo, n_all, stride=S5_T), :].astype(BF16)
                             for t_lo in range(n_t)], axis=1)
        xp = jnp.dot(x, pb_ref[...], preferred_element_type=F32).astype(BF16)
        for g in range(n_t):
            u_scr[g, :, blk(t_hi)] = xp[:, blk(g)]

    n_pairs = w_ref.shape[0]
    for pr in range(n_pairs):
        u_pair = jnp.concatenate([u_scr[2 * pr], u_scr[2 * pr + 1]], axis=1)
        st = jnp.dot(u_pair, w_ref[pr], preferred_element_type=F32)
        for q in range(4):
            st_scr[q, :, blk(pr)] = st[:, blk(q)]

    afr, afi, abr, abi = (a_ref[pl.ds(q, 1), :] for q in range(4))

    def scan_step(k, carry):
        hr, hi, gr, gi = carry
        rf = pl.ds(jnp.where(k < n_ctx, k + n_lat, k - n_ctx), 1)
        rb = pl.ds(n_all - 1 - k, 1)
        sfr, sfi, sbr, sbi = st_scr[0, rf, :], st_scr[1, rf, :], st_scr[2, rb, :], st_scr[3, rb, :]
        st_scr[0, rf, :] = hr
        st_scr[1, rf, :] = hi
        st_scr[2, rb, :] = gr
        st_scr[3, rb, :] = gi
        return (afr * hr - afi * hi + sfr, afr * hi + afi * hr + sfi,
                abr * gr - abi * gi + sbr, abr * gi + abi * gr + sbi)

    lax.fori_loop(0, n_all, scan_step, (jnp.zeros_like(afr),) * 4)

    for pr in range(n_pairs):
        hst = jnp.concatenate([st_scr[q, :, blk(pr)].astype(BF16) for q in range(4)], axis=1)
        y_pair = lax.dot_general(hst, v_ref[pr], (((1,), (1,)), ((), ())), preferred_element_type=F32)
        for g in range(2):
            y_g = (y_pair[:, g * width:(g + 1) * width]
                   + jnp.dot(u_scr[2 * pr + g], m_ref[pr, g], preferred_element_type=F32))
            for t_hi in range(S5_T // n_t):
                yp_scr[t_hi, :, blk(2 * pr + g)] = y_g[:, blk(t_hi)].astype(BF16)

    for t_hi in range(S5_T // n_t):
        z = jnp.dot(yp_scr[t_hi], pbt_ref[...], preferred_element_type=F32)
        for t_lo in range(n_t):
            y_ref[pl.ds(n_t * t_hi + t_lo, n_all, stride=S5_T), :] = z[:, blk(t_lo)]


def _s5_mix(s_all, mats, *, n_lat_rows):
    m_pair, w_state, v_state, a_pow = mats
    batch, n_rows, _ = s_all.shape
    n_lat = n_lat_rows // S5_T
    n_all = n_rows // S5_T
    width = S5_T * S5_GC
    n_blocks = S5_W // LANES
    pairs = S5_PAIRS // n_blocks
    perm = _s5_fold_perm()
    rows_spec = pl.BlockSpec((None, n_rows, LANES), lambda b, v: (b, 0, v))
    return pl.pallas_call(
        functools.partial(_s5_kernel, n_lat=n_lat, n_ctx=n_all - n_lat),
        grid=(batch, n_blocks),
        in_specs=[rows_spec, _const_spec(perm.shape), _const_spec(perm.shape),
                  pl.BlockSpec((pairs, 2, width, width), lambda b, v: (v, 0, 0, 0)),
                  pl.BlockSpec((pairs, 2 * width, 4 * LANES), lambda b, v: (v, 0, 0)),
                  pl.BlockSpec((pairs, 4 * LANES, 2 * width), lambda b, v: (v, 0, 0)),
                  pl.BlockSpec((4, pairs * LANES), lambda b, v: (0, v))],
        out_specs=rows_spec,
        out_shape=jax.ShapeDtypeStruct(s_all.shape, F32),
        scratch_shapes=[pltpu.VMEM((2 * pairs, n_all, width), BF16), pltpu.VMEM((4, n_all, pairs * LANES), F32),
                        pltpu.VMEM((S5_T * S5_GC // LANES, n_all, 2 * pairs * LANES), BF16)],
        compiler_params=_cparams(2),
        name="s5_mix",
    )(s_all, jnp.asarray(perm, BF16), jnp.asarray(perm.T, BF16), m_pair, w_state, v_state, a_pow)


def kernel(x, c, ctx, c_ctx, w_mod, b_mod, w_out, b_out, ln_mix_g, ln_mix_b, w_ffn1, b_ffn1, w_ffn2,
           b_ffn2, ln_ffn_g, ln_ffn_b, w_in_ab, lam_q1, lam_k1, lam_q2, lam_k2, subln_g, w_in_cd,
           s5_lam_re, s5_lam_im, s5_log_dt, s5_b_re, s5_b_im, s5_c_re, s5_c_im, s5_d, w_glu, b_glu,
           w_sp, b_sp):
    batch, n_lat, d = x.shape
    n_ctx = ctx.shape[1]
    assert d == D_MODEL and n_lat % ROW_TILE == 0 and n_ctx % CHUNK == 0 and batch + 1 <= MOD_ROWS
    mod = _modulation(c, c_ctx, w_mod, b_mod)
    xl = x.reshape(batch * n_lat, d)
    xc = ctx.reshape(batch * n_ctx, d)
    tiles = n_lat // ROW_TILE
    lat_rows = dict(tm=ROW_TILE, row_fn=lambda i: i // tiles)
    ctx_rows = dict(tm=n_ctx, row_fn=lambda i: batch)
    rope_tabs = _rope_tables(n_lat)
    for l in range(DEPTH):
        last = l == DEPTH - 1
        e = l // 2
        modl = mod[l]
        tail_w = (w_out[l].astype(BF16), b_out[l], ln_mix_g[l], ln_mix_b[l], w_ffn1[l].astype(BF16), b_ffn1[l],
                  w_ffn2[l].astype(BF16), b_ffn2[l], ln_ffn_g[l], ln_ffn_b[l])
        if l % 2 == 0:
            w_in = w_in_ab[e].astype(BF16)
            lam_init = 0.8 - 0.6 * math.exp(-0.3 * l)
            lam_vecs = (lam_q1[e], lam_k1[e], lam_q2[e], lam_k2[e])
            n_keys = n_lat + n_ctx
            f, q, k_all, vt_all = _inproj_even(xl, modl, w_in, rope_tabs, None, batch=batch, n_keys=n_keys,
                                               key_row0=0, **lat_rows)
            fc, qc, k_all, vt_all = _inproj_even(xc, modl, w_in, None, (k_all, vt_all), batch=batch, n_keys=n_keys,
                                                 key_row0=n_lat, **ctx_rows)
            attend = functools.partial(_diff_attention, lam_vecs=lam_vecs, subln_g=subln_g[e], lam_init=lam_init,
                                       batch=batch)
            ya = attend(q, k_all, vt_all, tq=ATTN_TQ, key_row0=0, n_keys=n_keys)
            yf = _fourier_mix(f, batch=batch)
            xl = _layer_tail(xl, yf, ya, modl, None, *tail_w, **lat_rows)
            if not last:
                yac = attend(qc, k_all, vt_all, tq=n_ctx, key_row0=n_lat, n_keys=n_ctx)
                yfc = _fourier_mix(fc, batch=batch)
                xc = _layer_tail(xc, yfc, yac, modl, None, *tail_w, **ctx_rows)
        else:
            assert last, "the S5 / gMLP layer is only implemented as the final layer"
            w_in = w_in_cd[e].astype(BF16)
            bsp_lanes = jnp.broadcast_to(b_sp[e][:, :, None], (GMLP_GROUPS, CHUNK, GMLP_GC))
            s_all, gm = _inproj_odd(xl, modl, w_in, w_sp[e].astype(BF16), bsp_lanes, batch=batch,
                                    n_rows=n_lat + n_ctx, **lat_rows)
            s_all = _ln_mod_matmul_into(xc, modl, w_in[:, :S5_W], s_all, batch=batch, row0=n_lat, **ctx_rows)
            mats = _s5_matrices(s5_lam_re[e], s5_lam_im[e], s5_log_dt[e], s5_b_re[e], s5_b_im[e], s5_c_re[e],
                                s5_c_im[e], s5_d[e])
            ys = _s5_mix(s_all, mats, n_lat_rows=n_lat)
            xl = _layer_tail(xl, ys, gm, modl, (w_glu[e].astype(BF16), b_glu[e]), *tail_w, **lat_rows)
    return xl.reshape(batch, n_lat, d)
```

```python
import functools
import math

import numpy as np
import jax
import jax.numpy as jnp
from jax import lax
from jax.experimental import pallas as pl
from jax.experimental.pallas import tpu as pltpu

D_MODEL = 1024
DEPTH = 2
GRID_W = 64
FOURIER_W = D_MODEL // 4
FOURIER_GC = 64
DIFF_HD = 64
DIFF_VD = 2 * DIFF_HD
DIFF_HEADS = (D_MODEL - FOURIER_W) // DIFF_VD
QK_W = DIFF_HEADS * 2 * DIFF_HD
V_W = DIFF_HEADS * DIFF_VD
EVEN_IN_W = FOURIER_W + 2 * QK_W + V_W
DIFF_SCALE = DIFF_HD ** -0.5
ROPE_BASE = 10000.0
ROPE_FREQS = DIFF_HD // 4
S5_W = D_MODEL // 2
S5_GC = 16
S5_GROUPS = S5_W // S5_GC
S5_STATE = 64
GMLP_W = D_MODEL // 2
GMLP_GC = 128
GMLP_GROUPS = GMLP_W // GMLP_GC
CHUNK = 128
ODD_IN_W = S5_W + 2 * GMLP_W
FFN_W = 4 * D_MODEL
LN_EPS = 1e-5
ALPHA = (2 * DEPTH) ** 0.25

F32 = jnp.float32
BF16 = jnp.bfloat16
HIGHEST = lax.Precision.HIGHEST

LANES = 128
MOD_ROWS = 8
ROW_TILE = 512
ATTN_TQ = 1024
ATTN_TK = 256
ATTN_GROUP = 3
ATTN_SUM_ROWS = 16
FFT_BATCH = 4
S5_T = 16
S5_PAIRS = S5_GROUPS // 2
FFN_CHUNK = 1024
TAIL_PARTS = 2
VMEM_LIMIT = 56 * 2 ** 20
Q_SCALE = DIFF_SCALE * math.log2(math.e)


def _cparams(n_axes, vmem=VMEM_LIMIT):
    return pltpu.CompilerParams(dimension_semantics=("arbitrary",) * n_axes, vmem_limit_bytes=vmem)


def _const_spec(shape):
    zeros = (0,) * len(shape)
    return pl.BlockSpec(shape, lambda *_: zeros, pipeline_mode=pl.Buffered(1))


def _ln(x):
    xc = x - jnp.mean(x, -1, keepdims=True)
    var = jnp.mean(xc * xc, -1, keepdims=True)
    return xc * lax.rsqrt(var + LN_EPS)


def _mm(a, b):
    return jnp.dot(a.astype(BF16), b.astype(BF16), preferred_element_type=F32)


def _mod_spec(row_fn, which):
    return pl.BlockSpec((None, 1, D_MODEL), lambda i: (row_fn(i) * 6 + which, 0, 0))


def _mod_kernel(c_ref, w_ref, b_ref, o_ref):
    c = c_ref[...]
    a = c * jax.nn.sigmoid(c)
    o_ref[...] = jnp.dot(a, w_ref[...], preferred_element_type=F32, precision=HIGHEST) + b_ref[...]


def _modulation(c, c_ctx, w_mod, b_mod):
    batch = c.shape[0]
    rows = jnp.concatenate([c, c_ctx[None], jnp.zeros((MOD_ROWS - batch - 1, D_MODEL), F32)], axis=0)
    out = pl.pallas_call(
        _mod_kernel,
        grid=(DEPTH, 6),
        in_specs=[
            pl.BlockSpec((MOD_ROWS, D_MODEL), lambda l, j: (0, 0)),
            pl.BlockSpec((None, D_MODEL, D_MODEL), lambda l, j: (l, 0, j)),
            pl.BlockSpec((None, 1, D_MODEL), lambda l, j: (l, 0, j)),
        ],
        out_specs=pl.BlockSpec((None, MOD_ROWS, D_MODEL), lambda l, j: (l, 0, j)),
        out_shape=jax.ShapeDtypeStruct((DEPTH, MOD_ROWS, 6 * D_MODEL), F32),
        compiler_params=_cparams(2),
        name="modulation",
    )(rows, w_mod, b_mod.reshape(DEPTH, 1, 6 * D_MODEL))
    return out.reshape(DEPTH, MOD_ROWS * 6, 1, D_MODEL)


def _inproj_even_kernel(*refs, rope):
    if rope:
        x_ref, sh_ref, sc_ref, w_ref, cos_ref, sin_ref, f_ref, q_ref, k_ref, vt_ref = refs
    else:
        x_ref, sh_ref, sc_ref, w_ref, _, _, f_ref, q_ref, k_ref, vt_ref = refs
    h = (_ln(x_ref[...]) * (1.0 + sc_ref[...]) + sh_ref[...]).astype(BF16)
    f_ref[...] = jnp.dot(h, w_ref[:, :FOURIER_W], preferred_element_type=F32)
    width = 2 * LANES
    if rope:
        cos = cos_ref[...]
        sin = sin_ref[...]
        lane = lax.broadcasted_iota(jnp.int32, cos.shape, 1)
        first_half = (lane & ROPE_FREQS) == 0
    for j in range(QK_W // width):
        for base, o_ref, scale in ((FOURIER_W, q_ref, Q_SCALE), (FOURIER_W + QK_W, k_ref, None)):
            t = jnp.dot(h, w_ref[:, base + j * width:base + (j + 1) * width], preferred_element_type=F32)
            if rope:
                partner = jnp.where(first_half, pltpu.roll(t, width - ROPE_FREQS, 1),
                                    pltpu.roll(t, ROPE_FREQS, 1))
                t = t * cos + partner * sin
            if scale is not None:
                t = t * scale
            o_ref[:, j * width:(j + 1) * width] = t.astype(BF16)
    v0 = FOURIER_W + 2 * QK_W
    for j in range(V_W // width):
        v = jnp.dot(h, w_ref[:, v0 + j * width:v0 + (j + 1) * width], preferred_element_type=F32)
        tk = vt_ref.shape[2]
        for c in range(vt_ref.shape[0]):
            vt_ref[c, j * width:(j + 1) * width, :] = v[c * tk:(c + 1) * tk].T.astype(BF16)


def _inproj_even(x2d, modl, w_bf, rope_tabs, kv_all, *, batch, n_keys, key_row0, tm, row_fn):
    rows = x2d.shape[0]
    rope = rope_tabs is not None
    tiles_per_batch = rows // batch // tm
    blk0 = key_row0 // tm
    in_specs = [
        pl.BlockSpec((tm, D_MODEL), lambda i: (i, 0)),
        _mod_spec(row_fn, 0),
        _mod_spec(row_fn, 1),
        _const_spec((D_MODEL, EVEN_IN_W)),
    ]
    args = [x2d, modl, modl, w_bf]
    if rope:
        in_specs += [pl.BlockSpec((tm, 2 * LANES), lambda i: (i % tiles_per_batch, 0))] * 2
        args += list(rope_tabs)
        aliases = {}
    else:
        in_specs += [pl.BlockSpec(memory_space=pl.ANY)] * 2
        args += list(kv_all)
        aliases = {4: 2, 5: 3}
    kv_idx = lambda i: (i // tiles_per_batch, blk0 + i % tiles_per_batch, 0)
    return pl.pallas_call(
        functools.partial(_inproj_even_kernel, rope=rope),
        grid=(rows // tm,),
        in_specs=in_specs,
        out_specs=[
            pl.BlockSpec((tm, FOURIER_W), lambda i: (i, 0)),
            pl.BlockSpec((tm, QK_W), lambda i: (i, 0)),
            pl.BlockSpec((None, tm, QK_W), kv_idx),
            pl.BlockSpec((None, tm // ATTN_TK, V_W, ATTN_TK), lambda i: kv_idx(i) + (0,)),
        ],
        out_shape=[
            jax.ShapeDtypeStruct((rows, FOURIER_W), F32),
            jax.ShapeDtypeStruct((rows, QK_W), BF16),
            jax.ShapeDtypeStruct((batch, n_keys, QK_W), BF16),
            jax.ShapeDtypeStruct((batch, n_keys // ATTN_TK, V_W, ATTN_TK), BF16),
        ],
        input_output_aliases=aliases,
        compiler_params=_cparams(1),
        name="inproj_even_rope" if rope else "inproj_even",
    )(*args)


def _rope_tables(n):
    rows = n // GRID_W
    row = jnp.repeat(jnp.arange(rows, dtype=F32), GRID_W)
    col = jnp.tile(jnp.arange(GRID_W, dtype=F32), rows)
    inv = jnp.power(ROPE_BASE, -jnp.arange(ROPE_FREQS, dtype=F32) / ROPE_FREQS)
    ang = jnp.stack([row[:, None] * inv, col[:, None] * inv], axis=1)
    cos, sin = jnp.cos(ang), jnp.sin(ang)
    cos_map = jnp.concatenate([cos, cos], axis=-1).reshape(n, 4 * ROPE_FREQS)
    sin_map = jnp.concatenate([-sin, sin], axis=-1).reshape(n, 4 * ROPE_FREQS)
    reps = 2 * LANES // (4 * ROPE_FREQS)
    return jnp.tile(cos_map, (1, reps)), jnp.tile(sin_map, (1, reps))


def _attn_kernel(lq1, lk1, lq2, lk2, g_ref, q_ref, k_ref, vt_ref, o_ref, acc_ref, *scratch, lam_init):
    tq = q_ref.shape[0]
    n_chunks, _, tk = vt_ref.shape
    q = q_ref[...]
    lane = lax.broadcasted_iota(jnp.int32, q.shape, 1)
    zero = jnp.zeros_like(q)
    q_cat = jnp.concatenate([jnp.where(lane < DIFF_HD, q, zero), jnp.where(lane >= DIFF_HD, q, zero)], axis=0)
    nt_dims = (((1,), (1,)), ((), ()))

    p_ref, al_ref = scratch[-2:]
    grp = (len(scratch) - 2) // 2
    s_refs, mc_refs = scratch[:grp], scratch[grp:2 * grp]
    n_groups = n_chunks // grp

    def scores(g, r):
        kb = k_ref[pl.ds(pl.multiple_of((g * grp + r) * tk, tk), tk), :]
        s = lax.dot_general(kb, q_cat, nt_dims, preferred_element_type=F32)
        s_refs[r][...] = s
        mc_refs[r][...] = jnp.max(s, axis=0, keepdims=True)

    def group_max(m_prev):
        m = m_prev
        for r in range(grp):
            m = jnp.maximum(m, mc_refs[r][...])
        al_ref[...] = jnp.exp2(m_prev - m)
        return m

    def probs(r, m):
        p_ref[r * tk:(r + 1) * tk, :] = jnp.exp2(s_refs[r][...] - m).astype(BF16)

    ones_rows = (lax.broadcasted_iota(jnp.int32, (ATTN_SUM_ROWS, grp * tk), 0) == 0).astype(BF16)

    def values(g):
        vt = jnp.concatenate([vt_ref[g * grp + r] for r in range(grp)], axis=1)
        vt1 = jnp.concatenate([vt, ones_rows], axis=0)
        acc_ref[...] = acc_ref[...] * al_ref[...] + jnp.dot(vt1, p_ref[...], preferred_element_type=F32)

    acc_ref[...] = jnp.zeros_like(acc_ref)
    m = jnp.full((1, 2 * tq), -jnp.inf, F32)
    for r in range(grp):
        scores(0, r)
    m = group_max(m)
    for r in range(grp):
        probs(r, m)
        if n_groups > 1:
            scores(1, r)

    def body(g, m):
        values(g - 1)
        m = group_max(m)
        for r in range(grp):
            probs(r, m)
            scores(g + 1, r)
        return m

    if n_groups > 2:
        m = lax.fori_loop(1, n_groups - 1, body, m)
    if n_groups > 1:
        values(n_groups - 2)
        m = group_max(m)
        for r in range(grp):
            probs(r, m)
    values(n_groups - 1)

    lam = (jnp.exp(jnp.sum(lq1[...] * lk1[...], keepdims=True))
           - jnp.exp(jnp.sum(lq2[...] * lk2[...], keepdims=True)) + lam_init)
    on = acc_ref[:DIFF_VD, :] / acc_ref[DIFF_VD:DIFF_VD + 1, :]
    o = on[:, :tq] - lam * on[:, tq:]
    o = o * lax.rsqrt(jnp.mean(o * o, axis=0, keepdims=True) + LN_EPS)
    o_ref[...] = o.T * (g_ref[...] * (1.0 - lam_init))


def _diff_attention(q, k_all, vt_all, lam_vecs, subln_g, lam_init, *, batch, tq, key_row0, n_keys):
    rows = q.shape[0]
    nq = rows // batch // tq
    tk = vt_all.shape[3]
    n_chunks = n_keys // tk
    assert key_row0 % n_keys == 0
    kblk = key_row0 // n_keys
    grp = ATTN_GROUP if n_chunks % ATTN_GROUP == 0 else 1
    cols = 2 * tq
    slot_shapes = [((tk, cols), F32), ((1, cols), F32)]
    vec_spec = pl.BlockSpec((1, DIFF_HD), lambda b, h, i: (0, 0))
    in_specs = [vec_spec] * 4 + [
        pl.BlockSpec((1, DIFF_VD), lambda b, h, i: (0, 0)),
        pl.BlockSpec((tq, DIFF_VD), lambda b, h, i: (b * nq + i, h)),
        pl.BlockSpec((None, n_keys, DIFF_VD), lambda b, h, i: (b, kblk, h)),
        pl.BlockSpec((None, n_chunks, DIFF_VD, tk), lambda b, h, i: (b, kblk, h, 0)),
    ]
    args = [v.reshape(1, DIFF_HD) for v in lam_vecs] + [subln_g.reshape(1, DIFF_VD), q, k_all, vt_all]
    return pl.pallas_call(
        functools.partial(_attn_kernel, lam_init=lam_init),
        grid=(batch, DIFF_HEADS, nq),
        in_specs=in_specs,
        out_specs=pl.BlockSpec((tq, DIFF_VD), lambda b, h, i: (b * nq + i, h)),
        out_shape=jax.ShapeDtypeStruct((rows, V_W), F32),
        scratch_shapes=[pltpu.VMEM((DIFF_VD + ATTN_SUM_ROWS, cols), F32)]
        + [pltpu.VMEM(shape, dtype) for shape, dtype in slot_shapes for _ in range(grp)]
        + [pltpu.VMEM((grp * tk, cols), BF16), pltpu.VMEM((1, cols), F32)],
        compiler_params=_cparams(3),
        name="diff_attention" if n_chunks > 1 else "diff_attention_ctx",
    )(*args)


def _dft_cos_sin(n):
    idx = np.outer(np.arange(n), np.arange(n)) % n
    ang = 2.0 * np.pi * idx / n
    return np.cos(ang), np.sin(ang)


def _channel_dft():
    c, s = _dft_cos_sin(FOURIER_GC)
    eye = np.eye(LANES // FOURIER_GC)
    return jnp.asarray(np.kron(eye, c), F32).astype(BF16), jnp.asarray(np.kron(eye, s), F32).astype(BF16)


def _fourier_dense_kernel(x_ref, cl_ref, sl_ref, bdc_ref, bds_ref, o_ref, *, norm):
    x = x_ref[...]
    pr = _mm(cl_ref[...], x)
    pi = -_mm(sl_ref[...], x)
    out = _mm(pr, bdc_ref[...]) + _mm(pi, bds_ref[...])
    o_ref[...] = out * norm


def _fourier_fft_kernel(x_ref, f1_ref, twc_ref, tws_ref, f3_ref, bdc_ref, bds_ref, o_ref, ur_ref, ui_ref,
                        *, n1, norm):
    n2 = LANES

    nb = FFT_BATCH
    lanes = lambda j: slice(j * LANES, (j + 1) * LANES)

    def stage1(ib, carry):
        i0 = ib * nb
        xs = jnp.concatenate([x_ref[pl.ds(i0 + j, n2, stride=n1), :] for j in range(nb)], axis=1)
        t = _mm(f1_ref[...], xs)
        for j in range(nb):
            tr, ti = t[:n2, lanes(j)], t[n2:, lanes(j)]
            r0 = pl.multiple_of((i0 + j) * n2, n2)
            c = twc_ref[pl.ds(r0, n2), :]
            s = tws_ref[pl.ds(r0, n2), :]
            ur_ref[pl.ds(r0, n2), :] = tr * c + ti * s
            ui_ref[pl.ds(r0, n2), :] = ti * c - tr * s
        return carry

    lax.fori_loop(0, n1 // nb, stage1, 0)

    def stage2(kb, carry):
        k0 = kb * nb
        u = jnp.concatenate(
            [jnp.concatenate([ur_ref[pl.ds(k0 + j, n1, stride=n2), :], ui_ref[pl.ds(k0 + j, n1, stride=n2), :]],
                             axis=0) for j in range(nb)], axis=1)
        p = _mm(f3_ref[...], u)
        for j in range(nb):
            ur_ref[pl.ds(k0 + j, n1, stride=n2), :] = p[:n1, lanes(j)]
            ui_ref[pl.ds(k0 + j, n1, stride=n2), :] = p[n1:, lanes(j)]
        return carry

    lax.fori_loop(0, n2 // nb, stage2, 0)

    rows = ROW_TILE

    def stage3(r, carry):
        r0 = pl.multiple_of(r * rows, rows)
        out = _mm(ur_ref[pl.ds(r0, rows), :], bdc_ref[...]) + _mm(ui_ref[pl.ds(r0, rows), :], bds_ref[...])
        o_ref[pl.ds(r0, rows), :] = out * norm
        return carry

    lax.fori_loop(0, (n1 * n2) // rows, stage3, 0)


def _fourier_mix(f2d, *, batch):
    rows = f2d.shape[0]
    n = rows // batch
    norm = 1.0 / math.sqrt(n * FOURIER_GC)
    bdc, bds = _channel_dft()
    blk = pl.BlockSpec((n, LANES), lambda b, j: (b, j))
    grid = (batch, FOURIER_W // LANES)
    out_shape = jax.ShapeDtypeStruct((rows, FOURIER_W), F32)
    mat = _const_spec((LANES, LANES))
    if n <= ROW_TILE:
        c, s = _dft_cos_sin(n)
        return pl.pallas_call(
            functools.partial(_fourier_dense_kernel, norm=norm),
            grid=grid,
            in_specs=[blk, _const_spec((n, n)), _const_spec((n, n)), mat, mat],
            out_specs=blk,
            out_shape=out_shape,
            compiler_params=_cparams(2),
            name="fourier_dense",
        )(f2d, jnp.asarray(c, F32).astype(BF16), jnp.asarray(s, F32).astype(BF16), bdc, bds)
    n2 = LANES
    n1 = n // n2
    c2, s2 = _dft_cos_sin(n2)
    f1 = jnp.asarray(np.concatenate([c2, -s2], axis=0), F32).astype(BF16)
    c1, s1 = _dft_cos_sin(n1)
    f3 = jnp.asarray(np.block([[c1, s1], [-s1, c1]]), F32).astype(BF16)
    tw_idx = np.outer(np.arange(n1), np.arange(n2)).reshape(-1)
    tw_ang = 2.0 * np.pi * tw_idx / n
    twc = jnp.broadcast_to(jnp.asarray(np.cos(tw_ang), F32)[:, None], (n, LANES))
    tws = jnp.broadcast_to(jnp.asarray(np.sin(tw_ang), F32)[:, None], (n, LANES))
    return pl.pallas_call(
        functools.partial(_fourier_fft_kernel, n1=n1, norm=norm),
        grid=grid,
        in_specs=[blk, _const_spec((2 * n2, n2)), _const_spec((n, LANES)), _const_spec((n, LANES)),
                  _const_spec((2 * n1, 2 * n1)), mat, mat],
        out_specs=blk,
        out_shape=out_shape,
        scratch_shapes=[pltpu.VMEM((n, LANES), F32), pltpu.VMEM((n, LANES), F32)],
        compiler_params=_cparams(2),
        name="fourier_fft",
    )(f2d, f1, twc, tws, f3, bdc, bds)


def _gelu_tanh(x):
    return 0.5 * x * (1.0 + jnp.tanh(math.sqrt(2.0 / math.pi) * (x + 0.044715 * (x * x * x))))


def _tail_kernel(*refs, glu, wa):
    if glu:
        x_ref, ya_ref, yb_ref, wg_ref, bg_ref, *rest = refs
    else:
        x_ref, ya_ref, yb_ref, *rest = refs
    (wo_ref, bo_ref, g1_ref, lg1_ref, lb1_ref, sh2_ref, sc2_ref, g2_ref,
     w1_ref, b1_ref, w2_ref, b2_ref, lg2_ref, lb2_ref, o_ref) = rest
    part = x_ref.shape[0] // TAIL_PARTS
    rows = lambda r: slice(r * part, (r + 1) * part)

    def mix_out(r):
        ya = ya_ref[rows(r), :]
        if glu:
            g = _gelu_tanh(ya)
            ya = g * jax.nn.sigmoid(jnp.dot(g.astype(BF16), wg_ref[...], preferred_element_type=F32) + bg_ref[...])
        y = (jnp.dot(ya.astype(BF16), wo_ref[:wa, :], preferred_element_type=F32)
             + jnp.dot(yb_ref[rows(r), :].astype(BF16), wo_ref[wa:, :], preferred_element_type=F32) + bo_ref[...])
        x1 = _ln(ALPHA * x_ref[rows(r), :] + g1_ref[...] * y) * lg1_ref[...] + lb1_ref[...]
        return x1, (_ln(x1) * (1.0 + sc2_ref[...]) + sh2_ref[...]).astype(BF16)

    def ffn_chunk(h, c):
        cols = slice(c * FFN_CHUNK, (c + 1) * FFN_CHUNK)
        a = jnp.maximum(jnp.dot(h, w1_ref[:, cols], preferred_element_type=F32) + b1_ref[:, cols], 0.0)
        return jnp.dot((a * a).astype(BF16), w2_ref[cols, :], preferred_element_type=F32)

    def finish(r, x1, y2):
        o_ref[rows(r), :] = _ln(ALPHA * x1 + g2_ref[...] * (y2 + b2_ref[...])) * lg2_ref[...] + lb2_ref[...]

    state = mix_out(0)
    done = None
    for r in range(TAIL_PARTS):
        x1, h = state
        y2 = ffn_chunk(h, 0)
        if r + 1 < TAIL_PARTS:
            state = mix_out(r + 1)
        if done is not None:
            finish(*done)
        for c in range(1, FFN_W // FFN_CHUNK):
            y2 = y2 + ffn_chunk(h, c)
        done = (r, x1, y2)
    finish(*done)


def _layer_tail(x2d, ya, yb, modl, glu_params, w_out, b_out, lg1, lb1, w1, b1, w2, b2, lg2, lb2, *, tm, row_fn):
    rows = x2d.shape[0]
    wa, wb = ya.shape[-1], yb.shape[1]
    glu = glu_params is not None
    row = lambda v: v.reshape(1, -1)
    vec = _const_spec((1, D_MODEL))
    if ya.ndim == 3:
        tiles_per_batch = rows // ya.shape[0] // tm
        ya_spec = pl.BlockSpec((None, tm, wa), lambda i: (i // tiles_per_batch, i % tiles_per_batch, 0))
    else:
        ya_spec = pl.BlockSpec((tm, wa), lambda i: (i, 0))
    in_specs = [pl.BlockSpec((tm, D_MODEL), lambda i: (i, 0)), ya_spec,
                pl.BlockSpec((tm, wb), lambda i: (i, 0))]
    args = [x2d, ya, yb]
    if glu:
        in_specs += [_const_spec((wa, wa)), _const_spec((1, wa))]
        args += [glu_params[0], row(glu_params[1])]
    in_specs += [_const_spec((D_MODEL, D_MODEL)), vec, _mod_spec(row_fn, 2), vec, vec,
                 _mod_spec(row_fn, 3), _mod_spec(row_fn, 4), _mod_spec(row_fn, 5),
                 _const_spec((D_MODEL, FFN_W)), _const_spec((1, FFN_W)), _const_spec((FFN_W, D_MODEL)), vec, vec, vec]
    args += [w_out, row(b_out), modl, row(lg1), row(lb1), modl, modl, modl,
             w1, row(b1), w2, row(b2), row(lg2), row(lb2)]
    return pl.pallas_call(
        functools.partial(_tail_kernel, glu=glu, wa=wa),
        grid=(rows // tm,),
        in_specs=in_specs,
        out_specs=pl.BlockSpec((tm, D_MODEL), lambda i: (i, 0)),
        out_shape=jax.ShapeDtypeStruct((rows, D_MODEL), F32),
        compiler_params=_cparams(1),
        name="layer_tail_glu" if glu else "layer_tail",
    )(*args)


def _inproj_odd_kernel(x_ref, sh_ref, sc_ref, w_ref, wsp_ref, bsp_ref, s_ref, gm_ref):
    h = (_ln(x_ref[...]) * (1.0 + sc_ref[...]) + sh_ref[...]).astype(BF16)
    s_ref[...] = jnp.dot(h, w_ref[:, :S5_W], preferred_element_type=F32)
    u = jnp.dot(h, w_ref[:, S5_W:S5_W + GMLP_W], preferred_element_type=F32)
    v = jnp.dot(h, w_ref[:, S5_W + GMLP_W:], preferred_element_type=F32)
    n_ch = x_ref.shape[0] // CHUNK
    rows = lambda ch: slice(ch * CHUNK, (ch + 1) * CHUNK)
    for g in range(GMLP_GROUPS):
        cols = slice(g * GMLP_GC, (g + 1) * GMLP_GC)
        vn = jnp.concatenate([_ln(v[rows(ch), cols]).astype(BF16) for ch in range(n_ch)], axis=1)
        sp = jnp.dot(wsp_ref[g], vn, preferred_element_type=F32)
        for ch in range(n_ch):
            gm_ref[rows(ch), cols] = u[rows(ch), cols] * (sp[:, ch * GMLP_GC:(ch + 1) * GMLP_GC] + bsp_ref[g])


def _inproj_odd(x2d, modl, w_bf, wsp_bf, bsp_lanes, *, batch, n_rows, tm, row_fn):
    rows = x2d.shape[0]
    tiles_per_batch = rows // batch // tm
    return pl.pallas_call(
        _inproj_odd_kernel,
        grid=(rows // tm,),
        in_specs=[pl.BlockSpec((tm, D_MODEL), lambda i: (i, 0)), _mod_spec(row_fn, 0), _mod_spec(row_fn, 1),
                  _const_spec((D_MODEL, ODD_IN_W)), _const_spec((GMLP_GROUPS, CHUNK, CHUNK)),
                  _const_spec((GMLP_GROUPS, CHUNK, GMLP_GC))],
        out_specs=[pl.BlockSpec((None, tm, S5_W), lambda i: (i // tiles_per_batch, i % tiles_per_batch, 0)),
                   pl.BlockSpec((tm, GMLP_W), lambda i: (i, 0))],
        out_shape=[jax.ShapeDtypeStruct((batch, n_rows, S5_W), F32), jax.ShapeDtypeStruct((rows, GMLP_W), F32)],
        compiler_params=_cparams(1),
        name="inproj_odd_gmlp",
    )(x2d, modl, modl, w_bf, wsp_bf, bsp_lanes)


def _ln_mod_matmul_kernel(x_ref, sh_ref, sc_ref, w_ref, _, o_ref):
    h = (_ln(x_ref[...]) * (1.0 + sc_ref[...]) + sh_ref[...]).astype(BF16)
    o_ref[...] = jnp.dot(h, w_ref[...], preferred_element_type=F32)


def _ln_mod_matmul_into(x2d, modl, w_bf, dst, *, batch, row0, tm, row_fn):
    rows, n = x2d.shape[0], w_bf.shape[1]
    tiles_per_batch = rows // batch // tm
    blk0 = row0 // tm
    return pl.pallas_call(
        _ln_mod_matmul_kernel,
        grid=(rows // tm,),
        in_specs=[pl.BlockSpec((tm, D_MODEL), lambda i: (i, 0)), _mod_spec(row_fn, 0), _mod_spec(row_fn, 1),
                  _const_spec((D_MODEL, n)), pl.BlockSpec(memory_space=pl.ANY)],
        out_specs=pl.BlockSpec((None, tm, n), lambda i: (i // tiles_per_batch, blk0 + i % tiles_per_batch, 0)),
        out_shape=jax.ShapeDtypeStruct(dst.shape, dst.dtype),
        input_output_aliases={4: 0},
        compiler_params=_cparams(1),
        name="ln_mod_matmul",
    )(x2d, modl, modl, w_bf, dst)


def _s5_prep_kernel(pw_ref, bb_ref, cc_ref, dw_ref, w_ref, vt_ref, m_ref):
    t_len, gc = S5_T, S5_GC
    width = t_len * gc
    nt_dims = (((1,), (1,)), ((), ()))
    cols = lambda q: slice(q * LANES, (q + 1) * LANES)
    lane = lax.broadcasted_iota(jnp.int32, (gc, width), 1)
    for g in range(2):
        wide = []
        for d in range(2):
            pr, pi = pw_ref[g, d, 0], pw_ref[g, d, 1]
            bbr, bbi = bb_ref[g, d, 0], bb_ref[g, d, 1]
            cr, ci = cc_ref[g, d, 0], cc_ref[g, d, 1]

            def power(xr, xi, t, pr=pr, pi=pi):
                return pr[t:t + 1] * xr - pi[t:t + 1] * xi, pr[t:t + 1] * xi + pi[t:t + 1] * xr

            lc = [power(cr, ci, t) for t in range(t_len + 1)]
            for s in range(t_len):
                blk = slice(g * width + s * gc, g * width + (s + 1) * gc)
                lbr, lbi = power(bbr, bbi, t_len - 1 - s if d == 0 else s)
                w_ref[blk, cols(2 * d)] = lbr.astype(BF16)
                w_ref[blk, cols(2 * d + 1)] = lbi.astype(BF16)
                lcr, lci = lc[s + 1 if d == 0 else t_len - s]
                vt_ref[blk, cols(2 * d)] = lcr.astype(BF16)
                vt_ref[blk, cols(2 * d + 1)] = (-lci).astype(BF16)
            order = range(t_len) if d == 0 else range(t_len - 1, -1, -1)
            lcr_all = jnp.concatenate([lc[t][0] for t in order], axis=0)
            lci_all = jnp.concatenate([lc[t][1] for t in order], axis=0)
            wide.append(lax.dot_general(bbr, lcr_all, nt_dims, preferred_element_type=F32, precision=HIGHEST)
                        - lax.dot_general(bbi, lci_all, nt_dims, preferred_element_type=F32, precision=HIGHEST))
        kf = wide[0] + dw_ref[g]
        kb = wide[1]
        for s in range(t_len):
            fwd = jnp.where(lane >= s * gc, pltpu.roll(kf, s * gc, 1), 0.0)
            bwd = jnp.where(lane < (s + 1) * gc, pltpu.roll(kb, (s + 1) * gc % width, 1), 0.0)
            m_ref[g, s * gc:(s + 1) * gc, :] = (fwd + bwd).astype(BF16)


def _s5_matrices(lam_re, lam_im, log_dt, b_re, b_im, c_re, c_im, d_skip):
    t_len, g_n, p_n, i_n = S5_T, S5_GROUPS, S5_STATE, S5_GC
    tau_rows = -(-(t_len + 1) // 8) * 8
    dt = jnp.exp(log_dt)[..., None]
    mag, th = lam_re * dt, lam_im * dt
    er = jnp.exp(mag)
    lbr, lbi = er * jnp.cos(th), er * jnp.sin(th)
    den = lam_re * lam_re + lam_im * lam_im
    nr, ni = lbr - 1.0, lbi
    cr = (nr * lam_re + ni * lam_im) / den
    ci = (ni * lam_re - nr * lam_im) / den
    tau = jnp.arange(tau_rows, dtype=F32)[:, None]
    pm = jnp.exp(tau * mag[:, :, None, :])
    ang = tau * th[:, :, None, :]
    pw = jnp.stack([pm * jnp.cos(ang), pm * jnp.sin(ang)], axis=2)
    to_ip = lambda a: jnp.swapaxes(a, -1, -2)
    bb = jnp.stack([cr[..., None, :] * to_ip(b_re) - ci[..., None, :] * to_ip(b_im),
                    cr[..., None, :] * to_ip(b_im) + ci[..., None, :] * to_ip(b_re)], axis=2)
    cc = jnp.stack([c_re, c_im], axis=2)

    def per_group(a):
        a = jnp.swapaxes(a, 0, 1)
        a = a.reshape((g_n // 2, 2) + a.shape[1:])
        zero = jnp.zeros_like(a[:, 0])
        both = jnp.stack([jnp.concatenate([a[:, 0], zero], axis=-1), jnp.concatenate([zero, a[:, 1]], axis=-1)], axis=1)
        return both.reshape((g_n,) + both.shape[2:])

    width = t_len * i_n
    d_wide = jnp.pad(d_skip.reshape(g_n, i_n)[:, None, :] * jnp.asarray(np.eye(i_n), F32),
                     ((0, 0), (0, 0), (0, width - i_n)))
    pair = lambda shape: pl.BlockSpec((2,) + shape, lambda j: (j,) + (0,) * len(shape))
    state_spec = pl.BlockSpec((None, 2 * width, 4 * LANES), lambda j: (j, 0, 0))
    state_shape = jax.ShapeDtypeStruct((S5_PAIRS, 2 * width, 4 * LANES), BF16)
    w_state, vt_state, m_pair = pl.pallas_call(
        _s5_prep_kernel,
        grid=(S5_PAIRS,),
        in_specs=[pair((2, 2, tau_rows, 2 * p_n)), pair((2, 2, i_n, 2 * p_n)), pair((2, 2, i_n, 2 * p_n)),
                  pair((i_n, width))],
        out_specs=[state_spec, state_spec, pl.BlockSpec((None, 2, width, width), lambda j: (j, 0, 0, 0))],
        out_shape=[state_shape, state_shape, jax.ShapeDtypeStruct((S5_PAIRS, 2, width, width), BF16)],
        compiler_params=_cparams(1),
        name="s5_prep",
    )(per_group(pw), per_group(bb), per_group(cc), d_wide)
    a_pow = jnp.stack([pw[0, :, 0, t_len], pw[0, :, 1, t_len], pw[1, :, 0, t_len], pw[1, :, 1, t_len]], axis=0)
    return m_pair, w_state, vt_state, a_pow.reshape(4, g_n * p_n)


def _s5_fold_perm():
    n_t, n_g = LANES // S5_GC, LANES // S5_GC
    src = np.arange(n_t * LANES)
    t_lo, g, i = src // LANES, (src % LANES) // S5_GC, src % S5_GC
    perm = np.zeros((n_t * LANES, n_g * LANES), np.float32)
    perm[src, g * LANES + t_lo * S5_GC + i] = 1.0
    return perm


def _s5_kernel(s_ref, pb_ref, pbt_ref, m_ref, w_ref, v_ref, a_ref, y_ref, u_scr, st_scr, yp_scr, *, n_lat, n_ctx):
    n_all = n_lat + n_ctx
    n_t = LANES // S5_GC
    width = S5_T * S5_GC
    blk = lambda j: slice(j * LANES, (j + 1) * LANES)
    for t_hi in range(S5_T // n_t):
        x = jnp.concatenate([s_ref[pl.ds(n_t * t_hi + t_lo, n_all, stride=S5_T), :].astype(BF16)
                             for t_lo in range(n_t)], axis=1)
        xp = jnp.dot(x, pb_ref[...], preferred_element_type=F32).astype(BF16)
        for g in range(n_t):
            u_scr[g, :, blk(t_hi)] = xp[:, blk(g)]

    n_pairs = w_ref.shape[0]
    for pr in range(n_pairs):
        u_pair = jnp.concatenate([u_scr[2 * pr], u_scr[2 * pr + 1]], axis=1)
        st = jnp.dot(u_pair, w_ref[pr], preferred_element_type=F32)
        for q in range(4):
            st_scr[q, :, blk(pr)] = st[:, blk(q)]

    afr, afi, abr, abi = (a_ref[pl.ds(q, 1), :] for q in range(4))

    def scan_step(k, carry):
        hr, hi, gr, gi = carry
        rf = pl.ds(jnp.where(k < n_ctx, k + n_lat, k - n_ctx), 1)
        rb = pl.ds(n_all - 1 - k, 1)
        sfr, sfi, sbr, sbi = st_scr[0, rf, :], st_scr[1, rf, :], st_scr[2, rb, :], st_scr[3, rb, :]
        st_scr[0, rf, :] = hr
        st_scr[1, rf, :] = hi
        st_scr[2, rb, :] = gr
        st_scr[3, rb, :] = gi
        return (afr * hr - afi * hi + sfr, afr * hi + afi * hr + sfi,
                abr * gr - abi * gi + sbr, abr * gi + abi * gr + sbi)

    lax.fori_loop(0, n_all, scan_step, (jnp.zeros_like(afr),) * 4)

    for pr in range(n_pairs):
        hst = jnp.concatenate([st_scr[q, :, blk(pr)].astype(BF16) for q in range(4)], axis=1)
        y_pair = lax.dot_general(hst, v_ref[pr], (((1,), (1,)), ((), ())), preferred_element_type=F32)
        for g in range(2):
            y_g = (y_pair[:, g * width:(g + 1) * width]
                   + jnp.dot(u_scr[2 * pr + g], m_ref[pr, g], preferred_element_type=F32))
            for t_hi in range(S5_T // n_t):
                yp_scr[t_hi, :, blk(2 * pr + g)] = y_g[:, blk(t_hi)].astype(BF16)

    for t_hi in range(S5_T // n_t):
        z = jnp.dot(yp_scr[t_hi], pbt_ref[...], preferred_element_type=F32)
        for t_lo in range(n_t):
            y_ref[pl.ds(n_t * t_hi + t_lo, n_all, stride=S5_T), :] = z[:, blk(t_lo)]


def _s5_mix(s_all, mats, *, n_lat_rows):
    m_pair, w_state, v_state, a_pow = mats
    batch, n_rows, _ = s_all.shape
    n_lat = n_lat_rows // S5_T
    n_all = n_rows // S5_T
    width = S5_T * S5_GC
    n_blocks = S5_W // LANES
    pairs = S5_PAIRS // n_blocks
    perm = _s5_fold_perm()
    rows_spec = pl.BlockSpec((None, n_rows, LANES), lambda b, v: (b, 0, v))
    return pl.pallas_call(
        functools.partial(_s5_kernel, n_lat=n_lat, n_ctx=n_all - n_lat),
        grid=(batch, n_blocks),
        in_specs=[rows_spec, _const_spec(perm.shape), _const_spec(perm.shape),
                  pl.BlockSpec((pairs, 2, width, width), lambda b, v: (v, 0, 0, 0)),
                  pl.BlockSpec((pairs, 2 * width, 4 * LANES), lambda b, v: (v, 0, 0)),
                  pl.BlockSpec((pairs, 4 * LANES, 2 * width), lambda b, v: (v, 0, 0)),
                  pl.BlockSpec((4, pairs * LANES), lambda b, v: (0, v))],
        out_specs=rows_spec,
        out_shape=jax.ShapeDtypeStruct(s_all.shape, F32),
        scratch_shapes=[pltpu.VMEM((2 * pairs, n_all, width), BF16), pltpu.VMEM((4, n_all, pairs * LANES), F32),
                        pltpu.VMEM((S5_T * S5_GC // LANES, n_all, 2 * pairs * LANES), BF16)],
        compiler_params=_cparams(2),
        name="s5_mix",
    )(s_all, jnp.asarray(perm, BF16), jnp.asarray(perm.T, BF16), m_pair, w_state, v_state, a_pow)


def kernel(x, c, ctx, c_ctx, w_mod, b_mod, w_out, b_out, ln_mix_g, ln_mix_b, w_ffn1, b_ffn1, w_ffn2,
           b_ffn2, ln_ffn_g, ln_ffn_b, w_in_ab, lam_q1, lam_k1, lam_q2, lam_k2, subln_g, w_in_cd,
           s5_lam_re, s5_lam_im, s5_log_dt, s5_b_re, s5_b_im, s5_c_re, s5_c_im, s5_d, w_glu, b_glu,
           w_sp, b_sp):
    batch, n_lat, d = x.shape
    n_ctx = ctx.shape[1]
    assert d == D_MODEL and n_lat % ROW_TILE == 0 and n_ctx % CHUNK == 0 and batch + 1 <= MOD_ROWS
    mod = _modulation(c, c_ctx, w_mod, b_mod)
    xl = x.reshape(batch * n_lat, d)
    xc = ctx.reshape(batch * n_ctx, d)
    tiles = n_lat // ROW_TILE
    lat_rows = dict(tm=ROW_TILE, row_fn=lambda i: i // tiles)
    ctx_rows = dict(tm=n_ctx, row_fn=lambda i: batch)
    rope_tabs = _rope_tables(n_lat)
    for l in range(DEPTH):
        last = l == DEPTH - 1
        e = l // 2
        modl = mod[l]
        tail_w = (w_out[l].astype(BF16), b_out[l], ln_mix_g[l], ln_mix_b[l], w_ffn1[l].astype(BF16), b_ffn1[l],
                  w_ffn2[l].astype(BF16), b_ffn2[l], ln_ffn_g[l], ln_ffn_b[l])
        if l % 2 == 0:
            w_in = w_in_ab[e].astype(BF16)
            lam_init = 0.8 - 0.6 * math.exp(-0.3 * l)
            lam_vecs = (lam_q1[e], lam_k1[e], lam_q2[e], lam_k2[e])
            n_keys = n_lat + n_ctx
            f, q, k_all, vt_all = _inproj_even(xl, modl, w_in, rope_tabs, None, batch=batch, n_keys=n_keys,
                                               key_row0=0, **lat_rows)
            fc, qc, k_all, vt_all = _inproj_even(xc, modl, w_in, None, (k_all, vt_all), batch=batch, n_keys=n_keys,
                                                 key_row0=n_lat, **ctx_rows)
            attend = functools.partial(_diff_attention, lam_vecs=lam_vecs, subln_g=subln_g[e], lam_init=lam_init,
                                       batch=batch)
            ya = attend(q, k_all, vt_all, tq=ATTN_TQ, key_row0=0, n_keys=n_keys)
            yf = _fourier_mix(f, batch=batch)
            xl = _layer_tail(xl, yf, ya, modl, None, *tail_w, **lat_rows)
            if not last:
                yac = attend(qc, k_all, vt_all, tq=n_ctx, key_row0=n_lat, n_keys=n_ctx)
                yfc = _fourier_mix(fc, batch=batch)
                xc = _layer_tail(xc, yfc, yac, modl, None, *tail_w, **ctx_rows)
        else:
            assert last, "the S5 / gMLP layer is only implemented as the final layer"
            w_in = w_in_cd[e].astype(BF16)
            bsp_lanes = jnp.broadcast_to(b_sp[e][:, :, None], (GMLP_GROUPS, CHUNK, GMLP_GC))
            s_all, gm = _inproj_odd(xl, modl, w_in, w_sp[e].astype(BF16), bsp_lanes, batch=batch,
                                    n_rows=n_lat + n_ctx, **lat_rows)
            s_all = _ln_mod_matmul_into(xc, modl, w_in[:, :S5_W], s_all, batch=batch, row0=n_lat, **ctx_rows)
            mats = _s5_matrices(s5_lam_re[e], s5_lam_im[e], s5_log_dt[e], s5_b_re[e], s5_b_im[e], s5_c_re[e],
                                s5_c_im[e], s5_d[e])
            ys = _s5_mix(s_all, mats, n_lat_rows=n_lat)
            xl = _layer_tail(xl, ys, gm, modl, (w_glu[e].astype(BF16), b_glu[e]), *tail_w, **lat_rows)
    return xl.reshape(batch, n_lat, d)
```

```python
import functools
import math

import numpy as np
import jax
import jax.numpy as jnp
from jax import lax
from jax.experimental import pallas as pl
from jax.experimental.pallas import tpu as pltpu

D_MODEL = 1024
DEPTH = 2
GRID_W = 64
FOURIER_W = D_MODEL // 4
FOURIER_GC = 64
DIFF_HD = 64
DIFF_VD = 2 * DIFF_HD
DIFF_HEADS = (D_MODEL - FOURIER_W) // DIFF_VD
QK_W = DIFF_HEADS * 2 * DIFF_HD
V_W = DIFF_HEADS * DIFF_VD
EVEN_IN_W = FOURIER_W + 2 * QK_W + V_W
DIFF_SCALE = DIFF_HD ** -0.5
ROPE_BASE = 10000.0
ROPE_FREQS = DIFF_HD // 4
S5_W = D_MODEL // 2
S5_GC = 16
S5_GROUPS = S5_W // S5_GC
S5_STATE = 64
GMLP_W = D_MODEL // 2
GMLP_GC = 128
GMLP_GROUPS = GMLP_W // GMLP_GC
CHUNK = 128
ODD_IN_W = S5_W + 2 * GMLP_W
FFN_W = 4 * D_MODEL
LN_EPS = 1e-5
ALPHA = (2 * DEPTH) ** 0.25

F32 = jnp.float32
BF16 = jnp.bfloat16
HIGHEST = lax.Precision.HIGHEST

LANES = 128
MOD_ROWS = 8
ROW_TILE = 1024
ATTN_TQ = 2048
ATTN_TK = 256
ATTN_GROUP = 3
ATTN_SUM_ROWS = 16
FFT_BATCH = 4
S5_T = 16
S5_PAIRS = S5_GROUPS // 2
FFN_CHUNK = 1024
ODD_PART_ROWS = 256
TAIL_TILE = 1024
TAIL_PART_ROWS = 256
VMEM_LIMIT = 56 * 2 ** 20
Q_SCALE = DIFF_SCALE * math.log2(math.e)


def _cparams(n_axes, vmem=VMEM_LIMIT):
    return pltpu.CompilerParams(dimension_semantics=("arbitrary",) * n_axes, vmem_limit_bytes=vmem)


def _const_spec(shape):
    zeros = (0,) * len(shape)
    return pl.BlockSpec(shape, lambda *_: zeros, pipeline_mode=pl.Buffered(1))


def _ln(x):
    xc = x - jnp.mean(x, -1, keepdims=True)
    var = jnp.mean(xc * xc, -1, keepdims=True)
    return xc * lax.rsqrt(var + LN_EPS)


def _mm(a, b):
    return jnp.dot(a.astype(BF16), b.astype(BF16), preferred_element_type=F32)


def _mod_spec(row_fn, which):
    return pl.BlockSpec((None, 1, D_MODEL), lambda i: (row_fn(i) * 6 + which, 0, 0))


def _mod_kernel(c_ref, w_ref, b_ref, o_ref):
    c = c_ref[...]
    a = c * jax.nn.sigmoid(c)
    w = w_ref[...]
    a_hi, w_hi = a.astype(BF16), w.astype(BF16)
    a_lo = (a - a_hi.astype(F32)).astype(BF16)
    w_lo = (w - w_hi.astype(F32)).astype(BF16)
    on_hi = jnp.dot(jnp.concatenate([a_hi, a_lo], axis=0), w_hi, preferred_element_type=F32)
    o_ref[...] = (on_hi[:MOD_ROWS] + on_hi[MOD_ROWS:] + jnp.dot(a_hi, w_lo, preferred_element_type=F32)
                  + b_ref[...])


def _modulation(c, c_ctx, w_mod, b_mod):
    batch = c.shape[0]
    rows = jnp.concatenate([c, c_ctx[None], jnp.zeros((MOD_ROWS - batch - 1, D_MODEL), F32)], axis=0)
    out = pl.pallas_call(
        _mod_kernel,
        grid=(DEPTH, 6),
        in_specs=[
            pl.BlockSpec((MOD_ROWS, D_MODEL), lambda l, j: (0, 0)),
            pl.BlockSpec((None, D_MODEL, D_MODEL), lambda l, j: (l, 0, j)),
            pl.BlockSpec((None, 1, D_MODEL), lambda l, j: (l, 0, j)),
        ],
        out_specs=pl.BlockSpec((None, MOD_ROWS, D_MODEL), lambda l, j: (l, 0, j)),
        out_shape=jax.ShapeDtypeStruct((DEPTH, MOD_ROWS, 6 * D_MODEL), F32),
        compiler_params=_cparams(2),
        name="modulation",
    )(rows, w_mod, b_mod.reshape(DEPTH, 1, 6 * D_MODEL))
    return out.reshape(DEPTH, MOD_ROWS * 6, 1, D_MODEL)


def _inproj_even_kernel(*refs, rope):
    if rope:
        x_ref, sh_ref, sc_ref, w_ref, cos_ref, sin_ref, f_ref, q_ref, k_ref, vt_ref = refs
    else:
        x_ref, sh_ref, sc_ref, w_ref, _, _, f_ref, q_ref, k_ref, vt_ref = refs
    width = 2 * LANES
    part = vt_ref.shape[2]
    rows = lambda r: slice(r * part, (r + 1) * part)
    norm = lambda r: (_ln(x_ref[rows(r), :]) * (1.0 + sc_ref[...]) + sh_ref[...]).astype(BF16)
    if rope:
        lane = lax.broadcasted_iota(jnp.int32, (part, width), 1)
        first_half = (lane & ROPE_FREQS) == 0
    v0 = FOURIER_W + 2 * QK_W
    n_parts = vt_ref.shape[0]
    h_next = norm(0)
    for r in range(n_parts):
        h = h_next
        f_ref[rows(r), :] = jnp.dot(h, w_ref[:, :FOURIER_W], preferred_element_type=F32)
        if r + 1 < n_parts:
            h_next = norm(r + 1)
        for j in range(QK_W // width):
            for base, o_ref, scale in ((FOURIER_W, q_ref, Q_SCALE), (FOURIER_W + QK_W, k_ref, None)):
                t = jnp.dot(h, w_ref[:, base + j * width:base + (j + 1) * width], preferred_element_type=F32)
                if rope:
                    partner = jnp.where(first_half, pltpu.roll(t, width - ROPE_FREQS, 1),
                                        pltpu.roll(t, ROPE_FREQS, 1))
                    t = t * cos_ref[rows(r), :] + partner * sin_ref[rows(r), :]
                if scale is not None:
                    t = t * scale
                o_ref[rows(r), j * width:(j + 1) * width] = t.astype(BF16)
        for j in range(V_W // width):
            v = jnp.dot(h, w_ref[:, v0 + j * width:v0 + (j + 1) * width], preferred_element_type=F32)
            vt_ref[r, j * width:(j + 1) * width, :] = v.T.astype(BF16)


def _inproj_even(x2d, modl, w_bf, rope_tabs, kv_all, *, batch, n_keys, key_row0, tm, row_fn):
    rows = x2d.shape[0]
    rope = rope_tabs is not None
    tiles_per_batch = rows // batch // tm
    blk0 = key_row0 // tm
    in_specs = [
        pl.BlockSpec((tm, D_MODEL), lambda i: (i, 0)),
        _mod_spec(row_fn, 0),
        _mod_spec(row_fn, 1),
        _const_spec((D_MODEL, EVEN_IN_W)),
    ]
    args = [x2d, modl, modl, w_bf]
    if rope:
        in_specs += [pl.BlockSpec((tm, 2 * LANES), lambda i: (i % tiles_per_batch, 0))] * 2
        args += list(rope_tabs)
        aliases = {}
    else:
        in_specs += [pl.BlockSpec(memory_space=pl.ANY)] * 2
        args += list(kv_all)
        aliases = {4: 2, 5: 3}
    kv_idx = lambda i: (i // tiles_per_batch, blk0 + i % tiles_per_batch, 0)
    return pl.pallas_call(
        functools.partial(_inproj_even_kernel, rope=rope),
        grid=(rows // tm,),
        in_specs=in_specs,
        out_specs=[
            pl.BlockSpec((tm, FOURIER_W), lambda i: (i, 0)),
            pl.BlockSpec((tm, QK_W), lambda i: (i, 0)),
            pl.BlockSpec((None, tm, QK_W), kv_idx),
            pl.BlockSpec((None, tm // ATTN_TK, V_W, ATTN_TK), lambda i: kv_idx(i) + (0,)),
        ],
        out_shape=[
            jax.ShapeDtypeStruct((rows, FOURIER_W), F32),
            jax.ShapeDtypeStruct((rows, QK_W), BF16),
            jax.ShapeDtypeStruct((batch, n_keys, QK_W), BF16),
            jax.ShapeDtypeStruct((batch, n_keys // ATTN_TK, V_W, ATTN_TK), BF16),
        ],
        input_output_aliases=aliases,
        compiler_params=_cparams(1),
        name="inproj_even_rope" if rope else "inproj_even",
    )(*args)


def _rope_tables(n):
    rows = n // GRID_W
    row = jnp.repeat(jnp.arange(rows, dtype=F32), GRID_W)
    col = jnp.tile(jnp.arange(GRID_W, dtype=F32), rows)
    inv = jnp.power(ROPE_BASE, -jnp.arange(ROPE_FREQS, dtype=F32) / ROPE_FREQS)
    ang = jnp.stack([row[:, None] * inv, col[:, None] * inv], axis=1)
    cos, sin = jnp.cos(ang), jnp.sin(ang)
    cos_map = jnp.concatenate([cos, cos], axis=-1).reshape(n, 4 * ROPE_FREQS)
    sin_map = jnp.concatenate([-sin, sin], axis=-1).reshape(n, 4 * ROPE_FREQS)
    reps = 2 * LANES // (4 * ROPE_FREQS)
    return jnp.tile(cos_map, (1, reps)), jnp.tile(sin_map, (1, reps))


def _attn_kernel(lq1, lk1, lq2, lk2, g_ref, q_ref, k_ref, vt_ref, o_ref, acc_ref, *scratch, lam_init):
    tq = q_ref.shape[0]
    n_chunks, _, tk = vt_ref.shape
    q = q_ref[...]
    lane = lax.broadcasted_iota(jnp.int32, q.shape, 1)
    zero = jnp.zeros_like(q)
    q_cat = jnp.concatenate([jnp.where(lane < DIFF_HD, q, zero), jnp.where(lane >= DIFF_HD, q, zero)], axis=0)
    nt_dims = (((1,), (1,)), ((), ()))

    p_ref, al_ref = scratch[-2:]
    grp = (len(scratch) - 2) // 2
    s_refs, mc_refs = scratch[:grp], scratch[grp:2 * grp]
    n_groups = n_chunks // grp

    def scores(g, r):
        kb = k_ref[pl.ds(pl.multiple_of((g * grp + r) * tk, tk), tk), :]
        s = lax.dot_general(kb, q_cat, nt_dims, preferred_element_type=F32)
        s_refs[r][...] = s
        mc_refs[r][...] = jnp.max(s, axis=0, keepdims=True)

    def group_max(m_prev):
        m = m_prev
        for r in range(grp):
            m = jnp.maximum(m, mc_refs[r][...])
        al_ref[...] = jnp.exp2(m_prev - m)
        return m

    def probs(r, m):
        p_ref[r * tk:(r + 1) * tk, :] = jnp.exp2(s_refs[r][...] - m).astype(BF16)

    ones_rows = (lax.broadcasted_iota(jnp.int32, (ATTN_SUM_ROWS, grp * tk), 0) == 0).astype(BF16)

    def values(g):
        vt = jnp.concatenate([vt_ref[g * grp + r] for r in range(grp)], axis=1)
        vt1 = jnp.concatenate([vt, ones_rows], axis=0)
        acc_ref[...] = acc_ref[...] * al_ref[...] + jnp.dot(vt1, p_ref[...], preferred_element_type=F32)

    acc_ref[...] = jnp.zeros_like(acc_ref)
    m = jnp.full((1, 2 * tq), -jnp.inf, F32)
    for r in range(grp):
        scores(0, r)
    m = group_max(m)
    for r in range(grp):
        probs(r, m)
        if n_groups > 1:
            scores(1, r)

    def body(g, m):
        values(g - 1)
        m = group_max(m)
        for r in range(grp):
            probs(r, m)
            scores(g + 1, r)
        return m

    if n_groups > 2:
        m = lax.fori_loop(1, n_groups - 1, body, m)
    if n_groups > 1:
        values(n_groups - 2)
        m = group_max(m)
        for r in range(grp):
            probs(r, m)
    values(n_groups - 1)

    lam = (jnp.exp(jnp.sum(lq1[...] * lk1[...], keepdims=True))
           - jnp.exp(jnp.sum(lq2[...] * lk2[...], keepdims=True)) + lam_init)
    on = acc_ref[:DIFF_VD, :] / acc_ref[DIFF_VD:DIFF_VD + 1, :]
    o = on[:, :tq] - lam * on[:, tq:]
    o = o * lax.rsqrt(jnp.mean(o * o, axis=0, keepdims=True) + LN_EPS)
    o_ref[...] = o.T * (g_ref[...] * (1.0 - lam_init))


def _diff_attention(q, k_all, vt_all, lam_vecs, subln_g, lam_init, *, batch, tq, key_row0, n_keys):
    rows = q.shape[0]
    nq = rows // batch // tq
    tk = vt_all.shape[3]
    n_chunks = n_keys // tk
    assert key_row0 % n_keys == 0
    kblk = key_row0 // n_keys
    grp = ATTN_GROUP if n_chunks % ATTN_GROUP == 0 else 1
    cols = 2 * tq
    slot_shapes = [((tk, cols), F32), ((1, cols), F32)]
    vec_spec = pl.BlockSpec((1, DIFF_HD), lambda b, h, i: (0, 0))
    in_specs = [vec_spec] * 4 + [
        pl.BlockSpec((1, DIFF_VD), lambda b, h, i: (0, 0)),
        pl.BlockSpec((tq, DIFF_VD), lambda b, h, i: (b * nq + i, h)),
        pl.BlockSpec((None, n_keys, DIFF_VD), lambda b, h, i: (b, kblk, h)),
        pl.BlockSpec((None, n_chunks, DIFF_VD, tk), lambda b, h, i: (b, kblk, h, 0)),
    ]
    args = [v.reshape(1, DIFF_HD) for v in lam_vecs] + [subln_g.reshape(1, DIFF_VD), q, k_all, vt_all]
    return pl.pallas_call(
        functools.partial(_attn_kernel, lam_init=lam_init),
        grid=(batch, DIFF_HEADS, nq),
        in_specs=in_specs,
        out_specs=pl.BlockSpec((tq, DIFF_VD), lambda b, h, i: (b * nq + i, h)),
        out_shape=jax.ShapeDtypeStruct((rows, V_W), F32),
        scratch_shapes=[pltpu.VMEM((DIFF_VD + ATTN_SUM_ROWS, cols), F32)]
        + [pltpu.VMEM(shape, dtype) for shape, dtype in slot_shapes for _ in range(grp)]
        + [pltpu.VMEM((grp * tk, cols), BF16), pltpu.VMEM((1, cols), F32)],
        compiler_params=_cparams(3),
        name="diff_attention" if n_chunks > 1 else "diff_attention_ctx",
    )(*args)


def _dft_cos_sin(n):
    idx = np.outer(np.arange(n), np.arange(n)) % n
    ang = 2.0 * np.pi * idx / n
    return np.cos(ang), np.sin(ang)


def _channel_dft():
    c, s = _dft_cos_sin(FOURIER_GC)
    eye = np.eye(LANES // FOURIER_GC)
    return jnp.asarray(np.kron(eye, c), F32).astype(BF16), jnp.asarray(np.kron(eye, s), F32).astype(BF16)


def _fourier_dense_kernel(x_ref, cl_ref, sl_ref, bdc_ref, bds_ref, o_ref, *, norm):
    x = x_ref[...]
    pr = _mm(cl_ref[...], x)
    pi = -_mm(sl_ref[...], x)
    out = _mm(pr, bdc_ref[...]) + _mm(pi, bds_ref[...])
    o_ref[...] = out * norm


def _fourier_fft_kernel(x_ref, f1_ref, twc_ref, tws_ref, f3_ref, bdc_ref, bds_ref, o_ref, ur_ref, ui_ref,
                        *, n1, norm):
    n2 = LANES

    nb = FFT_BATCH
    lanes = lambda j: slice(j * LANES, (j + 1) * LANES)

    def stage1(ib, carry):
        i0 = ib * nb
        xs = jnp.concatenate([x_ref[pl.ds(i0 + j, n2, stride=n1), :] for j in range(nb)], axis=1)
        t = _mm(f1_ref[...], xs)
        for j in range(nb):
            tr, ti = t[:n2, lanes(j)], t[n2:, lanes(j)]
            r0 = pl.multiple_of((i0 + j) * n2, n2)
            c = twc_ref[pl.ds(r0, n2), :]
            s = tws_ref[pl.ds(r0, n2), :]
            ur_ref[pl.ds(r0, n2), :] = tr * c + ti * s
            ui_ref[pl.ds(r0, n2), :] = ti * c - tr * s
        return carry

    lax.fori_loop(0, n1 // nb, stage1, 0)

    def stage2(kb, carry):
        k0 = kb * nb
        u = jnp.concatenate(
            [jnp.concatenate([ur_ref[pl.ds(k0 + j, n1, stride=n2), :], ui_ref[pl.ds(k0 + j, n1, stride=n2), :]],
                             axis=0) for j in range(nb)], axis=1)
        p = _mm(f3_ref[...], u)
        for j in range(nb):
            ur_ref[pl.ds(k0 + j, n1, stride=n2), :] = p[:n1, lanes(j)]
            ui_ref[pl.ds(k0 + j, n1, stride=n2), :] = p[n1:, lanes(j)]
        return carry

    lax.fori_loop(0, n2 // nb, stage2, 0)

    rows = ROW_TILE

    def stage3(r, carry):
        r0 = pl.multiple_of(r * rows, rows)
        out = _mm(ur_ref[pl.ds(r0, rows), :], bdc_ref[...]) + _mm(ui_ref[pl.ds(r0, rows), :], bds_ref[...])
        o_ref[pl.ds(r0, rows), :] = out * norm
        return carry

    lax.fori_loop(0, (n1 * n2) // rows, stage3, 0)


def _fourier_mix(f2d, *, batch):
    rows = f2d.shape[0]
    n = rows // batch
    norm = 1.0 / math.sqrt(n * FOURIER_GC)
    bdc, bds = _channel_dft()
    blk = pl.BlockSpec((n, LANES), lambda b, j: (b, j))
    grid = (batch, FOURIER_W // LANES)
    out_shape = jax.ShapeDtypeStruct((rows, FOURIER_W), F32)
    mat = _const_spec((LANES, LANES))
    if n <= ROW_TILE:
        c, s = _dft_cos_sin(n)
        return pl.pallas_call(
            functools.partial(_fourier_dense_kernel, norm=norm),
            grid=grid,
            in_specs=[blk, _const_spec((n, n)), _const_spec((n, n)), mat, mat],
            out_specs=blk,
            out_shape=out_shape,
            compiler_params=_cparams(2),
            name="fourier_dense",
        )(f2d, jnp.asarray(c, F32).astype(BF16), jnp.asarray(s, F32).astype(BF16), bdc, bds)
    n2 = LANES
    n1 = n // n2
    c2, s2 = _dft_cos_sin(n2)
    f1 = jnp.asarray(np.concatenate([c2, -s2], axis=0), F32).astype(BF16)
    c1, s1 = _dft_cos_sin(n1)
    f3 = jnp.asarray(np.block([[c1, s1], [-s1, c1]]), F32).astype(BF16)
    tw_idx = np.outer(np.arange(n1), np.arange(n2)).reshape(-1)
    tw_ang = 2.0 * np.pi * tw_idx / n
    twc = jnp.broadcast_to(jnp.asarray(np.cos(tw_ang), F32)[:, None], (n, LANES))
    tws = jnp.broadcast_to(jnp.asarray(np.sin(tw_ang), F32)[:, None], (n, LANES))
    return pl.pallas_call(
        functools.partial(_fourier_fft_kernel, n1=n1, norm=norm),
        grid=grid,
        in_specs=[blk, _const_spec((2 * n2, n2)), _const_spec((n, LANES)), _const_spec((n, LANES)),
                  _const_spec((2 * n1, 2 * n1)), mat, mat],
        out_specs=blk,
        out_shape=out_shape,
        scratch_shapes=[pltpu.VMEM((n, LANES), F32), pltpu.VMEM((n, LANES), F32)],
        compiler_params=_cparams(2),
        name="fourier_fft",
    )(f2d, f1, twc, tws, f3, bdc, bds)


def _gelu_tanh(x):
    return 0.5 * x * (1.0 + jnp.tanh(math.sqrt(2.0 / math.pi) * (x + 0.044715 * (x * x * x))))


def _tail_kernel(*refs, glu, wa):
    if glu:
        x_ref, ya_ref, yb_ref, wg_ref, bg_ref, *rest = refs
    else:
        x_ref, ya_ref, yb_ref, *rest = refs
    (wo_ref, bo_ref, g1_ref, lg1_ref, lb1_ref, sh2_ref, sc2_ref, g2_ref,
     w1_ref, b1_ref, w2_ref, b2_ref, lg2_ref, lb2_ref, o_ref) = rest
    part = min(TAIL_PART_ROWS, x_ref.shape[0])
    n_parts = x_ref.shape[0] // part
    rows = lambda r: slice(r * part, (r + 1) * part)

    def mix_out(r):
        ya = ya_ref[rows(r), :]
        if glu:
            g = _gelu_tanh(ya)
            ya = g * jax.nn.sigmoid(jnp.dot(g.astype(BF16), wg_ref[...], preferred_element_type=F32) + bg_ref[...])
        y = (jnp.dot(ya.astype(BF16), wo_ref[:wa, :], preferred_element_type=F32)
             + jnp.dot(yb_ref[rows(r), :].astype(BF16), wo_ref[wa:, :], preferred_element_type=F32) + bo_ref[...])
        x1 = _ln(ALPHA * x_ref[rows(r), :] + g1_ref[...] * y) * lg1_ref[...] + lb1_ref[...]
        return x1, (_ln(x1) * (1.0 + sc2_ref[...]) + sh2_ref[...]).astype(BF16)

    def ffn_chunk(h, c):
        cols = slice(c * FFN_CHUNK, (c + 1) * FFN_CHUNK)
        a = jnp.maximum(jnp.dot(h, w1_ref[:, cols], preferred_element_type=F32) + b1_ref[:, cols], 0.0)
        return jnp.dot((a * a).astype(BF16), w2_ref[cols, :], preferred_element_type=F32)

    def finish(r, x1, y2):
        o_ref[rows(r), :] = _ln(ALPHA * x1 + g2_ref[...] * (y2 + b2_ref[...])) * lg2_ref[...] + lb2_ref[...]

    state = mix_out(0)
    done = None
    for r in range(n_parts):
        x1, h = state
        y2 = ffn_chunk(h, 0)
        if r + 1 < n_parts:
            state = mix_out(r + 1)
        if done is not None:
            finish(*done)
        for c in range(1, FFN_W // FFN_CHUNK):
            y2 = y2 + ffn_chunk(h, c)
        done = (r, x1, y2)
    finish(*done)


def _layer_tail(x2d, ya, yb, modl, glu_params, w_out, b_out, lg1, lb1, w1, b1, w2, b2, lg2, lb2, *, tm, row_fn):
    rows = x2d.shape[0]
    wa, wb = ya.shape[-1], yb.shape[1]
    glu = glu_params is not None
    row = lambda v: v.reshape(1, -1)
    vec = _const_spec((1, D_MODEL))
    if ya.ndim == 3:
        tiles_per_batch = rows // ya.shape[0] // tm
        ya_spec = pl.BlockSpec((None, tm, wa), lambda i: (i // tiles_per_batch, i % tiles_per_batch, 0))
    else:
        ya_spec = pl.BlockSpec((tm, wa), lambda i: (i, 0))
    in_specs = [pl.BlockSpec((tm, D_MODEL), lambda i: (i, 0)), ya_spec,
                pl.BlockSpec((tm, wb), lambda i: (i, 0))]
    args = [x2d, ya, yb]
    if glu:
        in_specs += [_const_spec((wa, wa)), _const_spec((1, wa))]
        args += [glu_params[0], row(glu_params[1])]
    in_specs += [_const_spec((D_MODEL, D_MODEL)), vec, _mod_spec(row_fn, 2), vec, vec,
                 _mod_spec(row_fn, 3), _mod_spec(row_fn, 4), _mod_spec(row_fn, 5),
                 _const_spec((D_MODEL, FFN_W)), _const_spec((1, FFN_W)), _const_spec((FFN_W, D_MODEL)), vec, vec, vec]
    args += [w_out, row(b_out), modl, row(lg1), row(lb1), modl, modl, modl,
             w1, row(b1), w2, row(b2), row(lg2), row(lb2)]
    return pl.pallas_call(
        functools.partial(_tail_kernel, glu=glu, wa=wa),
        grid=(rows // tm,),
        in_specs=in_specs,
        out_specs=pl.BlockSpec((tm, D_MODEL), lambda i: (i, 0)),
        out_shape=jax.ShapeDtypeStruct((rows, D_MODEL), F32),
        compiler_params=_cparams(1),
        name="layer_tail_glu" if glu else "layer_tail",
    )(*args)


def _inproj_odd_kernel(x_ref, sh_ref, sc_ref, w_ref, wsp_ref, bsp_ref, s_ref, gm_ref):
    part = min(ODD_PART_ROWS, x_ref.shape[0])
    n_parts = x_ref.shape[0] // part
    n_ch = part // CHUNK
    rows = lambda r: slice(r * part, (r + 1) * part)
    norm = lambda r: (_ln(x_ref[rows(r), :]) * (1.0 + sc_ref[...]) + sh_ref[...]).astype(BF16)

    def gate(r, u, v):
        chunk = lambda ch: slice(ch * CHUNK, (ch + 1) * CHUNK)
        for g in range(GMLP_GROUPS):
            cols = slice(g * GMLP_GC, (g + 1) * GMLP_GC)
            vn = jnp.concatenate([_ln(v[chunk(ch), cols]).astype(BF16) for ch in range(n_ch)], axis=1)
            sp = jnp.dot(wsp_ref[g], vn, preferred_element_type=F32)
            for ch in range(n_ch):
                out_rows = slice(r * part + ch * CHUNK, r * part + (ch + 1) * CHUNK)
                gm_ref[out_rows, cols] = u[chunk(ch), cols] * (sp[:, ch * GMLP_GC:(ch + 1) * GMLP_GC] + bsp_ref[g])

    h_next = norm(0)
    pending = None
    for r in range(n_parts):
        h = h_next
        s_ref[rows(r), :] = jnp.dot(h, w_ref[:, :S5_W], preferred_element_type=F32)
        if pending is not None:
            gate(*pending)
        u = jnp.dot(h, w_ref[:, S5_W:S5_W + GMLP_W], preferred_element_type=F32)
        if r + 1 < n_parts:
            h_next = norm(r + 1)
        v = jnp.dot(h, w_ref[:, S5_W + GMLP_W:], preferred_element_type=F32)
        pending = (r, u, v)
    gate(*pending)


def _inproj_odd(x2d, modl, w_bf, wsp_bf, bsp_lanes, *, batch, n_rows, tm, row_fn):
    rows = x2d.shape[0]
    tiles_per_batch = rows // batch // tm
    return pl.pallas_call(
        _inproj_odd_kernel,
        grid=(rows // tm,),
        in_specs=[pl.BlockSpec((tm, D_MODEL), lambda i: (i, 0)), _mod_spec(row_fn, 0), _mod_spec(row_fn, 1),
                  _const_spec((D_MODEL, ODD_IN_W)), _const_spec((GMLP_GROUPS, CHUNK, CHUNK)),
                  _const_spec((GMLP_GROUPS, CHUNK, GMLP_GC))],
        out_specs=[pl.BlockSpec((None, tm, S5_W), lambda i: (i // tiles_per_batch, i % tiles_per_batch, 0)),
                   pl.BlockSpec((tm, GMLP_W), lambda i: (i, 0))],
        out_shape=[jax.ShapeDtypeStruct((batch, n_rows, S5_W), F32), jax.ShapeDtypeStruct((rows, GMLP_W), F32)],
        compiler_params=_cparams(1),
        name="inproj_odd_gmlp",
    )(x2d, modl, modl, w_bf, wsp_bf, bsp_lanes)


def _ln_mod_matmul_kernel(x_ref, sh_ref, sc_ref, w_ref, _, o_ref):
    h = (_ln(x_ref[...]) * (1.0 + sc_ref[...]) + sh_ref[...]).astype(BF16)
    o_ref[...] = jnp.dot(h, w_ref[...], preferred_element_type=F32)


def _ln_mod_matmul_into(x2d, modl, w_bf, dst, *, batch, row0, tm, row_fn):
    rows, n = x2d.shape[0], w_bf.shape[1]
    tiles_per_batch = rows // batch // tm
    blk0 = row0 // tm
    return pl.pallas_call(
        _ln_mod_matmul_kernel,
        grid=(rows // tm,),
        in_specs=[pl.BlockSpec((tm, D_MODEL), lambda i: (i, 0)), _mod_spec(row_fn, 0), _mod_spec(row_fn, 1),
                  _const_spec((D_MODEL, n)), pl.BlockSpec(memory_space=pl.ANY)],
        out_specs=pl.BlockSpec((None, tm, n), lambda i: (i // tiles_per_batch, blk0 + i % tiles_per_batch, 0)),
        out_shape=jax.ShapeDtypeStruct(dst.shape, dst.dtype),
        input_output_aliases={4: 0},
        compiler_params=_cparams(1),
        name="ln_mod_matmul",
    )(x2d, modl, modl, w_bf, dst)


def _s5_prep_kernel(pw_ref, bb_ref, cc_ref, dw_ref, w_ref, vt_ref, m_ref):
    t_len, gc = S5_T, S5_GC
    width = t_len * gc
    nt_dims = (((1,), (1,)), ((), ()))
    cols = lambda q: slice(q * LANES, (q + 1) * LANES)
    lane = lax.broadcasted_iota(jnp.int32, (gc, width), 1)
    for g in range(2):
        wide = []
        for d in range(2):
            pr, pi = pw_ref[g, d, 0], pw_ref[g, d, 1]
            bbr, bbi = bb_ref[g, d, 0], bb_ref[g, d, 1]
            cr, ci = cc_ref[g, d, 0], cc_ref[g, d, 1]

            def power(xr, xi, t, pr=pr, pi=pi):
                return pr[t:t + 1] * xr - pi[t:t + 1] * xi, pr[t:t + 1] * xi + pi[t:t + 1] * xr

            lc = [power(cr, ci, t) for t in range(t_len + 1)]
            for s in range(t_len):
                blk = slice(g * width + s * gc, g * width + (s + 1) * gc)
                lbr, lbi = power(bbr, bbi, t_len - 1 - s if d == 0 else s)
                w_ref[blk, cols(2 * d)] = lbr.astype(BF16)
                w_ref[blk, cols(2 * d + 1)] = lbi.astype(BF16)
                lcr, lci = lc[s + 1 if d == 0 else t_len - s]
                vt_ref[blk, cols(2 * d)] = lcr.astype(BF16)
                vt_ref[blk, cols(2 * d + 1)] = (-lci).astype(BF16)
            order = range(t_len) if d == 0 else range(t_len - 1, -1, -1)
            lcr_all = jnp.concatenate([lc[t][0] for t in order], axis=0)
            lci_all = jnp.concatenate([lc[t][1] for t in order], axis=0)
            wide.append(lax.dot_general(bbr, lcr_all, nt_dims, preferred_element_type=F32, precision=HIGHEST)
                        - lax.dot_general(bbi, lci_all, nt_dims, preferred_element_type=F32, precision=HIGHEST))
        kf = wide[0] + dw_ref[g]
        kb = wide[1]
        for s in range(t_len):
            fwd = jnp.where(lane >= s * gc, pltpu.roll(kf, s * gc, 1), 0.0)
            bwd = jnp.where(lane < (s + 1) * gc, pltpu.roll(kb, (s + 1) * gc % width, 1), 0.0)
            m_ref[g, s * gc:(s + 1) * gc, :] = (fwd + bwd).astype(BF16)


def _s5_matrices(lam_re, lam_im, log_dt, b_re, b_im, c_re, c_im, d_skip):
    t_len, g_n, p_n, i_n = S5_T, S5_GROUPS, S5_STATE, S5_GC
    tau_rows = -(-(t_len + 1) // 8) * 8
    dt = jnp.exp(log_dt)[..., None]
    mag, th = lam_re * dt, lam_im * dt
    er = jnp.exp(mag)
    lbr, lbi = er * jnp.cos(th), er * jnp.sin(th)
    den = lam_re * lam_re + lam_im * lam_im
    nr, ni = lbr - 1.0, lbi
    cr = (nr * lam_re + ni * lam_im) / den
    ci = (ni * lam_re - nr * lam_im) / den
    tau = jnp.arange(tau_rows, dtype=F32)[:, None]
    pm = jnp.exp(tau * mag[:, :, None, :])
    ang = tau * th[:, :, None, :]
    pw = jnp.stack([pm * jnp.cos(ang), pm * jnp.sin(ang)], axis=2)
    to_ip = lambda a: jnp.swapaxes(a, -1, -2)
    bb = jnp.stack([cr[..., None, :] * to_ip(b_re) - ci[..., None, :] * to_ip(b_im),
                    cr[..., None, :] * to_ip(b_im) + ci[..., None, :] * to_ip(b_re)], axis=2)
    cc = jnp.stack([c_re, c_im], axis=2)

    def per_group(a):
        a = jnp.swapaxes(a, 0, 1)
        a = a.reshape((g_n // 2, 2) + a.shape[1:])
        zero = jnp.zeros_like(a[:, 0])
        both = jnp.stack([jnp.concatenate([a[:, 0], zero], axis=-1), jnp.concatenate([zero, a[:, 1]], axis=-1)], axis=1)
        return both.reshape((g_n,) + both.shape[2:])

    width = t_len * i_n
    d_wide = jnp.pad(d_skip.reshape(g_n, i_n)[:, None, :] * jnp.asarray(np.eye(i_n), F32),
                     ((0, 0), (0, 0), (0, width - i_n)))
    pair = lambda shape: pl.BlockSpec((2,) + shape, lambda j: (j,) + (0,) * len(shape))
    state_spec = pl.BlockSpec((None, 2 * width, 4 * LANES), lambda j: (j, 0, 0))
    state_shape = jax.ShapeDtypeStruct((S5_PAIRS, 2 * width, 4 * LANES), BF16)
    w_state, vt_state, m_pair = pl.pallas_call(
        _s5_prep_kernel,
        grid=(S5_PAIRS,),
        in_specs=[pair((2, 2, tau_rows, 2 * p_n)), pair((2, 2, i_n, 2 * p_n)), pair((2, 2, i_n, 2 * p_n)),
                  pair((i_n, width))],
        out_specs=[state_spec, state_spec, pl.BlockSpec((None, 2, width, width), lambda j: (j, 0, 0, 0))],
        out_shape=[state_shape, state_shape, jax.ShapeDtypeStruct((S5_PAIRS, 2, width, width), BF16)],
        compiler_params=_cparams(1),
        name="s5_prep",
    )(per_group(pw), per_group(bb), per_group(cc), d_wide)
    a_pow = jnp.stack([pw[0, :, 0, t_len], pw[0, :, 1, t_len], pw[1, :, 0, t_len], pw[1, :, 1, t_len]], axis=0)
    return m_pair, w_state, vt_state, a_pow.reshape(4, g_n * p_n)


def _s5_fold_perm():
    n_t, n_g = LANES // S5_GC, LANES // S5_GC
    src = np.arange(n_t * LANES)
    t_lo, g, i = src // LANES, (src % LANES) // S5_GC, src % S5_GC
    perm = np.zeros((n_t * LANES, n_g * LANES), np.float32)
    perm[src, g * LANES + t_lo * S5_GC + i] = 1.0
    return perm


def _s5_kernel(s_ref, pb_ref, pbt_ref, m_ref, w_ref, v_ref, a_ref, y_ref, u_scr, st_scr, yp_scr, *, n_lat, n_ctx):
    n_all = n_lat + n_ctx
    n_t = LANES // S5_GC
    width = S5_T * S5_GC
    blk = lambda j: slice(j * LANES, (j + 1) * LANES)
    for t_hi in range(S5_T // n_t):
        x = jnp.concatenate([s_ref[pl.ds(n_t * t_hi + t_lo, n_all, stride=S5_T), :].astype(BF16)
                             for t_lo in range(n_t)], axis=1)
        xp = jnp.dot(x, pb_ref[...], preferred_element_type=F32).astype(BF16)
        for g in range(n_t):
            u_scr[g, :, blk(t_hi)] = xp[:, blk(g)]

    n_pairs = w_ref.shape[0]
    for pr in range(n_pairs):
        u_pair = jnp.concatenate([u_scr[2 * pr], u_scr[2 * pr + 1]], axis=1)
        st = jnp.dot(u_pair, w_ref[pr], preferred_element_type=F32)
        for q in range(4):
            st_scr[q, :, blk(pr)] = st[:, blk(q)]

    afr, afi, abr, abi = (a_ref[pl.ds(q, 1), :] for q in range(4))

    def scan_step(k, carry):
        hr, hi, gr, gi = carry
        rf = pl.ds(jnp.where(k < n_ctx, k + n_lat, k - n_ctx), 1)
        rb = pl.ds(n_all - 1 - k, 1)
        sfr, sfi, sbr, sbi = st_scr[0, rf, :], st_scr[1, rf, :], st_scr[2, rb, :], st_scr[3, rb, :]
        st_scr[0, rf, :] = hr
        st_scr[1, rf, :] = hi
        st_scr[2, rb, :] = gr
        st_scr[3, rb, :] = gi
        return (afr * hr - afi * hi + sfr, afr * hi + afi * hr + sfi,
                abr * gr - abi * gi + sbr, abr * gi + abi * gr + sbi)

    lax.fori_loop(0, n_all, scan_step, (jnp.zeros_like(afr),) * 4)

    for pr in range(n_pairs):
        hst = jnp.concatenate([st_scr[q, :, blk(pr)].astype(BF16) for q in range(4)], axis=1)
        y_pair = lax.dot_general(hst, v_ref[pr], (((1,), (1,)), ((), ())), preferred_element_type=F32)
        for g in range(2):
            y_g = (y_pair[:, g * width:(g + 1) * width]
                   + jnp.dot(u_scr[2 * pr + g], m_ref[pr, g], preferred_element_type=F32))
            for t_hi in range(S5_T // n_t):
                yp_scr[t_hi, :, blk(2 * pr + g)] = y_g[:, blk(t_hi)].astype(BF16)

    for t_hi in range(S5_T // n_t):
        z = jnp.dot(yp_scr[t_hi], pbt_ref[...], preferred_element_type=F32)
        for t_lo in range(n_t):
            y_ref[pl.ds(n_t * t_hi + t_lo, n_all, stride=S5_T), :] = z[:, blk(t_lo)]


def _s5_mix(s_all, mats, *, n_lat_rows):
    m_pair, w_state, v_state, a_pow = mats
    batch, n_rows, _ = s_all.shape
    n_lat = n_lat_rows // S5_T
    n_all = n_rows // S5_T
    width = S5_T * S5_GC
    n_blocks = S5_W // LANES
    pairs = S5_PAIRS // n_blocks
    perm = _s5_fold_perm()
    rows_spec = pl.BlockSpec((None, n_rows, LANES), lambda b, v: (b, 0, v))
    return pl.pallas_call(
        functools.partial(_s5_kernel, n_lat=n_lat, n_ctx=n_all - n_lat),
        grid=(batch, n_blocks),
        in_specs=[rows_spec, _const_spec(perm.shape), _const_spec(perm.shape),
                  pl.BlockSpec((pairs, 2, width, width), lambda b, v: (v, 0, 0, 0)),
                  pl.BlockSpec((pairs, 2 * width, 4 * LANES), lambda b, v: (v, 0, 0)),
                  pl.BlockSpec((pairs, 4 * LANES, 2 * width), lambda b, v: (v, 0, 0)),
                  pl.BlockSpec((4, pairs * LANES), lambda b, v: (0, v))],
        out_specs=rows_spec,
        out_shape=jax.ShapeDtypeStruct(s_all.shape, F32),
        scratch_shapes=[pltpu.VMEM((2 * pairs, n_all, width), BF16), pltpu.VMEM((4, n_all, pairs * LANES), F32),
                        pltpu.VMEM((S5_T * S5_GC // LANES, n_all, 2 * pairs * LANES), BF16)],
        compiler_params=_cparams(2),
        name="s5_mix",
    )(s_all, jnp.asarray(perm, BF16), jnp.asarray(perm.T, BF16), m_pair, w_state, v_state, a_pow)


def kernel(x, c, ctx, c_ctx, w_mod, b_mod, w_out, b_out, ln_mix_g, ln_mix_b, w_ffn1, b_ffn1, w_ffn2,
           b_ffn2, ln_ffn_g, ln_ffn_b, w_in_ab, lam_q1, lam_k1, lam_q2, lam_k2, subln_g, w_in_cd,
           s5_lam_re, s5_lam_im, s5_log_dt, s5_b_re, s5_b_im, s5_c_re, s5_c_im, s5_d, w_glu, b_glu,
           w_sp, b_sp):
    batch, n_lat, d = x.shape
    n_ctx = ctx.shape[1]
    assert d == D_MODEL and n_lat % TAIL_TILE == 0 and n_ctx % CHUNK == 0 and batch + 1 <= MOD_ROWS
    mod = _modulation(c, c_ctx, w_mod, b_mod)
    xl = x.reshape(batch * n_lat, d)
    xc = ctx.reshape(batch * n_ctx, d)
    tiles = n_lat // ROW_TILE
    lat_rows = dict(tm=ROW_TILE, row_fn=lambda i: i // tiles)
    tail_tiles = n_lat // TAIL_TILE
    lat_tail_rows = dict(tm=TAIL_TILE, row_fn=lambda i: i // tail_tiles)
    ctx_rows = dict(tm=n_ctx, row_fn=lambda i: batch)
    rope_tabs = _rope_tables(n_lat)
    for l in range(DEPTH):
        last = l == DEPTH - 1
        e = l // 2
        modl = mod[l]
        tail_w = (w_out[l].astype(BF16), b_out[l], ln_mix_g[l], ln_mix_b[l], w_ffn1[l].astype(BF16), b_ffn1[l],
                  w_ffn2[l].astype(BF16), b_ffn2[l], ln_ffn_g[l], ln_ffn_b[l])
        if l % 2 == 0:
            w_in = w_in_ab[e].astype(BF16)
            lam_init = 0.8 - 0.6 * math.exp(-0.3 * l)
            lam_vecs = (lam_q1[e], lam_k1[e], lam_q2[e], lam_k2[e])
            n_keys = n_lat + n_ctx
            f, q, k_all, vt_all = _inproj_even(xl, modl, w_in, rope_tabs, None, batch=batch, n_keys=n_keys,
                                               key_row0=0, **lat_rows)
            fc, qc, k_all, vt_all = _inproj_even(xc, modl, w_in, None, (k_all, vt_all), batch=batch, n_keys=n_keys,
                                                 key_row0=n_lat, **ctx_rows)
            attend = functools.partial(_diff_attention, lam_vecs=lam_vecs, subln_g=subln_g[e], lam_init=lam_init,
                                       batch=batch)
            ya = attend(q, k_all, vt_all, tq=ATTN_TQ, key_row0=0, n_keys=n_keys)
            yf = _fourier_mix(f, batch=batch)
            xl = _layer_tail(xl, yf, ya, modl, None, *tail_w, **lat_tail_rows)
            if not last:
                yac = attend(qc, k_all, vt_all, tq=n_ctx, key_row0=n_lat, n_keys=n_ctx)
                yfc = _fourier_mix(fc, batch=batch)
                xc = _layer_tail(xc, yfc, yac, modl, None, *tail_w, **ctx_rows)
        else:
            assert last, "the S5 / gMLP layer is only implemented as the final layer"
            w_in = w_in_cd[e].astype(BF16)
            bsp_lanes = jnp.broadcast_to(b_sp[e][:, :, None], (GMLP_GROUPS, CHUNK, GMLP_GC))
            s_all, gm = _inproj_odd(xl, modl, w_in, w_sp[e].astype(BF16), bsp_lanes, batch=batch,
                                    n_rows=n_lat + n_ctx, **lat_rows)
            s_all = _ln_mod_matmul_into(xc, modl, w_in[:, :S5_W], s_all, batch=batch, row0=n_lat, **ctx_rows)
            mats = _s5_matrices(s5_lam_re[e], s5_lam_im[e], s5_log_dt[e], s5_b_re[e], s5_b_im[e], s5_c_re[e],
                                s5_c_im[e], s5_d[e])
            ys = _s5_mix(s_all, mats, n_lat_rows=n_lat)
            xl = _layer_tail(xl, ys, gm, modl, (w_glu[e].astype(BF16), b_glu[e]), *tail_w, **lat_tail_rows)
    return xl.reshape(batch, n_lat, d)
```

```python
import functools
import math

import numpy as np
import jax
import jax.numpy as jnp
from jax import lax
from jax.experimental import pallas as pl
from jax.experimental.pallas import tpu as pltpu

D_MODEL = 1024
DEPTH = 2
GRID_W = 64
FOURIER_W = D_MODEL // 4
FOURIER_GC = 64
DIFF_HD = 64
DIFF_VD = 2 * DIFF_HD
DIFF_HEADS = (D_MODEL - FOURIER_W) // DIFF_VD
QK_W = DIFF_HEADS * 2 * DIFF_HD
V_W = DIFF_HEADS * DIFF_VD
EVEN_IN_W = FOURIER_W + 2 * QK_W + V_W
DIFF_SCALE = DIFF_HD ** -0.5
ROPE_BASE = 10000.0
ROPE_FREQS = DIFF_HD // 4
S5_W = D_MODEL // 2
S5_GC = 16
S5_GROUPS = S5_W // S5_GC
S5_STATE = 64
GMLP_W = D_MODEL // 2
GMLP_GC = 128
GMLP_GROUPS = GMLP_W // GMLP_GC
CHUNK = 128
ODD_IN_W = S5_W + 2 * GMLP_W
FFN_W = 4 * D_MODEL
LN_EPS = 1e-5
ALPHA = (2 * DEPTH) ** 0.25

F32 = jnp.float32
BF16 = jnp.bfloat16
HIGHEST = lax.Precision.HIGHEST

LANES = 128
MOD_ROWS = 8
ROW_TILE = 1024
ATTN_TQ = 2048
ATTN_TK = 256
ATTN_GROUP = 3
ATTN_SUM_ROWS = 16
FFT_BATCH = 4
S5_T = 16
S5_PAIRS = S5_GROUPS // 2
S5_SCAN_UNROLL = 4
FFN_CHUNK = 1024
ODD_PART_ROWS = 256
TAIL_TILE = 1024
TAIL_PART_ROWS = 256
VMEM_LIMIT = 56 * 2 ** 20
Q_SCALE = DIFF_SCALE * math.log2(math.e)


def _cparams(n_axes, vmem=VMEM_LIMIT):
    return pltpu.CompilerParams(dimension_semantics=("arbitrary",) * n_axes, vmem_limit_bytes=vmem)


def _const_spec(shape):
    zeros = (0,) * len(shape)
    return pl.BlockSpec(shape, lambda *_: zeros, pipeline_mode=pl.Buffered(1))


def _ln(x):
    xc = x - jnp.mean(x, -1, keepdims=True)
    var = jnp.mean(xc * xc, -1, keepdims=True)
    return xc * lax.rsqrt(var + LN_EPS)


def _mm(a, b):
    return jnp.dot(a.astype(BF16), b.astype(BF16), preferred_element_type=F32)


def _mod_spec(row_fn, which):
    return pl.BlockSpec((None, 1, D_MODEL), lambda i: (row_fn(i) * 6 + which, 0, 0))


def _mod_kernel(c_ref, w_ref, b_ref, o_ref):
    c = c_ref[...]
    a = c * jax.nn.sigmoid(c)
    w = w_ref[...]
    a_hi, w_hi = a.astype(BF16), w.astype(BF16)
    a_lo = (a - a_hi.astype(F32)).astype(BF16)
    w_lo = (w - w_hi.astype(F32)).astype(BF16)
    on_hi = jnp.dot(jnp.concatenate([a_hi, a_lo], axis=0), w_hi, preferred_element_type=F32)
    o_ref[...] = (on_hi[:MOD_ROWS] + on_hi[MOD_ROWS:] + jnp.dot(a_hi, w_lo, preferred_element_type=F32)
                  + b_ref[...])


def _modulation(c, c_ctx, w_mod, b_mod):
    batch = c.shape[0]
    rows = jnp.concatenate([c, c_ctx[None], jnp.zeros((MOD_ROWS - batch - 1, D_MODEL), F32)], axis=0)
    out = pl.pallas_call(
        _mod_kernel,
        grid=(DEPTH, 6),
        in_specs=[
            pl.BlockSpec((MOD_ROWS, D_MODEL), lambda l, j: (0, 0)),
            pl.BlockSpec((None, D_MODEL, D_MODEL), lambda l, j: (l, 0, j)),
            pl.BlockSpec((None, 1, D_MODEL), lambda l, j: (l, 0, j)),
        ],
        out_specs=pl.BlockSpec((None, MOD_ROWS, D_MODEL), lambda l, j: (l, 0, j)),
        out_shape=jax.ShapeDtypeStruct((DEPTH, MOD_ROWS, 6 * D_MODEL), F32),
        compiler_params=_cparams(2),
        name="modulation",
    )(rows, w_mod, b_mod.reshape(DEPTH, 1, 6 * D_MODEL))
    return out.reshape(DEPTH, MOD_ROWS * 6, 1, D_MODEL)


def _inproj_even_kernel(*refs, rope):
    if rope:
        x_ref, sh_ref, sc_ref, w_ref, cos_ref, sin_ref, f_ref, q_ref, k_ref, vt_ref = refs
    else:
        x_ref, sh_ref, sc_ref, w_ref, _, _, f_ref, q_ref, k_ref, vt_ref = refs
    width = 2 * LANES
    part = vt_ref.shape[2]
    rows = lambda r: slice(r * part, (r + 1) * part)
    norm = lambda r: (_ln(x_ref[rows(r), :]) * (1.0 + sc_ref[...]) + sh_ref[...]).astype(BF16)
    if rope:
        lane = lax.broadcasted_iota(jnp.int32, (part, width), 1)
        first_half = (lane & ROPE_FREQS) == 0
    v0 = FOURIER_W + 2 * QK_W
    n_parts = vt_ref.shape[0]
    h_next = norm(0)
    for r in range(n_parts):
        h = h_next
        f_ref[rows(r), :] = jnp.dot(h, w_ref[:, :FOURIER_W], preferred_element_type=F32)
        if r + 1 < n_parts:
            h_next = norm(r + 1)
        if rope:
            cos = jnp.concatenate([cos_ref[rows(r), :]] * (width // LANES), axis=1)
            sin = jnp.concatenate([sin_ref[rows(r), :]] * (width // LANES), axis=1)
        for j in range(QK_W // width):
            for base, o_ref, scale in ((FOURIER_W, q_ref, Q_SCALE), (FOURIER_W + QK_W, k_ref, None)):
                t = jnp.dot(h, w_ref[:, base + j * width:base + (j + 1) * width], preferred_element_type=F32)
                if rope:
                    partner = jnp.where(first_half, pltpu.roll(t, width - ROPE_FREQS, 1),
                                        pltpu.roll(t, ROPE_FREQS, 1))
                    t = t * cos + partner * sin
                if scale is not None:
                    t = t * scale
                o_ref[rows(r), j * width:(j + 1) * width] = t.astype(BF16)
        for j in range(V_W // width):
            v = jnp.dot(h, w_ref[:, v0 + j * width:v0 + (j + 1) * width], preferred_element_type=F32)
            vt_ref[r, j * width:(j + 1) * width, :] = v.T.astype(BF16)


def _inproj_even(x2d, modl, w_bf, rope_tabs, kv_all, *, batch, n_keys, key_row0, tm, row_fn):
    rows = x2d.shape[0]
    rope = rope_tabs is not None
    tiles_per_batch = rows // batch // tm
    blk0 = key_row0 // tm
    in_specs = [
        pl.BlockSpec((tm, D_MODEL), lambda i: (i, 0)),
        _mod_spec(row_fn, 0),
        _mod_spec(row_fn, 1),
        _const_spec((D_MODEL, EVEN_IN_W)),
    ]
    args = [x2d, modl, modl, w_bf]
    if rope:
        in_specs += [pl.BlockSpec((tm, LANES), lambda i: (i % tiles_per_batch, 0))] * 2
        args += list(rope_tabs)
        aliases = {}
    else:
        in_specs += [pl.BlockSpec(memory_space=pl.ANY)] * 2
        args += list(kv_all)
        aliases = {4: 2, 5: 3}
    kv_idx = lambda i: (i // tiles_per_batch, blk0 + i % tiles_per_batch, 0)
    return pl.pallas_call(
        functools.partial(_inproj_even_kernel, rope=rope),
        grid=(rows // tm,),
        in_specs=in_specs,
        out_specs=[
            pl.BlockSpec((tm, FOURIER_W), lambda i: (i, 0)),
            pl.BlockSpec((tm, QK_W), lambda i: (i, 0)),
            pl.BlockSpec((None, tm, QK_W), kv_idx),
            pl.BlockSpec((None, tm // ATTN_TK, V_W, ATTN_TK), lambda i: kv_idx(i) + (0,)),
        ],
        out_shape=[
            jax.ShapeDtypeStruct((rows, FOURIER_W), F32),
            jax.ShapeDtypeStruct((rows, QK_W), BF16),
            jax.ShapeDtypeStruct((batch, n_keys, QK_W), BF16),
            jax.ShapeDtypeStruct((batch, n_keys // ATTN_TK, V_W, ATTN_TK), BF16),
        ],
        input_output_aliases=aliases,
        compiler_params=_cparams(1),
        name="inproj_even_rope" if rope else "inproj_even",
    )(*args)


def _rope_tables(n):
    rows = n // GRID_W
    row = jnp.repeat(jnp.arange(rows, dtype=F32), GRID_W)
    col = jnp.tile(jnp.arange(GRID_W, dtype=F32), rows)
    inv = jnp.power(ROPE_BASE, -jnp.arange(ROPE_FREQS, dtype=F32) / ROPE_FREQS)
    ang = jnp.stack([row[:, None] * inv, col[:, None] * inv], axis=1)
    cos, sin = jnp.cos(ang), jnp.sin(ang)
    cos_map = jnp.concatenate([cos, cos], axis=-1).reshape(n, 4 * ROPE_FREQS)
    sin_map = jnp.concatenate([-sin, sin], axis=-1).reshape(n, 4 * ROPE_FREQS)
    reps = LANES // (4 * ROPE_FREQS)
    return jnp.tile(cos_map, (1, reps)), jnp.tile(sin_map, (1, reps))


def _attn_kernel(lq1, lk1, lq2, lk2, g_ref, q_ref, k_ref, vt_ref, o_ref, acc_ref, *scratch, lam_init):
    tq = q_ref.shape[0]
    n_chunks, _, tk = vt_ref.shape
    q = q_ref[...]
    lane = lax.broadcasted_iota(jnp.int32, q.shape, 1)
    zero = jnp.zeros_like(q)
    q_cat = jnp.concatenate([jnp.where(lane < DIFF_HD, q, zero), jnp.where(lane >= DIFF_HD, q, zero)], axis=0)
    nt_dims = (((1,), (1,)), ((), ()))

    p_ref, al_ref = scratch[-2:]
    grp = (len(scratch) - 2) // 2
    s_refs, mc_refs = scratch[:grp], scratch[grp:2 * grp]
    n_groups = n_chunks // grp

    def scores(g, r):
        kb = k_ref[pl.ds(pl.multiple_of((g * grp + r) * tk, tk), tk), :]
        s = lax.dot_general(kb, q_cat, nt_dims, preferred_element_type=F32)
        s_refs[r][...] = s
        mc_refs[r][...] = jnp.max(s, axis=0, keepdims=True)

    def group_max(m_prev):
        m = m_prev
        for r in range(grp):
            m = jnp.maximum(m, mc_refs[r][...])
        al_ref[...] = jnp.exp2(m_prev - m)
        return m

    def probs(r, m):
        p_ref[r * tk:(r + 1) * tk, :] = jnp.exp2(s_refs[r][...] - m).astype(BF16)

    ones_rows = (lax.broadcasted_iota(jnp.int32, (ATTN_SUM_ROWS, grp * tk), 0) == 0).astype(BF16)

    def values(g):
        vt = jnp.concatenate([vt_ref[g * grp + r] for r in range(grp)], axis=1)
        vt1 = jnp.concatenate([vt, ones_rows], axis=0)
        acc_ref[...] = acc_ref[...] * al_ref[...] + jnp.dot(vt1, p_ref[...], preferred_element_type=F32)

    acc_ref[...] = jnp.zeros_like(acc_ref)
    m = jnp.full((1, 2 * tq), -jnp.inf, F32)
    for r in range(grp):
        scores(0, r)
    m = group_max(m)
    for r in range(grp):
        probs(r, m)
        if n_groups > 1:
            scores(1, r)

    def body(g, m):
        values(g - 1)
        m = group_max(m)
        for r in range(grp):
            probs(r, m)
            scores(g + 1, r)
        return m

    if n_groups > 2:
        m = lax.fori_loop(1, n_groups - 1, body, m)
    if n_groups > 1:
        values(n_groups - 2)
        m = group_max(m)
        for r in range(grp):
            probs(r, m)
    values(n_groups - 1)

    lam = (jnp.exp(jnp.sum(lq1[...] * lk1[...], keepdims=True))
           - jnp.exp(jnp.sum(lq2[...] * lk2[...], keepdims=True)) + lam_init)
    on = acc_ref[:DIFF_VD, :] / acc_ref[DIFF_VD:DIFF_VD + 1, :]
    o = on[:, :tq] - lam * on[:, tq:]
    o = o * lax.rsqrt(jnp.mean(o * o, axis=0, keepdims=True) + LN_EPS)
    o_ref[...] = o.T * (g_ref[...] * (1.0 - lam_init))


def _diff_attention(q, k_all, vt_all, lam_vecs, subln_g, lam_init, *, batch, tq, key_row0, n_keys):
    rows = q.shape[0]
    nq = rows // batch // tq
    tk = vt_all.shape[3]
    n_chunks = n_keys // tk
    assert key_row0 % n_keys == 0
    kblk = key_row0 // n_keys
    grp = ATTN_GROUP if n_chunks % ATTN_GROUP == 0 else 1
    cols = 2 * tq
    slot_shapes = [((tk, cols), F32), ((1, cols), F32)]
    vec_spec = pl.BlockSpec((1, DIFF_HD), lambda b, h, i: (0, 0))
    in_specs = [vec_spec] * 4 + [
        pl.BlockSpec((1, DIFF_VD), lambda b, h, i: (0, 0)),
        pl.BlockSpec((tq, DIFF_VD), lambda b, h, i: (b * nq + i, h)),
        pl.BlockSpec((None, n_keys, DIFF_VD), lambda b, h, i: (b, kblk, h)),
        pl.BlockSpec((None, n_chunks, DIFF_VD, tk), lambda b, h, i: (b, kblk, h, 0)),
    ]
    args = [v.reshape(1, DIFF_HD) for v in lam_vecs] + [subln_g.reshape(1, DIFF_VD), q, k_all, vt_all]
    return pl.pallas_call(
        functools.partial(_attn_kernel, lam_init=lam_init),
        grid=(batch, DIFF_HEADS, nq),
        in_specs=in_specs,
        out_specs=pl.BlockSpec((tq, DIFF_VD), lambda b, h, i: (b * nq + i, h)),
        out_shape=jax.ShapeDtypeStruct((rows, V_W), F32),
        scratch_shapes=[pltpu.VMEM((DIFF_VD + ATTN_SUM_ROWS, cols), F32)]
        + [pltpu.VMEM(shape, dtype) for shape, dtype in slot_shapes for _ in range(grp)]
        + [pltpu.VMEM((grp * tk, cols), BF16), pltpu.VMEM((1, cols), F32)],
        compiler_params=_cparams(3),
        name="diff_attention" if n_chunks > 1 else "diff_attention_ctx",
    )(*args)


def _dft_cos_sin(n):
    idx = np.outer(np.arange(n), np.arange(n)) % n
    ang = 2.0 * np.pi * idx / n
    return np.cos(ang), np.sin(ang)


def _channel_dft():
    c, s = _dft_cos_sin(FOURIER_GC)
    eye = np.eye(LANES // FOURIER_GC)
    return jnp.asarray(np.kron(eye, c), F32).astype(BF16), jnp.asarray(np.kron(eye, s), F32).astype(BF16)


def _fourier_dense_kernel(x_ref, cl_ref, sl_ref, bdc_ref, bds_ref, o_ref, *, norm):
    x = x_ref[...]
    pr = _mm(cl_ref[...], x)
    pi = -_mm(sl_ref[...], x)
    out = _mm(pr, bdc_ref[...]) + _mm(pi, bds_ref[...])
    o_ref[...] = out * norm


def _fourier_fft_kernel(x_ref, f1_ref, twc_ref, tws_ref, f3_ref, bdc_ref, bds_ref, o_ref, ur_ref, ui_ref,
                        *, n1, norm):
    n2 = LANES

    nb = FFT_BATCH
    lanes = lambda j: slice(j * LANES, (j + 1) * LANES)

    def stage1(ib, carry):
        i0 = ib * nb
        xs = jnp.concatenate([x_ref[pl.ds(i0 + j, n2, stride=n1), :] for j in range(nb)], axis=1)
        t = _mm(f1_ref[...], xs)
        for j in range(nb):
            tr, ti = t[:n2, lanes(j)], t[n2:, lanes(j)]
            r0 = pl.multiple_of((i0 + j) * n2, n2)
            c = twc_ref[pl.ds(r0, n2), :]
            s = tws_ref[pl.ds(r0, n2), :]
            ur_ref[pl.ds(r0, n2), :] = tr * c + ti * s
            ui_ref[pl.ds(r0, n2), :] = ti * c - tr * s
        return carry

    lax.fori_loop(0, n1 // nb, stage1, 0)

    def stage2(kb, carry):
        k0 = kb * nb
        u = jnp.concatenate(
            [jnp.concatenate([ur_ref[pl.ds(k0 + j, n1, stride=n2), :], ui_ref[pl.ds(k0 + j, n1, stride=n2), :]],
                             axis=0) for j in range(nb)], axis=1)
        p = _mm(f3_ref[...], u)
        for j in range(nb):
            ur_ref[pl.ds(k0 + j, n1, stride=n2), :] = p[:n1, lanes(j)]
            ui_ref[pl.ds(k0 + j, n1, stride=n2), :] = p[n1:, lanes(j)]
        return carry

    lax.fori_loop(0, n2 // nb, stage2, 0)

    rows = ROW_TILE

    def stage3(r, carry):
        r0 = pl.multiple_of(r * rows, rows)
        out = _mm(ur_ref[pl.ds(r0, rows), :], bdc_ref[...]) + _mm(ui_ref[pl.ds(r0, rows), :], bds_ref[...])
        o_ref[pl.ds(r0, rows), :] = out * norm
        return carry

    lax.fori_loop(0, (n1 * n2) // rows, stage3, 0)


def _fourier_mix(f2d, *, batch):
    rows = f2d.shape[0]
    n = rows // batch
    norm = 1.0 / math.sqrt(n * FOURIER_GC)
    bdc, bds = _channel_dft()
    blk = pl.BlockSpec((n, LANES), lambda b, j: (b, j))
    grid = (batch, FOURIER_W // LANES)
    out_shape = jax.ShapeDtypeStruct((rows, FOURIER_W), F32)
    mat = _const_spec((LANES, LANES))
    if n <= ROW_TILE:
        c, s = _dft_cos_sin(n)
        return pl.pallas_call(
            functools.partial(_fourier_dense_kernel, norm=norm),
            grid=grid,
            in_specs=[blk, _const_spec((n, n)), _const_spec((n, n)), mat, mat],
            out_specs=blk,
            out_shape=out_shape,
            compiler_params=_cparams(2),
            name="fourier_dense",
        )(f2d, jnp.asarray(c, F32).astype(BF16), jnp.asarray(s, F32).astype(BF16), bdc, bds)
    n2 = LANES
    n1 = n // n2
    c2, s2 = _dft_cos_sin(n2)
    f1 = jnp.asarray(np.concatenate([c2, -s2], axis=0), F32).astype(BF16)
    c1, s1 = _dft_cos_sin(n1)
    f3 = jnp.asarray(np.block([[c1, s1], [-s1, c1]]), F32).astype(BF16)
    tw_idx = np.outer(np.arange(n1), np.arange(n2)).reshape(-1)
    tw_ang = 2.0 * np.pi * tw_idx / n
    twc = jnp.broadcast_to(jnp.asarray(np.cos(tw_ang), F32)[:, None], (n, LANES))
    tws = jnp.broadcast_to(jnp.asarray(np.sin(tw_ang), F32)[:, None], (n, LANES))
    return pl.pallas_call(
        functools.partial(_fourier_fft_kernel, n1=n1, norm=norm),
        grid=grid,
        in_specs=[blk, _const_spec((2 * n2, n2)), _const_spec((n, LANES)), _const_spec((n, LANES)),
                  _const_spec((2 * n1, 2 * n1)), mat, mat],
        out_specs=blk,
        out_shape=out_shape,
        scratch_shapes=[pltpu.VMEM((n, LANES), F32), pltpu.VMEM((n, LANES), F32)],
        compiler_params=_cparams(2),
        name="fourier_fft",
    )(f2d, f1, twc, tws, f3, bdc, bds)


def _gelu_tanh(x):
    return 0.5 * x * (1.0 + jnp.tanh(math.sqrt(2.0 / math.pi) * (x + 0.044715 * (x * x * x))))


def _tail_kernel(*refs, glu, wa):
    if glu:
        x_ref, ya_ref, yb_ref, wg_ref, bg_ref, *rest = refs
    else:
        x_ref, ya_ref, yb_ref, *rest = refs
    (wo_ref, bo_ref, g1_ref, lg1_ref, lb1_ref, sh2_ref, sc2_ref, g2_ref,
     w1_ref, b1_ref, w2_ref, b2_ref, lg2_ref, lb2_ref, o_ref) = rest
    part = min(TAIL_PART_ROWS, x_ref.shape[0])
    n_parts = x_ref.shape[0] // part
    rows = lambda r: slice(r * part, (r + 1) * part)

    def mix_out(r):
        ya = ya_ref[rows(r), :]
        if glu:
            g = _gelu_tanh(ya)
            ya = g * jax.nn.sigmoid(jnp.dot(g.astype(BF16), wg_ref[...], preferred_element_type=F32) + bg_ref[...])
        y = (jnp.dot(ya.astype(BF16), wo_ref[:wa, :], preferred_element_type=F32)
             + jnp.dot(yb_ref[rows(r), :].astype(BF16), wo_ref[wa:, :], preferred_element_type=F32) + bo_ref[...])
        x1 = _ln(ALPHA * x_ref[rows(r), :] + g1_ref[...] * y) * lg1_ref[...] + lb1_ref[...]
        return x1, (_ln(x1) * (1.0 + sc2_ref[...]) + sh2_ref[...]).astype(BF16)

    def ffn_chunk(h, c):
        cols = slice(c * FFN_CHUNK, (c + 1) * FFN_CHUNK)
        a = jnp.maximum(jnp.dot(h, w1_ref[:, cols], preferred_element_type=F32) + b1_ref[:, cols], 0.0)
        return jnp.dot((a * a).astype(BF16), w2_ref[cols, :], preferred_element_type=F32)

    def finish(r, x1, y2):
        o_ref[rows(r), :] = _ln(ALPHA * x1 + g2_ref[...] * (y2 + b2_ref[...])) * lg2_ref[...] + lb2_ref[...]

    state = mix_out(0)
    done = None
    for r in range(n_parts):
        x1, h = state
        y2 = ffn_chunk(h, 0)
        if r + 1 < n_parts:
            state = mix_out(r + 1)
        if done is not None:
            finish(*done)
        for c in range(1, FFN_W // FFN_CHUNK):
            y2 = y2 + ffn_chunk(h, c)
        done = (r, x1, y2)
    finish(*done)


def _layer_tail(x2d, ya, yb, modl, glu_params, w_out, b_out, lg1, lb1, w1, b1, w2, b2, lg2, lb2, *, layer, tm,
                row_fn):
    rows = x2d.shape[0]
    wa, wb = ya.shape[-1], yb.shape[1]
    glu = glu_params is not None
    row = lambda v: v.reshape(1, -1)
    vec = _const_spec((1, D_MODEL))
    if ya.ndim == 3:
        tiles_per_batch = rows // ya.shape[0] // tm
        ya_spec = pl.BlockSpec((None, tm, wa), lambda i: (i // tiles_per_batch, i % tiles_per_batch, 0))
    else:
        ya_spec = pl.BlockSpec((tm, wa), lambda i: (i, 0))
    in_specs = [pl.BlockSpec((tm, D_MODEL), lambda i: (i, 0)), ya_spec,
                pl.BlockSpec((tm, wb), lambda i: (i, 0))]
    args = [x2d, ya, yb]
    if glu:
        in_specs += [_const_spec((wa, wa)), _const_spec((1, wa))]
        args += [glu_params[0], row(glu_params[1])]
    slab = lambda shape: pl.BlockSpec((None,) + shape, lambda i: (layer, 0, 0), pipeline_mode=pl.Buffered(1))
    in_specs += [slab((D_MODEL, D_MODEL)), vec, _mod_spec(row_fn, 2), vec, vec,
                 _mod_spec(row_fn, 3), _mod_spec(row_fn, 4), _mod_spec(row_fn, 5),
                 slab((D_MODEL, FFN_W)), _const_spec((1, FFN_W)), slab((FFN_W, D_MODEL)), vec, vec, vec]
    args += [w_out, row(b_out), modl, row(lg1), row(lb1), modl, modl, modl,
             w1, row(b1), w2, row(b2), row(lg2), row(lb2)]
    return pl.pallas_call(
        functools.partial(_tail_kernel, glu=glu, wa=wa),
        grid=(rows // tm,),
        in_specs=in_specs,
        out_specs=pl.BlockSpec((tm, D_MODEL), lambda i: (i, 0)),
        out_shape=jax.ShapeDtypeStruct((rows, D_MODEL), F32),
        compiler_params=_cparams(1),
        name="layer_tail_glu" if glu else "layer_tail",
    )(*args)


def _inproj_odd_kernel(x_ref, sh_ref, sc_ref, w_ref, wsp_ref, bsp_ref, s_ref, gm_ref):
    part = min(ODD_PART_ROWS, x_ref.shape[0])
    n_parts = x_ref.shape[0] // part
    n_ch = part // CHUNK
    rows = lambda r: slice(r * part, (r + 1) * part)
    norm = lambda r: (_ln(x_ref[rows(r), :]) * (1.0 + sc_ref[...]) + sh_ref[...]).astype(BF16)

    def gate(r, u, v):
        chunk = lambda ch: slice(ch * CHUNK, (ch + 1) * CHUNK)
        for g in range(GMLP_GROUPS):
            cols = slice(g * GMLP_GC, (g + 1) * GMLP_GC)
            vn = jnp.concatenate([_ln(v[chunk(ch), cols]).astype(BF16) for ch in range(n_ch)], axis=1)
            sp = jnp.dot(wsp_ref[g], vn, preferred_element_type=F32)
            for ch in range(n_ch):
                out_rows = slice(r * part + ch * CHUNK, r * part + (ch + 1) * CHUNK)
                gm_ref[out_rows, cols] = u[chunk(ch), cols] * (sp[:, ch * GMLP_GC:(ch + 1) * GMLP_GC] + bsp_ref[g])

    h_next = norm(0)
    pending = None
    for r in range(n_parts):
        h = h_next
        s_ref[rows(r), :] = jnp.dot(h, w_ref[:, :S5_W], preferred_element_type=F32)
        if pending is not None:
            gate(*pending)
        u = jnp.dot(h, w_ref[:, S5_W:S5_W + GMLP_W], preferred_element_type=F32)
        if r + 1 < n_parts:
            h_next = norm(r + 1)
        v = jnp.dot(h, w_ref[:, S5_W + GMLP_W:], preferred_element_type=F32)
        pending = (r, u, v)
    gate(*pending)


def _inproj_odd(x2d, modl, w_bf, wsp_bf, bsp_lanes, *, batch, n_rows, tm, row_fn):
    rows = x2d.shape[0]
    tiles_per_batch = rows // batch // tm
    return pl.pallas_call(
        _inproj_odd_kernel,
        grid=(rows // tm,),
        in_specs=[pl.BlockSpec((tm, D_MODEL), lambda i: (i, 0)), _mod_spec(row_fn, 0), _mod_spec(row_fn, 1),
                  _const_spec((D_MODEL, ODD_IN_W)), _const_spec((GMLP_GROUPS, CHUNK, CHUNK)),
                  _const_spec((GMLP_GROUPS, CHUNK, GMLP_GC))],
        out_specs=[pl.BlockSpec((None, tm, S5_W), lambda i: (i // tiles_per_batch, i % tiles_per_batch, 0)),
                   pl.BlockSpec((tm, GMLP_W), lambda i: (i, 0))],
        out_shape=[jax.ShapeDtypeStruct((batch, n_rows, S5_W), F32), jax.ShapeDtypeStruct((rows, GMLP_W), F32)],
        compiler_params=_cparams(1),
        name="inproj_odd_gmlp",
    )(x2d, modl, modl, w_bf, wsp_bf, bsp_lanes)


def _ln_mod_matmul_kernel(x_ref, sh_ref, sc_ref, w_ref, _, o_ref):
    h = (_ln(x_ref[...]) * (1.0 + sc_ref[...]) + sh_ref[...]).astype(BF16)
    o_ref[...] = jnp.dot(h, w_ref[...], preferred_element_type=F32)


def _ln_mod_matmul_into(x2d, modl, w_bf, dst, *, batch, row0, tm, row_fn):
    rows, n = x2d.shape[0], w_bf.shape[1]
    tiles_per_batch = rows // batch // tm
    blk0 = row0 // tm
    return pl.pallas_call(
        _ln_mod_matmul_kernel,
        grid=(rows // tm,),
        in_specs=[pl.BlockSpec((tm, D_MODEL), lambda i: (i, 0)), _mod_spec(row_fn, 0), _mod_spec(row_fn, 1),
                  _const_spec((D_MODEL, n)), pl.BlockSpec(memory_space=pl.ANY)],
        out_specs=pl.BlockSpec((None, tm, n), lambda i: (i // tiles_per_batch, blk0 + i % tiles_per_batch, 0)),
        out_shape=jax.ShapeDtypeStruct(dst.shape, dst.dtype),
        input_output_aliases={4: 0},
        compiler_params=_cparams(1),
        name="ln_mod_matmul",
    )(x2d, modl, modl, w_bf, dst)


def _s5_prep_kernel(pw_ref, bb_ref, cc_ref, dw_ref, w_ref, vt_ref, m_ref):
    t_len, gc = S5_T, S5_GC
    width = t_len * gc
    nt_dims = (((1,), (1,)), ((), ()))
    cols = lambda q: slice(q * LANES, (q + 1) * LANES)
    lane = lax.broadcasted_iota(jnp.int32, (gc, width), 1)
    for g in range(2):
        wide = []
        for d in range(2):
            pr, pi = pw_ref[g, d, 0], pw_ref[g, d, 1]
            bbr, bbi = bb_ref[g, d, 0], bb_ref[g, d, 1]
            cr, ci = cc_ref[g, d, 0], cc_ref[g, d, 1]

            def power(xr, xi, t, pr=pr, pi=pi):
                return pr[t:t + 1] * xr - pi[t:t + 1] * xi, pr[t:t + 1] * xi + pi[t:t + 1] * xr

            lc = [power(cr, ci, t) for t in range(t_len + 1)]
            for s in range(t_len):
                blk = slice(g * width + s * gc, g * width + (s + 1) * gc)
                lbr, lbi = power(bbr, bbi, t_len - 1 - s if d == 0 else s)
                w_ref[blk, cols(2 * d)] = lbr.astype(BF16)
                w_ref[blk, cols(2 * d + 1)] = lbi.astype(BF16)
                lcr, lci = lc[s + 1 if d == 0 else t_len - s]
                vt_ref[blk, cols(2 * d)] = lcr.astype(BF16)
                vt_ref[blk, cols(2 * d + 1)] = (-lci).astype(BF16)
            order = range(t_len) if d == 0 else range(t_len - 1, -1, -1)
            lcr_all = jnp.concatenate([lc[t][0] for t in order], axis=0)
            lci_all = jnp.concatenate([lc[t][1] for t in order], axis=0)
            wide.append(lax.dot_general(bbr, lcr_all, nt_dims, preferred_element_type=F32, precision=HIGHEST)
                        - lax.dot_general(bbi, lci_all, nt_dims, preferred_element_type=F32, precision=HIGHEST))
        kf = wide[0] + dw_ref[g]
        kb = wide[1]
        for s in range(t_len):
            fwd = jnp.where(lane >= s * gc, pltpu.roll(kf, s * gc, 1), 0.0)
            bwd = jnp.where(lane < (s + 1) * gc, pltpu.roll(kb, (s + 1) * gc % width, 1), 0.0)
            m_ref[g, s * gc:(s + 1) * gc, :] = (fwd + bwd).astype(BF16)


def _s5_matrices(lam_re, lam_im, log_dt, b_re, b_im, c_re, c_im, d_skip):
    t_len, g_n, p_n, i_n = S5_T, S5_GROUPS, S5_STATE, S5_GC
    tau_rows = -(-(t_len + 1) // 8) * 8
    dt = jnp.exp(log_dt)[..., None]
    mag, th = lam_re * dt, lam_im * dt
    er = jnp.exp(mag)
    lbr, lbi = er * jnp.cos(th), er * jnp.sin(th)
    den = lam_re * lam_re + lam_im * lam_im
    nr, ni = lbr - 1.0, lbi
    cr = (nr * lam_re + ni * lam_im) / den
    ci = (ni * lam_re - nr * lam_im) / den
    tau = jnp.arange(tau_rows, dtype=F32)[:, None]
    pm = jnp.exp(tau * mag[:, :, None, :])
    ang = tau * th[:, :, None, :]
    pw = jnp.stack([pm * jnp.cos(ang), pm * jnp.sin(ang)], axis=2)
    to_ip = lambda a: jnp.swapaxes(a, -1, -2)
    bb = jnp.stack([cr[..., None, :] * to_ip(b_re) - ci[..., None, :] * to_ip(b_im),
                    cr[..., None, :] * to_ip(b_im) + ci[..., None, :] * to_ip(b_re)], axis=2)
    cc = jnp.stack([c_re, c_im], axis=2)

    def per_group(a):
        a = jnp.swapaxes(a, 0, 1)
        a = a.reshape((g_n // 2, 2) + a.shape[1:])
        zero = jnp.zeros_like(a[:, 0])
        both = jnp.stack([jnp.concatenate([a[:, 0], zero], axis=-1), jnp.concatenate([zero, a[:, 1]], axis=-1)], axis=1)
        return both.reshape((g_n,) + both.shape[2:])

    width = t_len * i_n
    d_wide = jnp.pad(d_skip.reshape(g_n, i_n)[:, None, :] * jnp.asarray(np.eye(i_n), F32),
                     ((0, 0), (0, 0), (0, width - i_n)))
    pair = lambda shape: pl.BlockSpec((2,) + shape, lambda j: (j,) + (0,) * len(shape))
    state_spec = pl.BlockSpec((None, 2 * width, 4 * LANES), lambda j: (j, 0, 0))
    state_shape = jax.ShapeDtypeStruct((S5_PAIRS, 2 * width, 4 * LANES), BF16)
    w_state, vt_state, m_pair = pl.pallas_call(
        _s5_prep_kernel,
        grid=(S5_PAIRS,),
        in_specs=[pair((2, 2, tau_rows, 2 * p_n)), pair((2, 2, i_n, 2 * p_n)), pair((2, 2, i_n, 2 * p_n)),
                  pair((i_n, width))],
        out_specs=[state_spec, state_spec, pl.BlockSpec((None, 2, width, width), lambda j: (j, 0, 0, 0))],
        out_shape=[state_shape, state_shape, jax.ShapeDtypeStruct((S5_PAIRS, 2, width, width), BF16)],
        compiler_params=_cparams(1),
        name="s5_prep",
    )(per_group(pw), per_group(bb), per_group(cc), d_wide)
    a_pow = jnp.stack([pw[0, :, 0, t_len], pw[0, :, 1, t_len], pw[1, :, 0, t_len], pw[1, :, 1, t_len]], axis=0)
    return m_pair, w_state, vt_state, a_pow.reshape(4, g_n * p_n)


def _s5_fold_perm():
    n_t, n_g = LANES // S5_GC, LANES // S5_GC
    src = np.arange(n_t * LANES)
    t_lo, g, i = src // LANES, (src % LANES) // S5_GC, src % S5_GC
    perm = np.zeros((n_t * LANES, n_g * LANES), np.float32)
    perm[src, g * LANES + t_lo * S5_GC + i] = 1.0
    return perm


def _s5_kernel(s_ref, pb_ref, pbt_ref, m_ref, w_ref, v_ref, a_ref, y_ref, u_scr, st_scr, yp_scr, *, n_lat, n_ctx):
    n_all = n_lat + n_ctx
    n_t = LANES // S5_GC
    width = S5_T * S5_GC
    blk = lambda j: slice(j * LANES, (j + 1) * LANES)
    for t_hi in range(S5_T // n_t):
        x = jnp.concatenate([s_ref[pl.ds(n_t * t_hi + t_lo, n_all, stride=S5_T), :].astype(BF16)
                             for t_lo in range(n_t)], axis=1)
        xp = jnp.dot(x, pb_ref[...], preferred_element_type=F32).astype(BF16)
        for g in range(n_t):
            u_scr[g, :, blk(t_hi)] = xp[:, blk(g)]

    n_pairs = w_ref.shape[0]
    for pr in range(n_pairs):
        u_pair = jnp.concatenate([u_scr[2 * pr], u_scr[2 * pr + 1]], axis=1)
        st = jnp.dot(u_pair, w_ref[pr], preferred_element_type=F32)
        for q in range(4):
            st_scr[q, :, blk(pr)] = st[:, blk(q)]

    afr, afi, abr, abi = (a_ref[pl.ds(q, 1), :] for q in range(4))

    def scan_step(k, carry):
        hr, hi, gr, gi = carry
        rf = pl.ds(jnp.where(k < n_ctx, k + n_lat, k - n_ctx), 1)
        rb = pl.ds(n_all - 1 - k, 1)
        sfr, sfi, sbr, sbi = st_scr[0, rf, :], st_scr[1, rf, :], st_scr[2, rb, :], st_scr[3, rb, :]
        st_scr[0, rf, :] = hr
        st_scr[1, rf, :] = hi
        st_scr[2, rb, :] = gr
        st_scr[3, rb, :] = gi
        return (afr * hr - afi * hi + sfr, afr * hi + afi * hr + sfi,
                abr * gr - abi * gi + sbr, abr * gi + abi * gr + sbi)

    lax.fori_loop(0, n_all, scan_step, (jnp.zeros_like(afr),) * 4, unroll=S5_SCAN_UNROLL)

    for pr in range(n_pairs):
        hst = jnp.concatenate([st_scr[q, :, blk(pr)].astype(BF16) for q in range(4)], axis=1)
        y_pair = lax.dot_general(hst, v_ref[pr], (((1,), (1,)), ((), ())), preferred_element_type=F32)
        for g in range(2):
            y_g = (y_pair[:, g * width:(g + 1) * width]
                   + jnp.dot(u_scr[2 * pr + g], m_ref[pr, g], preferred_element_type=F32))
            for t_hi in range(S5_T // n_t):
                yp_scr[t_hi, :, blk(2 * pr + g)] = y_g[:, blk(t_hi)].astype(BF16)

    for t_hi in range(S5_T // n_t):
        z = jnp.dot(yp_scr[t_hi], pbt_ref[...], preferred_element_type=F32)
        for t_lo in range(n_t):
            y_ref[pl.ds(n_t * t_hi + t_lo, n_all, stride=S5_T), :] = z[:, blk(t_lo)]


def _s5_mix(s_all, mats, *, n_lat_rows):
    m_pair, w_state, v_state, a_pow = mats
    batch, n_rows, _ = s_all.shape
    n_lat = n_lat_rows // S5_T
    n_all = n_rows // S5_T
    width = S5_T * S5_GC
    n_blocks = S5_W // LANES
    pairs = S5_PAIRS // n_blocks
    perm = _s5_fold_perm()
    rows_spec = pl.BlockSpec((None, n_rows, LANES), lambda b, v: (b, 0, v))
    return pl.pallas_call(
        functools.partial(_s5_kernel, n_lat=n_lat, n_ctx=n_all - n_lat),
        grid=(batch, n_blocks),
        in_specs=[rows_spec, _const_spec(perm.shape), _const_spec(perm.shape),
                  pl.BlockSpec((pairs, 2, width, width), lambda b, v: (v, 0, 0, 0)),
                  pl.BlockSpec((pairs, 2 * width, 4 * LANES), lambda b, v: (v, 0, 0)),
                  pl.BlockSpec((pairs, 4 * LANES, 2 * width), lambda b, v: (v, 0, 0)),
                  pl.BlockSpec((4, pairs * LANES), lambda b, v: (0, v))],
        out_specs=rows_spec,
        out_shape=jax.ShapeDtypeStruct(s_all.shape, F32),
        scratch_shapes=[pltpu.VMEM((2 * pairs, n_all, width), BF16), pltpu.VMEM((4, n_all, pairs * LANES), F32),
                        pltpu.VMEM((S5_T * S5_GC // LANES, n_all, 2 * pairs * LANES), BF16)],
        compiler_params=_cparams(2),
        name="s5_mix",
    )(s_all, jnp.asarray(perm, BF16), jnp.asarray(perm.T, BF16), m_pair, w_state, v_state, a_pow)


def kernel(x, c, ctx, c_ctx, w_mod, b_mod, w_out, b_out, ln_mix_g, ln_mix_b, w_ffn1, b_ffn1, w_ffn2,
           b_ffn2, ln_ffn_g, ln_ffn_b, w_in_ab, lam_q1, lam_k1, lam_q2, lam_k2, subln_g, w_in_cd,
           s5_lam_re, s5_lam_im, s5_log_dt, s5_b_re, s5_b_im, s5_c_re, s5_c_im, s5_d, w_glu, b_glu,
           w_sp, b_sp):
    batch, n_lat, d = x.shape
    n_ctx = ctx.shape[1]
    assert d == D_MODEL and n_lat % TAIL_TILE == 0 and n_ctx % CHUNK == 0 and batch + 1 <= MOD_ROWS
    mod = _modulation(c, c_ctx, w_mod, b_mod)
    xl = x.reshape(batch * n_lat, d)
    xc = ctx.reshape(batch * n_ctx, d)
    tiles = n_lat // ROW_TILE
    lat_rows = dict(tm=ROW_TILE, row_fn=lambda i: i // tiles)
    tail_tiles = n_lat // TAIL_TILE
    lat_tail_rows = dict(tm=TAIL_TILE, row_fn=lambda i: i // tail_tiles)
    ctx_rows = dict(tm=n_ctx, row_fn=lambda i: batch)
    rope_tabs = _rope_tables(n_lat)
    w_out_bf, w_ffn1_bf, w_ffn2_bf = w_out.astype(BF16), w_ffn1.astype(BF16), w_ffn2.astype(BF16)
    for l in range(DEPTH):
        last = l == DEPTH - 1
        e = l // 2
        modl = mod[l]
        tail_w = (w_out_bf, b_out[l], ln_mix_g[l], ln_mix_b[l], w_ffn1_bf, b_ffn1[l],
                  w_ffn2_bf, b_ffn2[l], ln_ffn_g[l], ln_ffn_b[l])
        if l % 2 == 0:
            w_in = w_in_ab[e].astype(BF16)
            lam_init = 0.8 - 0.6 * math.exp(-0.3 * l)
            lam_vecs = (lam_q1[e], lam_k1[e], lam_q2[e], lam_k2[e])
            n_keys = n_lat + n_ctx
            f, q, k_all, vt_all = _inproj_even(xl, modl, w_in, rope_tabs, None, batch=batch, n_keys=n_keys,
                                               key_row0=0, **lat_rows)
            fc, qc, k_all, vt_all = _inproj_even(xc, modl, w_in, None, (k_all, vt_all), batch=batch, n_keys=n_keys,
                                                 key_row0=n_lat, **ctx_rows)
            attend = functools.partial(_diff_attention, lam_vecs=lam_vecs, subln_g=subln_g[e], lam_init=lam_init,
                                       batch=batch)
            ya = attend(q, k_all, vt_all, tq=min(ATTN_TQ, n_lat), key_row0=0, n_keys=n_keys)
            yf = _fourier_mix(f, batch=batch)
            xl = _layer_tail(xl, yf, ya, modl, None, *tail_w, layer=l, **lat_tail_rows)
            if not last:
                yac = attend(qc, k_all, vt_all, tq=n_ctx, key_row0=n_lat, n_keys=n_ctx)
                yfc = _fourier_mix(fc, batch=batch)
                xc = _layer_tail(xc, yfc, yac, modl, None, *tail_w, layer=l, **ctx_rows)
        else:
            assert last, "the S5 / gMLP layer is only implemented as the final layer"
            w_in = w_in_cd[e].astype(BF16)
            bsp_lanes = jnp.broadcast_to(b_sp[e][:, :, None], (GMLP_GROUPS, CHUNK, GMLP_GC))
            s_all, gm = _inproj_odd(xl, modl, w_in, w_sp[e].astype(BF16), bsp_lanes, batch=batch,
                                    n_rows=n_lat + n_ctx, **lat_rows)
            s_all = _ln_mod_matmul_into(xc, modl, w_in[:, :S5_W], s_all, batch=batch, row0=n_lat, **ctx_rows)
            mats = _s5_matrices(s5_lam_re[e], s5_lam_im[e], s5_log_dt[e], s5_b_re[e], s5_b_im[e], s5_c_re[e],
                                s5_c_im[e], s5_d[e])
            ys = _s5_mix(s_all, mats, n_lat_rows=n_lat)
            xl = _layer_tail(xl, ys, gm, modl, (w_glu[e].astype(BF16), b_glu[e]), *tail_w, layer=l, **lat_tail_rows)
    return xl.reshape(batch, n_lat, d)
```

```python
import functools
import math

import numpy as np
import jax
import jax.numpy as jnp
from jax import lax
from jax.experimental import pallas as pl
from jax.experimental.pallas import tpu as pltpu

D_MODEL = 1024
DEPTH = 2
GRID_W = 64
FOURIER_W = D_MODEL // 4
FOURIER_GC = 64
DIFF_HD = 64
DIFF_VD = 2 * DIFF_HD
DIFF_HEADS = (D_MODEL - FOURIER_W) // DIFF_VD
QK_W = DIFF_HEADS * 2 * DIFF_HD
V_W = DIFF_HEADS * DIFF_VD
EVEN_IN_W = FOURIER_W + 2 * QK_W + V_W
DIFF_SCALE = DIFF_HD ** -0.5
ROPE_BASE = 10000.0
ROPE_FREQS = DIFF_HD // 4
S5_W = D_MODEL // 2
S5_GC = 16
S5_GROUPS = S5_W // S5_GC
S5_STATE = 64
GMLP_W = D_MODEL // 2
GMLP_GC = 128
GMLP_GROUPS = GMLP_W // GMLP_GC
CHUNK = 128
ODD_IN_W = S5_W + 2 * GMLP_W
FFN_W = 4 * D_MODEL
LN_EPS = 1e-5
ALPHA = (2 * DEPTH) ** 0.25

F32 = jnp.float32
BF16 = jnp.bfloat16
HIGHEST = lax.Precision.HIGHEST

LANES = 128
MOD_ROWS = 8
ROW_TILE = 1024
ATTN_TQ = 2048
ATTN_TK = 256
ATTN_GROUP = 3
ATTN_SUM_ROWS = 16
FFT_BATCH = 8
S5_T = 16
S5_PAIRS = S5_GROUPS // 2
S5_SCAN_UNROLL = 4
FFN_CHUNK = 1024
ODD_PART_ROWS = 256
TAIL_TILE = 1024
TAIL_PART_ROWS = 256
VMEM_LIMIT = 56 * 2 ** 20
Q_SCALE = DIFF_SCALE * math.log2(math.e)


def _cparams(n_axes, vmem=VMEM_LIMIT):
    return pltpu.CompilerParams(dimension_semantics=("arbitrary",) * n_axes, vmem_limit_bytes=vmem)


def _const_spec(shape):
    zeros = (0,) * len(shape)
    return pl.BlockSpec(shape, lambda *_: zeros, pipeline_mode=pl.Buffered(1))


def _ln(x):
    xc = x - jnp.mean(x, -1, keepdims=True)
    var = jnp.mean(xc * xc, -1, keepdims=True)
    return xc * lax.rsqrt(var + LN_EPS)


def _mm(a, b):
    return jnp.dot(a.astype(BF16), b.astype(BF16), preferred_element_type=F32)


def _mod_spec(row_fn, which):
    return pl.BlockSpec((None, 1, D_MODEL), lambda i: (row_fn(i) * 6 + which, 0, 0))


def _mod_kernel(c_ref, w_ref, b_ref, o_ref):
    c = c_ref[...]
    a = c * jax.nn.sigmoid(c)
    w = w_ref[...]
    a_hi, w_hi = a.astype(BF16), w.astype(BF16)
    a_lo = (a - a_hi.astype(F32)).astype(BF16)
    w_lo = (w - w_hi.astype(F32)).astype(BF16)
    on_hi = jnp.dot(jnp.concatenate([a_hi, a_lo], axis=0), w_hi, preferred_element_type=F32)
    o_ref[...] = (on_hi[:MOD_ROWS] + on_hi[MOD_ROWS:] + jnp.dot(a_hi, w_lo, preferred_element_type=F32)
                  + b_ref[...])


def _modulation(c, c_ctx, w_mod, b_mod):
    batch = c.shape[0]
    rows = jnp.concatenate([c, c_ctx[None], jnp.zeros((MOD_ROWS - batch - 1, D_MODEL), F32)], axis=0)
    out = pl.pallas_call(
        _mod_kernel,
        grid=(DEPTH, 6),
        in_specs=[
            pl.BlockSpec((MOD_ROWS, D_MODEL), lambda l, j: (0, 0)),
            pl.BlockSpec((None, D_MODEL, D_MODEL), lambda l, j: (l, 0, j)),
            pl.BlockSpec((None, 1, D_MODEL), lambda l, j: (l, 0, j)),
        ],
        out_specs=pl.BlockSpec((None, MOD_ROWS, D_MODEL), lambda l, j: (l, 0, j)),
        out_shape=jax.ShapeDtypeStruct((DEPTH, MOD_ROWS, 6 * D_MODEL), F32),
        compiler_params=_cparams(2),
        name="modulation",
    )(rows, w_mod, b_mod.reshape(DEPTH, 1, 6 * D_MODEL))
    return out.reshape(DEPTH, MOD_ROWS * 6, 1, D_MODEL)


def _inproj_even_kernel(*refs, rope):
    if rope:
        x_ref, sh_ref, sc_ref, w_ref, cos_ref, sin_ref, f_ref, q_ref, k_ref, vt_ref = refs
    else:
        x_ref, sh_ref, sc_ref, w_ref, _, _, f_ref, q_ref, k_ref, vt_ref = refs
    width = 2 * LANES
    part = vt_ref.shape[2]
    rows = lambda r: slice(r * part, (r + 1) * part)
    norm = lambda r: (_ln(x_ref[rows(r), :]) * (1.0 + sc_ref[...]) + sh_ref[...]).astype(BF16)
    if rope:
        lane = lax.broadcasted_iota(jnp.int32, (part, width), 1)
        first_half = (lane & ROPE_FREQS) == 0
    v0 = FOURIER_W + 2 * QK_W
    n_parts = vt_ref.shape[0]
    h_next = norm(0)
    for r in range(n_parts):
        h = h_next
        f_ref[rows(r), :] = jnp.dot(h, w_ref[:, :FOURIER_W], preferred_element_type=F32)
        if r + 1 < n_parts:
            h_next = norm(r + 1)
        if rope:
            cos = jnp.concatenate([cos_ref[rows(r), :]] * (width // LANES), axis=1)
            sin = jnp.concatenate([sin_ref[rows(r), :]] * (width // LANES), axis=1)
        for j in range(QK_W // width):
            for base, o_ref, scale in ((FOURIER_W, q_ref, Q_SCALE), (FOURIER_W + QK_W, k_ref, None)):
                t = jnp.dot(h, w_ref[:, base + j * width:base + (j + 1) * width], preferred_element_type=F32)
                if rope:
                    partner = jnp.where(first_half, pltpu.roll(t, width - ROPE_FREQS, 1),
                                        pltpu.roll(t, ROPE_FREQS, 1))
                    t = t * cos + partner * sin
                if scale is not None:
                    t = t * scale
                o_ref[rows(r), j * width:(j + 1) * width] = t.astype(BF16)
        for j in range(V_W // width):
            v = jnp.dot(h, w_ref[:, v0 + j * width:v0 + (j + 1) * width], preferred_element_type=F32)
            vt_ref[r, j * width:(j + 1) * width, :] = v.T.astype(BF16)


def _inproj_even(x2d, modl, w_bf, rope_tabs, kv_all, *, batch, n_keys, key_row0, tm, row_fn):
    rows = x2d.shape[0]
    rope = rope_tabs is not None
    tiles_per_batch = rows // batch // tm
    blk0 = key_row0 // tm
    in_specs = [
        pl.BlockSpec((tm, D_MODEL), lambda i: (i, 0)),
        _mod_spec(row_fn, 0),
        _mod_spec(row_fn, 1),
        _const_spec((D_MODEL, EVEN_IN_W)),
    ]
    args = [x2d, modl, modl, w_bf]
    if rope:
        in_specs += [pl.BlockSpec((tm, LANES), lambda i: (i % tiles_per_batch, 0))] * 2
        args += list(rope_tabs)
        aliases = {}
    else:
        in_specs += [pl.BlockSpec(memory_space=pl.ANY)] * 2
        args += list(kv_all)
        aliases = {4: 2, 5: 3}
    kv_idx = lambda i: (i // tiles_per_batch, blk0 + i % tiles_per_batch, 0)
    return pl.pallas_call(
        functools.partial(_inproj_even_kernel, rope=rope),
        grid=(rows // tm,),
        in_specs=in_specs,
        out_specs=[
            pl.BlockSpec((tm, FOURIER_W), lambda i: (i, 0)),
            pl.BlockSpec((tm, QK_W), lambda i: (i, 0)),
            pl.BlockSpec((None, tm, QK_W), kv_idx),
            pl.BlockSpec((None, tm // ATTN_TK, V_W, ATTN_TK), lambda i: kv_idx(i) + (0,)),
        ],
        out_shape=[
            jax.ShapeDtypeStruct((rows, FOURIER_W), F32),
            jax.ShapeDtypeStruct((rows, QK_W), BF16),
            jax.ShapeDtypeStruct((batch, n_keys, QK_W), BF16),
            jax.ShapeDtypeStruct((batch, n_keys // ATTN_TK, V_W, ATTN_TK), BF16),
        ],
        input_output_aliases=aliases,
        compiler_params=_cparams(1),
        name="inproj_even_rope" if rope else "inproj_even",
    )(*args)


def _rope_tables(n):
    rows = n // GRID_W
    row = jnp.repeat(jnp.arange(rows, dtype=F32), GRID_W)
    col = jnp.tile(jnp.arange(GRID_W, dtype=F32), rows)
    inv = jnp.power(ROPE_BASE, -jnp.arange(ROPE_FREQS, dtype=F32) / ROPE_FREQS)
    ang = jnp.stack([row[:, None] * inv, col[:, None] * inv], axis=1)
    cos, sin = jnp.cos(ang), jnp.sin(ang)
    cos_map = jnp.concatenate([cos, cos], axis=-1).reshape(n, 4 * ROPE_FREQS)
    sin_map = jnp.concatenate([-sin, sin], axis=-1).reshape(n, 4 * ROPE_FREQS)
    reps = LANES // (4 * ROPE_FREQS)
    return jnp.tile(cos_map, (1, reps)), jnp.tile(sin_map, (1, reps))


def _attn_kernel(lq1, lk1, lq2, lk2, g_ref, q_ref, k_ref, vt_ref, o_ref, acc_ref, *scratch, lam_init):
    tq = q_ref.shape[0]
    n_chunks, _, tk = vt_ref.shape
    q = q_ref[...]
    lane = lax.broadcasted_iota(jnp.int32, q.shape, 1)
    zero = jnp.zeros_like(q)
    q_cat = jnp.concatenate([jnp.where(lane < DIFF_HD, q, zero), jnp.where(lane >= DIFF_HD, q, zero)], axis=0)
    nt_dims = (((1,), (1,)), ((), ()))

    p_ref, al_ref = scratch[-2:]
    grp = (len(scratch) - 2) // 2
    s_refs, mc_refs = scratch[:grp], scratch[grp:2 * grp]
    n_groups = n_chunks // grp

    def scores(g, r):
        kb = k_ref[pl.ds(pl.multiple_of((g * grp + r) * tk, tk), tk), :]
        s = lax.dot_general(kb, q_cat, nt_dims, preferred_element_type=F32)
        s_refs[r][...] = s
        mc_refs[r][...] = jnp.max(s, axis=0, keepdims=True)

    def group_max(m_prev):
        m = m_prev
        for r in range(grp):
            m = jnp.maximum(m, mc_refs[r][...])
        al_ref[...] = jnp.exp2(m_prev - m)
        return m

    def probs(r, m):
        p_ref[r * tk:(r + 1) * tk, :] = jnp.exp2(s_refs[r][...] - m).astype(BF16)

    ones_rows = (lax.broadcasted_iota(jnp.int32, (ATTN_SUM_ROWS, grp * tk), 0) == 0).astype(BF16)

    def values(g):
        vt = jnp.concatenate([vt_ref[g * grp + r] for r in range(grp)], axis=1)
        vt1 = jnp.concatenate([vt, ones_rows], axis=0)
        acc_ref[...] = acc_ref[...] * al_ref[...] + jnp.dot(vt1, p_ref[...], preferred_element_type=F32)

    acc_ref[...] = jnp.zeros_like(acc_ref)
    m = jnp.full((1, 2 * tq), -jnp.inf, F32)
    for r in range(grp):
        scores(0, r)
    m = group_max(m)
    for r in range(grp):
        probs(r, m)
        if n_groups > 1:
            scores(1, r)

    def body(g, m):
        values(g - 1)
        m = group_max(m)
        for r in range(grp):
            probs(r, m)
            scores(g + 1, r)
        return m

    if n_groups > 2:
        m = lax.fori_loop(1, n_groups - 1, body, m)
    if n_groups > 1:
        values(n_groups - 2)
        m = group_max(m)
        for r in range(grp):
            probs(r, m)
    values(n_groups - 1)

    lam = (jnp.exp(jnp.sum(lq1[...] * lk1[...], keepdims=True))
           - jnp.exp(jnp.sum(lq2[...] * lk2[...], keepdims=True)) + lam_init)
    on = acc_ref[:DIFF_VD, :] / acc_ref[DIFF_VD:DIFF_VD + 1, :]
    o = on[:, :tq] - lam * on[:, tq:]
    o = o * lax.rsqrt(jnp.mean(o * o, axis=0, keepdims=True) + LN_EPS)
    o_ref[...] = o.T * (g_ref[...] * (1.0 - lam_init))


def _diff_attention(q, k_all, vt_all, lam_vecs, subln_g, lam_init, *, batch, tq, key_row0, n_keys):
    rows = q.shape[0]
    nq = rows // batch // tq
    tk = vt_all.shape[3]
    n_chunks = n_keys // tk
    assert key_row0 % n_keys == 0
    kblk = key_row0 // n_keys
    grp = ATTN_GROUP if n_chunks % ATTN_GROUP == 0 else 1
    cols = 2 * tq
    slot_shapes = [((tk, cols), F32), ((1, cols), F32)]
    vec_spec = pl.BlockSpec((1, DIFF_HD), lambda b, h, i: (0, 0))
    in_specs = [vec_spec] * 4 + [
        pl.BlockSpec((1, DIFF_VD), lambda b, h, i: (0, 0)),
        pl.BlockSpec((tq, DIFF_VD), lambda b, h, i: (b * nq + i, h)),
        pl.BlockSpec((None, n_keys, DIFF_VD), lambda b, h, i: (b, kblk, h)),
        pl.BlockSpec((None, n_chunks, DIFF_VD, tk), lambda b, h, i: (b, kblk, h, 0)),
    ]
    args = [v.reshape(1, DIFF_HD) for v in lam_vecs] + [subln_g.reshape(1, DIFF_VD), q, k_all, vt_all]
    return pl.pallas_call(
        functools.partial(_attn_kernel, lam_init=lam_init),
        grid=(batch, DIFF_HEADS, nq),
        in_specs=in_specs,
        out_specs=pl.BlockSpec((tq, DIFF_VD), lambda b, h, i: (b * nq + i, h)),
        out_shape=jax.ShapeDtypeStruct((rows, V_W), F32),
        scratch_shapes=[pltpu.VMEM((DIFF_VD + ATTN_SUM_ROWS, cols), F32)]
        + [pltpu.VMEM(shape, dtype) for shape, dtype in slot_shapes for _ in range(grp)]
        + [pltpu.VMEM((grp * tk, cols), BF16), pltpu.VMEM((1, cols), F32)],
        compiler_params=_cparams(3),
        name="diff_attention" if n_chunks > 1 else "diff_attention_ctx",
    )(*args)


def _dft_cos_sin(n):
    idx = np.outer(np.arange(n), np.arange(n)) % n
    ang = 2.0 * np.pi * idx / n
    return np.cos(ang), np.sin(ang)


def _channel_dft():
    c, s = _dft_cos_sin(FOURIER_GC)
    eye = np.eye(LANES // FOURIER_GC)
    return jnp.asarray(np.kron(eye, c), F32).astype(BF16), jnp.asarray(np.kron(eye, s), F32).astype(BF16)


def _fourier_dense_kernel(x_ref, cl_ref, sl_ref, bdc_ref, bds_ref, o_ref, *, norm):
    x = x_ref[...]
    pr = _mm(cl_ref[...], x)
    pi = -_mm(sl_ref[...], x)
    out = _mm(pr, bdc_ref[...]) + _mm(pi, bds_ref[...])
    o_ref[...] = out * norm


def _fourier_fft_kernel(x_ref, f1_ref, twc_ref, tws_ref, f3_ref, bdc_ref, bds_ref, o_ref, ur_ref, ui_ref,
                        *, n1, norm):
    n2 = LANES

    nb = FFT_BATCH
    lanes = lambda j: slice(j * LANES, (j + 1) * LANES)

    def stage1(ib, carry):
        i0 = ib * nb
        xs = jnp.concatenate([x_ref[pl.ds(i0 + j, n2, stride=n1), :] for j in range(nb)], axis=1)
        t = _mm(f1_ref[...], xs)
        for j in range(nb):
            tr, ti = t[:n2, lanes(j)], t[n2:, lanes(j)]
            r0 = pl.multiple_of((i0 + j) * n2, n2)
            c = twc_ref[pl.ds(r0, n2), :]
            s = tws_ref[pl.ds(r0, n2), :]
            ur_ref[pl.ds(r0, n2), :] = tr * c + ti * s
            ui_ref[pl.ds(r0, n2), :] = ti * c - tr * s
        return carry

    lax.fori_loop(0, n1 // nb, stage1, 0)

    def stage2(kb, carry):
        k0 = kb * nb
        u = jnp.concatenate(
            [jnp.concatenate([ur_ref[pl.ds(k0 + j, n1, stride=n2), :], ui_ref[pl.ds(k0 + j, n1, stride=n2), :]],
                             axis=0) for j in range(nb)], axis=1)
        p = _mm(f3_ref[...], u)
        for j in range(nb):
            ur_ref[pl.ds(k0 + j, n1, stride=n2), :] = p[:n1, lanes(j)]
            ui_ref[pl.ds(k0 + j, n1, stride=n2), :] = p[n1:, lanes(j)]
        return carry

    lax.fori_loop(0, n2 // nb, stage2, 0)

    rows = ROW_TILE

    def stage3(r, carry):
        r0 = pl.multiple_of(r * rows, rows)
        out = _mm(ur_ref[pl.ds(r0, rows), :], bdc_ref[...]) + _mm(ui_ref[pl.ds(r0, rows), :], bds_ref[...])
        o_ref[pl.ds(r0, rows), :] = out * norm
        return carry

    lax.fori_loop(0, (n1 * n2) // rows, stage3, 0)


def _fourier_mix(f2d, *, batch):
    rows = f2d.shape[0]
    n = rows // batch
    norm = 1.0 / math.sqrt(n * FOURIER_GC)
    bdc, bds = _channel_dft()
    blk = pl.BlockSpec((n, LANES), lambda b, j: (b, j))
    grid = (batch, FOURIER_W // LANES)
    out_shape = jax.ShapeDtypeStruct((rows, FOURIER_W), F32)
    mat = _const_spec((LANES, LANES))
    if n <= ROW_TILE:
        c, s = _dft_cos_sin(n)
        return pl.pallas_call(
            functools.partial(_fourier_dense_kernel, norm=norm),
            grid=grid,
            in_specs=[blk, _const_spec((n, n)), _const_spec((n, n)), mat, mat],
            out_specs=blk,
            out_shape=out_shape,
            compiler_params=_cparams(2),
            name="fourier_dense",
        )(f2d, jnp.asarray(c, F32).astype(BF16), jnp.asarray(s, F32).astype(BF16), bdc, bds)
    n2 = LANES
    n1 = n // n2
    c2, s2 = _dft_cos_sin(n2)
    f1 = jnp.asarray(np.concatenate([c2, -s2], axis=0), F32).astype(BF16)
    c1, s1 = _dft_cos_sin(n1)
    f3 = jnp.asarray(np.block([[c1, s1], [-s1, c1]]), F32).astype(BF16)
    tw_idx = np.outer(np.arange(n1), np.arange(n2)).reshape(-1)
    tw_ang = 2.0 * np.pi * tw_idx / n
    twc = jnp.broadcast_to(jnp.asarray(np.cos(tw_ang), F32)[:, None], (n, LANES))
    tws = jnp.broadcast_to(jnp.asarray(np.sin(tw_ang), F32)[:, None], (n, LANES))
    return pl.pallas_call(
        functools.partial(_fourier_fft_kernel, n1=n1, norm=norm),
        grid=grid,
        in_specs=[blk, _const_spec((2 * n2, n2)), _const_spec((n, LANES)), _const_spec((n, LANES)),
                  _const_spec((2 * n1, 2 * n1)), mat, mat],
        out_specs=blk,
        out_shape=out_shape,
        scratch_shapes=[pltpu.VMEM((n, LANES), F32), pltpu.VMEM((n, LANES), F32)],
        compiler_params=_cparams(2),
        name="fourier_fft",
    )(f2d, f1, twc, tws, f3, bdc, bds)


def _gelu_tanh(x):
    return 0.5 * x * (1.0 + jnp.tanh(math.sqrt(2.0 / math.pi) * (x + 0.044715 * (x * x * x))))


def _tail_kernel(*refs, glu, wa):
    if glu:
        x_ref, ya_ref, yb_ref, wg_ref, bg_ref, *rest = refs
    else:
        x_ref, ya_ref, yb_ref, *rest = refs
    (wo_ref, bo_ref, g1_ref, lg1_ref, lb1_ref, sh2_ref, sc2_ref, g2_ref,
     w1_ref, b1_ref, w2_ref, b2_ref, lg2_ref, lb2_ref, o_ref) = rest
    part = min(TAIL_PART_ROWS, x_ref.shape[0])
    n_parts = x_ref.shape[0] // part
    rows = lambda r: slice(r * part, (r + 1) * part)

    def mix_out(r):
        ya = ya_ref[rows(r), :]
        if glu:
            g = _gelu_tanh(ya)
            ya = g * jax.nn.sigmoid(jnp.dot(g.astype(BF16), wg_ref[...], preferred_element_type=F32) + bg_ref[...])
        y = (jnp.dot(ya.astype(BF16), wo_ref[:wa, :], preferred_element_type=F32)
             + jnp.dot(yb_ref[rows(r), :].astype(BF16), wo_ref[wa:, :], preferred_element_type=F32) + bo_ref[...])
        x1 = _ln(ALPHA * x_ref[rows(r), :] + g1_ref[...] * y) * lg1_ref[...] + lb1_ref[...]
        return x1, (_ln(x1) * (1.0 + sc2_ref[...]) + sh2_ref[...]).astype(BF16)

    def ffn_chunk(h, c):
        cols = slice(c * FFN_CHUNK, (c + 1) * FFN_CHUNK)
        a = jnp.maximum(jnp.dot(h, w1_ref[:, cols], preferred_element_type=F32) + b1_ref[:, cols], 0.0)
        return jnp.dot((a * a).astype(BF16), w2_ref[cols, :], preferred_element_type=F32)

    def finish(r, x1, y2):
        o_ref[rows(r), :] = _ln(ALPHA * x1 + g2_ref[...] * (y2 + b2_ref[...])) * lg2_ref[...] + lb2_ref[...]

    state = mix_out(0)
    done = None
    for r in range(n_parts):
        x1, h = state
        y2 = ffn_chunk(h, 0)
        if r + 1 < n_parts:
            state = mix_out(r + 1)
        if done is not None:
            finish(*done)
        for c in range(1, FFN_W // FFN_CHUNK):
            y2 = y2 + ffn_chunk(h, c)
        done = (r, x1, y2)
    finish(*done)


def _layer_tail(x2d, ya, yb, modl, glu_params, w_out, b_out, lg1, lb1, w1, b1, w2, b2, lg2, lb2, *, layer, tm,
                row_fn):
    rows = x2d.shape[0]
    wa, wb = ya.shape[-1], yb.shape[1]
    glu = glu_params is not None
    row = lambda v: v.reshape(1, -1)
    vec = _const_spec((1, D_MODEL))
    if ya.ndim == 3:
        tiles_per_batch = rows // ya.shape[0] // tm
        ya_spec = pl.BlockSpec((None, tm, wa), lambda i: (i // tiles_per_batch, i % tiles_per_batch, 0))
    else:
        ya_spec = pl.BlockSpec((tm, wa), lambda i: (i, 0))
    in_specs = [pl.BlockSpec((tm, D_MODEL), lambda i: (i, 0)), ya_spec,
                pl.BlockSpec((tm, wb), lambda i: (i, 0))]
    args = [x2d, ya, yb]
    if glu:
        in_specs += [_const_spec((wa, wa)), _const_spec((1, wa))]
        args += [glu_params[0], row(glu_params[1])]
    slab = lambda shape: pl.BlockSpec((None,) + shape, lambda i: (layer, 0, 0), pipeline_mode=pl.Buffered(1))
    in_specs += [slab((D_MODEL, D_MODEL)), vec, _mod_spec(row_fn, 2), vec, vec,
                 _mod_spec(row_fn, 3), _mod_spec(row_fn, 4), _mod_spec(row_fn, 5),
                 slab((D_MODEL, FFN_W)), _const_spec((1, FFN_W)), slab((FFN_W, D_MODEL)), vec, vec, vec]
    args += [w_out, row(b_out), modl, row(lg1), row(lb1), modl, modl, modl,
             w1, row(b1), w2, row(b2), row(lg2), row(lb2)]
    return pl.pallas_call(
        functools.partial(_tail_kernel, glu=glu, wa=wa),
        grid=(rows // tm,),
        in_specs=in_specs,
        out_specs=pl.BlockSpec((tm, D_MODEL), lambda i: (i, 0)),
        out_shape=jax.ShapeDtypeStruct((rows, D_MODEL), F32),
        compiler_params=_cparams(1),
        name="layer_tail_glu" if glu else "layer_tail",
    )(*args)


def _inproj_odd_kernel(x_ref, sh_ref, sc_ref, w_ref, wsp_ref, bsp_ref, s_ref, gm_ref):
    part = min(ODD_PART_ROWS, x_ref.shape[0])
    n_parts = x_ref.shape[0] // part
    n_ch = part // CHUNK
    rows = lambda r: slice(r * part, (r + 1) * part)
    norm = lambda r: (_ln(x_ref[rows(r), :]) * (1.0 + sc_ref[...]) + sh_ref[...]).astype(BF16)

    def gate(r, u, v):
        chunk = lambda ch: slice(ch * CHUNK, (ch + 1) * CHUNK)
        for g in range(GMLP_GROUPS):
            cols = slice(g * GMLP_GC, (g + 1) * GMLP_GC)
            vn = jnp.concatenate([_ln(v[chunk(ch), cols]).astype(BF16) for ch in range(n_ch)], axis=1)
            sp = jnp.dot(wsp_ref[g], vn, preferred_element_type=F32)
            for ch in range(n_ch):
                out_rows = slice(r * part + ch * CHUNK, r * part + (ch + 1) * CHUNK)
                gm_ref[out_rows, cols] = u[chunk(ch), cols] * (sp[:, ch * GMLP_GC:(ch + 1) * GMLP_GC] + bsp_ref[g])

    h_next = norm(0)
    pending = None
    for r in range(n_parts):
        h = h_next
        s_ref[rows(r), :] = jnp.dot(h, w_ref[:, :S5_W], preferred_element_type=F32)
        if pending is not None:
            gate(*pending)
        u = jnp.dot(h, w_ref[:, S5_W:S5_W + GMLP_W], preferred_element_type=F32)
        if r + 1 < n_parts:
            h_next = norm(r + 1)
        v = jnp.dot(h, w_ref[:, S5_W + GMLP_W:], preferred_element_type=F32)
        pending = (r, u, v)
    gate(*pending)


def _inproj_odd(x2d, modl, w_bf, wsp_bf, bsp_lanes, *, batch, n_rows, tm, row_fn):
    rows = x2d.shape[0]
    tiles_per_batch = rows // batch // tm
    return pl.pallas_call(
        _inproj_odd_kernel,
        grid=(rows // tm,),
        in_specs=[pl.BlockSpec((tm, D_MODEL), lambda i: (i, 0)), _mod_spec(row_fn, 0), _mod_spec(row_fn, 1),
                  _const_spec((D_MODEL, ODD_IN_W)), _const_spec((GMLP_GROUPS, CHUNK, CHUNK)),
                  _const_spec((GMLP_GROUPS, CHUNK, GMLP_GC))],
        out_specs=[pl.BlockSpec((None, tm, S5_W), lambda i: (i // tiles_per_batch, i % tiles_per_batch, 0)),
                   pl.BlockSpec((tm, GMLP_W), lambda i: (i, 0))],
        out_shape=[jax.ShapeDtypeStruct((batch, n_rows, S5_W), F32), jax.ShapeDtypeStruct((rows, GMLP_W), F32)],
        compiler_params=_cparams(1),
        name="inproj_odd_gmlp",
    )(x2d, modl, modl, w_bf, wsp_bf, bsp_lanes)


def _ln_mod_matmul_kernel(x_ref, sh_ref, sc_ref, w_ref, _, o_ref):
    h = (_ln(x_ref[...]) * (1.0 + sc_ref[...]) + sh_ref[...]).astype(BF16)
    o_ref[...] = jnp.dot(h, w_ref[...], preferred_element_type=F32)


def _ln_mod_matmul_into(x2d, modl, w_bf, dst, *, batch, row0, tm, row_fn):
    rows, n = x2d.shape[0], w_bf.shape[1]
    tiles_per_batch = rows // batch // tm
    blk0 = row0 // tm
    return pl.pallas_call(
        _ln_mod_matmul_kernel,
        grid=(rows // tm,),
        in_specs=[pl.BlockSpec((tm, D_MODEL), lambda i: (i, 0)), _mod_spec(row_fn, 0), _mod_spec(row_fn, 1),
                  _const_spec((D_MODEL, n)), pl.BlockSpec(memory_space=pl.ANY)],
        out_specs=pl.BlockSpec((None, tm, n), lambda i: (i // tiles_per_batch, blk0 + i % tiles_per_batch, 0)),
        out_shape=jax.ShapeDtypeStruct(dst.shape, dst.dtype),
        input_output_aliases={4: 0},
        compiler_params=_cparams(1),
        name="ln_mod_matmul",
    )(x2d, modl, modl, w_bf, dst)


def _s5_prep_kernel(pw_ref, bb_ref, cc_ref, dw_ref, w_ref, vt_ref, m_ref):
    t_len, gc = S5_T, S5_GC
    width = t_len * gc
    nt_dims = (((1,), (1,)), ((), ()))
    cols = lambda q: slice(q * LANES, (q + 1) * LANES)
    lane = lax.broadcasted_iota(jnp.int32, (gc, width), 1)
    for g in range(2):
        wide = []
        for d in range(2):
            pr, pi = pw_ref[g, d, 0], pw_ref[g, d, 1]
            bbr, bbi = bb_ref[g, d, 0], bb_ref[g, d, 1]
            cr, ci = cc_ref[g, d, 0], cc_ref[g, d, 1]

            def power(xr, xi, t, pr=pr, pi=pi):
                return pr[t:t + 1] * xr - pi[t:t + 1] * xi, pr[t:t + 1] * xi + pi[t:t + 1] * xr

            lc = [power(cr, ci, t) for t in range(t_len + 1)]
            for s in range(t_len):
                blk = slice(g * width + s * gc, g * width + (s + 1) * gc)
                lbr, lbi = power(bbr, bbi, t_len - 1 - s if d == 0 else s)
                w_ref[blk, cols(2 * d)] = lbr.astype(BF16)
                w_ref[blk, cols(2 * d + 1)] = lbi.astype(BF16)
                lcr, lci = lc[s + 1 if d == 0 else t_len - s]
                vt_ref[blk, cols(2 * d)] = lcr.astype(BF16)
                vt_ref[blk, cols(2 * d + 1)] = (-lci).astype(BF16)
            order = range(t_len) if d == 0 else range(t_len - 1, -1, -1)
            lcr_all = jnp.concatenate([lc[t][0] for t in order], axis=0)
            lci_all = jnp.concatenate([lc[t][1] for t in order], axis=0)
            wide.append(lax.dot_general(bbr, lcr_all, nt_dims, preferred_element_type=F32, precision=HIGHEST)
                        - lax.dot_general(bbi, lci_all, nt_dims, preferred_element_type=F32, precision=HIGHEST))
        kf = wide[0] + dw_ref[g]
        kb = wide[1]
        for s in range(t_len):
            fwd = jnp.where(lane >= s * gc, pltpu.roll(kf, s * gc, 1), 0.0)
            bwd = jnp.where(lane < (s + 1) * gc, pltpu.roll(kb, (s + 1) * gc % width, 1), 0.0)
            m_ref[g, s * gc:(s + 1) * gc, :] = (fwd + bwd).astype(BF16)


def _s5_matrices(lam_re, lam_im, log_dt, b_re, b_im, c_re, c_im, d_skip):
    t_len, g_n, p_n, i_n = S5_T, S5_GROUPS, S5_STATE, S5_GC
    tau_rows = -(-(t_len + 1) // 8) * 8
    dt = jnp.exp(log_dt)[..., None]
    mag, th = lam_re * dt, lam_im * dt
    er = jnp.exp(mag)
    lbr, lbi = er * jnp.cos(th), er * jnp.sin(th)
    den = lam_re * lam_re + lam_im * lam_im
    nr, ni = lbr - 1.0, lbi
    cr = (nr * lam_re + ni * lam_im) / den
    ci = (ni * lam_re - nr * lam_im) / den
    tau = jnp.arange(tau_rows, dtype=F32)[:, None]
    pm = jnp.exp(tau * mag[:, :, None, :])
    ang = tau * th[:, :, None, :]
    pw = jnp.stack([pm * jnp.cos(ang), pm * jnp.sin(ang)], axis=2)
    to_ip = lambda a: jnp.swapaxes(a, -1, -2)
    bb = jnp.stack([cr[..., None, :] * to_ip(b_re) - ci[..., None, :] * to_ip(b_im),
                    cr[..., None, :] * to_ip(b_im) + ci[..., None, :] * to_ip(b_re)], axis=2)
    cc = jnp.stack([c_re, c_im], axis=2)

    def per_group(a):
        a = jnp.swapaxes(a, 0, 1)
        a = a.reshape((g_n // 2, 2) + a.shape[1:])
        zero = jnp.zeros_like(a[:, 0])
        both = jnp.stack([jnp.concatenate([a[:, 0], zero], axis=-1), jnp.concatenate([zero, a[:, 1]], axis=-1)], axis=1)
        return both.reshape((g_n,) + both.shape[2:])

    width = t_len * i_n
    d_wide = jnp.pad(d_skip.reshape(g_n, i_n)[:, None, :] * jnp.asarray(np.eye(i_n), F32),
                     ((0, 0), (0, 0), (0, width - i_n)))
    pair = lambda shape: pl.BlockSpec((2,) + shape, lambda j: (j,) + (0,) * len(shape))
    state_spec = pl.BlockSpec((None, 2 * width, 4 * LANES), lambda j: (j, 0, 0))
    state_shape = jax.ShapeDtypeStruct((S5_PAIRS, 2 * width, 4 * LANES), BF16)
    w_state, vt_state, m_pair = pl.pallas_call(
        _s5_prep_kernel,
        grid=(S5_PAIRS,),
        in_specs=[pair((2, 2, tau_rows, 2 * p_n)), pair((2, 2, i_n, 2 * p_n)), pair((2, 2, i_n, 2 * p_n)),
                  pair((i_n, width))],
        out_specs=[state_spec, state_spec, pl.BlockSpec((None, 2, width, width), lambda j: (j, 0, 0, 0))],
        out_shape=[state_shape, state_shape, jax.ShapeDtypeStruct((S5_PAIRS, 2, width, width), BF16)],
        compiler_params=_cparams(1),
        name="s5_prep",
    )(per_group(pw), per_group(bb), per_group(cc), d_wide)
    a_pow = jnp.stack([pw[0, :, 0, t_len], pw[0, :, 1, t_len], pw[1, :, 0, t_len], pw[1, :, 1, t_len]], axis=0)
    return m_pair, w_state, vt_state, a_pow.reshape(4, g_n * p_n)


def _s5_fold_perm():
    n_t, n_g = LANES // S5_GC, LANES // S5_GC
    src = np.arange(n_t * LANES)
    t_lo, g, i = src // LANES, (src % LANES) // S5_GC, src % S5_GC
    perm = np.zeros((n_t * LANES, n_g * LANES), np.float32)
    perm[src, g * LANES + t_lo * S5_GC + i] = 1.0
    return perm


def _s5_kernel(s_ref, pb_ref, pbt_ref, m_ref, w_ref, v_ref, a_ref, y_ref, u_scr, st_scr, yp_scr, *, n_lat, n_ctx):
    n_all = n_lat + n_ctx
    n_t = LANES // S5_GC
    width = S5_T * S5_GC
    blk = lambda j: slice(j * LANES, (j + 1) * LANES)
    for t_hi in range(S5_T // n_t):
        x = jnp.concatenate([s_ref[pl.ds(n_t * t_hi + t_lo, n_all, stride=S5_T), :].astype(BF16)
                             for t_lo in range(n_t)], axis=1)
        xp = jnp.dot(x, pb_ref[...], preferred_element_type=F32).astype(BF16)
        for g in range(n_t):
            u_scr[g, :, blk(t_hi)] = xp[:, blk(g)]

    n_pairs = w_ref.shape[0]
    for pr in range(n_pairs):
        u_pair = jnp.concatenate([u_scr[2 * pr], u_scr[2 * pr + 1]], axis=1)
        st = jnp.dot(u_pair, w_ref[pr], preferred_element_type=F32)
        for q in range(4):
            st_scr[q, :, blk(pr)] = st[:, blk(q)]

    afr, afi, abr, abi = (a_ref[pl.ds(q, 1), :] for q in range(4))

    def scan_step(k, carry):
        hr, hi, gr, gi = carry
        rf = pl.ds(jnp.where(k < n_ctx, k + n_lat, k - n_ctx), 1)
        rb = pl.ds(n_all - 1 - k, 1)
        sfr, sfi, sbr, sbi = st_scr[0, rf, :], st_scr[1, rf, :], st_scr[2, rb, :], st_scr[3, rb, :]
        st_scr[0, rf, :] = hr
        st_scr[1, rf, :] = hi
        st_scr[2, rb, :] = gr
        st_scr[3, rb, :] = gi
        return (afr * hr - afi * hi + sfr, afr * hi + afi * hr + sfi,
                abr * gr - abi * gi + sbr, abr * gi + abi * gr + sbi)

    lax.fori_loop(0, n_all, scan_step, (jnp.zeros_like(afr),) * 4, unroll=S5_SCAN_UNROLL)

    for pr in range(n_pairs):
        hst = jnp.concatenate([st_scr[q, :, blk(pr)].astype(BF16) for q in range(4)], axis=1)
        y_pair = lax.dot_general(hst, v_ref[pr], (((1,), (1,)), ((), ())), preferred_element_type=F32)
        for g in range(2):
            y_g = (y_pair[:, g * width:(g + 1) * width]
                   + jnp.dot(u_scr[2 * pr + g], m_ref[pr, g], preferred_element_type=F32))
            for t_hi in range(S5_T // n_t):
                yp_scr[t_hi, :, blk(2 * pr + g)] = y_g[:, blk(t_hi)].astype(BF16)

    for t_hi in range(S5_T // n_t):
        z = jnp.dot(yp_scr[t_hi], pbt_ref[...], preferred_element_type=F32)
        for t_lo in range(n_t):
            y_ref[pl.ds(n_t * t_hi + t_lo, n_all, stride=S5_T), :] = z[:, blk(t_lo)]


def _s5_mix(s_all, mats, *, n_lat_rows):
    m_pair, w_state, v_state, a_pow = mats
    batch, n_rows, _ = s_all.shape
    n_lat = n_lat_rows // S5_T
    n_all = n_rows // S5_T
    width = S5_T * S5_GC
    n_blocks = S5_W // LANES
    pairs = S5_PAIRS // n_blocks
    perm = _s5_fold_perm()
    rows_spec = pl.BlockSpec((None, n_rows, LANES), lambda b, v: (b, 0, v))
    return pl.pallas_call(
        functools.partial(_s5_kernel, n_lat=n_lat, n_ctx=n_all - n_lat),
        grid=(batch, n_blocks),
        in_specs=[rows_spec, _const_spec(perm.shape), _const_spec(perm.shape),
                  pl.BlockSpec((pairs, 2, width, width), lambda b, v: (v, 0, 0, 0)),
                  pl.BlockSpec((pairs, 2 * width, 4 * LANES), lambda b, v: (v, 0, 0)),
                  pl.BlockSpec((pairs, 4 * LANES, 2 * width), lambda b, v: (v, 0, 0)),
                  pl.BlockSpec((4, pairs * LANES), lambda b, v: (0, v))],
        out_specs=rows_spec,
        out_shape=jax.ShapeDtypeStruct(s_all.shape, F32),
        scratch_shapes=[pltpu.VMEM((2 * pairs, n_all, width), BF16), pltpu.VMEM((4, n_all, pairs * LANES), F32),
                        pltpu.VMEM((S5_T * S5_GC // LANES, n_all, 2 * pairs * LANES), BF16)],
        compiler_params=_cparams(2),
        name="s5_mix",
    )(s_all, jnp.asarray(perm, BF16), jnp.asarray(perm.T, BF16), m_pair, w_state, v_state, a_pow)


def kernel(x, c, ctx, c_ctx, w_mod, b_mod, w_out, b_out, ln_mix_g, ln_mix_b, w_ffn1, b_ffn1, w_ffn2,
           b_ffn2, ln_ffn_g, ln_ffn_b, w_in_ab, lam_q1, lam_k1, lam_q2, lam_k2, subln_g, w_in_cd,
           s5_lam_re, s5_lam_im, s5_log_dt, s5_b_re, s5_b_im, s5_c_re, s5_c_im, s5_d, w_glu, b_glu,
           w_sp, b_sp):
    batch, n_lat, d = x.shape
    n_ctx = ctx.shape[1]
    assert d == D_MODEL and n_lat % TAIL_TILE == 0 and n_ctx % CHUNK == 0 and batch + 1 <= MOD_ROWS
    mod = _modulation(c, c_ctx, w_mod, b_mod)
    xl = x.reshape(batch * n_lat, d)
    xc = ctx.reshape(batch * n_ctx, d)
    tiles = n_lat // ROW_TILE
    lat_rows = dict(tm=ROW_TILE, row_fn=lambda i: i // tiles)
    tail_tiles = n_lat // TAIL_TILE
    lat_tail_rows = dict(tm=TAIL_TILE, row_fn=lambda i: i // tail_tiles)
    ctx_rows = dict(tm=n_ctx, row_fn=lambda i: batch)
    rope_tabs = _rope_tables(n_lat)
    w_out_bf, w_ffn1_bf, w_ffn2_bf = w_out.astype(BF16), w_ffn1.astype(BF16), w_ffn2.astype(BF16)
    for l in range(DEPTH):
        last = l == DEPTH - 1
        e = l // 2
        modl = mod[l]
        tail_w = (w_out_bf, b_out[l], ln_mix_g[l], ln_mix_b[l], w_ffn1_bf, b_ffn1[l],
                  w_ffn2_bf, b_ffn2[l], ln_ffn_g[l], ln_ffn_b[l])
        if l % 2 == 0:
            w_in = w_in_ab[e].astype(BF16)
            lam_init = 0.8 - 0.6 * math.exp(-0.3 * l)
            lam_vecs = (lam_q1[e], lam_k1[e], lam_q2[e], lam_k2[e])
            n_keys = n_lat + n_ctx
            f, q, k_all, vt_all = _inproj_even(xl, modl, w_in, rope_tabs, None, batch=batch, n_keys=n_keys,
                                               key_row0=0, **lat_rows)
            fc, qc, k_all, vt_all = _inproj_even(xc, modl, w_in, None, (k_all, vt_all), batch=batch, n_keys=n_keys,
                                                 key_row0=n_lat, **ctx_rows)
            attend = functools.partial(_diff_attention, lam_vecs=lam_vecs, subln_g=subln_g[e], lam_init=lam_init,
                                       batch=batch)
            ya = attend(q, k_all, vt_all, tq=min(ATTN_TQ, n_lat), key_row0=0, n_keys=n_keys)
            yf = _fourier_mix(f, batch=batch)
            xl = _layer_tail(xl, yf, ya, modl, None, *tail_w, layer=l, **lat_tail_rows)
            if not last:
                yac = attend(qc, k_all, vt_all, tq=n_ctx, key_row0=n_lat, n_keys=n_ctx)
                yfc = _fourier_mix(fc, batch=batch)
                xc = _layer_tail(xc, yfc, yac, modl, None, *tail_w, layer=l, tm=batch * n_ctx, row_fn=lambda i: batch)
        else:
            assert last, "the S5 / gMLP layer is only implemented as the final layer"
            w_in = w_in_cd[e].astype(BF16)
            bsp_lanes = jnp.broadcast_to(b_sp[e][:, :, None], (GMLP_GROUPS, CHUNK, GMLP_GC))
            s_all, gm = _inproj_odd(xl, modl, w_in, w_sp[e].astype(BF16), bsp_lanes, batch=batch,
                                    n_rows=n_lat + n_ctx, **lat_rows)
            s_all = _ln_mod_matmul_into(xc, modl, w_in[:, :S5_W], s_all, batch=batch, row0=n_lat, **ctx_rows)
            mats = _s5_matrices(s5_lam_re[e], s5_lam_im[e], s5_log_dt[e], s5_b_re[e], s5_b_im[e], s5_c_re[e],
                                s5_c_im[e], s5_d[e])
            ys = _s5_mix(s_all, mats, n_lat_rows=n_lat)
            xl = _layer_tail(xl, ys, gm, modl, (w_glu[e].astype(BF16), b_glu[e]), *tail_w, layer=l, **lat_tail_rows)
    return xl.reshape(batch, n_lat, d)
```

```python
import functools
import math

import numpy as np
import jax
import jax.numpy as jnp
from jax import lax
from jax.experimental import pallas as pl
from jax.experimental.pallas import tpu as pltpu

D_MODEL = 1024
DEPTH = 2
GRID_W = 64
FOURIER_W = D_MODEL // 4
FOURIER_GC = 64
DIFF_HD = 64
DIFF_VD = 2 * DIFF_HD
DIFF_HEADS = (D_MODEL - FOURIER_W) // DIFF_VD
QK_W = DIFF_HEADS * 2 * DIFF_HD
V_W = DIFF_HEADS * DIFF_VD
EVEN_IN_W = FOURIER_W + 2 * QK_W + V_W
DIFF_SCALE = DIFF_HD ** -0.5
ROPE_BASE = 10000.0
ROPE_FREQS = DIFF_HD // 4
S5_W = D_MODEL // 2
S5_GC = 16
S5_GROUPS = S5_W // S5_GC
S5_STATE = 64
GMLP_W = D_MODEL // 2
GMLP_GC = 128
GMLP_GROUPS = GMLP_W // GMLP_GC
CHUNK = 128
ODD_IN_W = S5_W + 2 * GMLP_W
FFN_W = 4 * D_MODEL
LN_EPS = 1e-5
ALPHA = (2 * DEPTH) ** 0.25

F32 = jnp.float32
BF16 = jnp.bfloat16
HIGHEST = lax.Precision.HIGHEST

LANES = 128
MOD_ROWS = 8
ROW_TILE = 1024
ATTN_TQ = 2048
ATTN_TK = 256
ATTN_GROUP = 3
ATTN_SUM_ROWS = 16
FFT_BATCH = 8
FFT_PITCH_PAD = 8
S5_T = 16
S5_PAIRS = S5_GROUPS // 2
S5_SCAN_UNROLL = 4
FFN_CHUNK = 1024
ODD_PART_ROWS = 256
TAIL_TILE = 1024
TAIL_PART_ROWS = 256
VMEM_LIMIT = 56 * 2 ** 20
Q_SCALE = DIFF_SCALE * math.log2(math.e)


def _cparams(n_axes, vmem=VMEM_LIMIT):
    return pltpu.CompilerParams(dimension_semantics=("arbitrary",) * n_axes, vmem_limit_bytes=vmem)


def _const_spec(shape):
    zeros = (0,) * len(shape)
    return pl.BlockSpec(shape, lambda *_: zeros, pipeline_mode=pl.Buffered(1))


def _ln(x):
    xc = x - jnp.mean(x, -1, keepdims=True)
    var = jnp.mean(xc * xc, -1, keepdims=True)
    return xc * lax.rsqrt(var + LN_EPS)


def _mm(a, b):
    return jnp.dot(a.astype(BF16), b.astype(BF16), preferred_element_type=F32)


def _mod_spec(row_fn, which):
    return pl.BlockSpec((None, 1, D_MODEL), lambda i: (row_fn(i) * 6 + which, 0, 0))


def _mod_kernel(c_ref, w_ref, b_ref, o_ref):
    c = c_ref[...]
    a = c * jax.nn.sigmoid(c)
    w = w_ref[...]
    a_hi, w_hi = a.astype(BF16), w.astype(BF16)
    a_lo = (a - a_hi.astype(F32)).astype(BF16)
    w_lo = (w - w_hi.astype(F32)).astype(BF16)
    on_hi = jnp.dot(jnp.concatenate([a_hi, a_lo], axis=0), w_hi, preferred_element_type=F32)
    o_ref[...] = (on_hi[:MOD_ROWS] + on_hi[MOD_ROWS:] + jnp.dot(a_hi, w_lo, preferred_element_type=F32)
                  + b_ref[...])


def _modulation(c, c_ctx, w_mod, b_mod):
    batch = c.shape[0]
    rows = jnp.concatenate([c, c_ctx[None], jnp.zeros((MOD_ROWS - batch - 1, D_MODEL), F32)], axis=0)
    out = pl.pallas_call(
        _mod_kernel,
        grid=(DEPTH, 6),
        in_specs=[
            pl.BlockSpec((MOD_ROWS, D_MODEL), lambda l, j: (0, 0)),
            pl.BlockSpec((None, D_MODEL, D_MODEL), lambda l, j: (l, 0, j)),
            pl.BlockSpec((None, 1, D_MODEL), lambda l, j: (l, 0, j)),
        ],
        out_specs=pl.BlockSpec((None, MOD_ROWS, D_MODEL), lambda l, j: (l, 0, j)),
        out_shape=jax.ShapeDtypeStruct((DEPTH, MOD_ROWS, 6 * D_MODEL), F32),
        compiler_params=_cparams(2),
        name="modulation",
    )(rows, w_mod, b_mod.reshape(DEPTH, 1, 6 * D_MODEL))
    return out.reshape(DEPTH, MOD_ROWS * 6, 1, D_MODEL)


def _inproj_even_kernel(*refs, rope):
    if rope:
        x_ref, sh_ref, sc_ref, w_ref, cos_ref, sin_ref, f_ref, q_ref, k_ref, vt_ref = refs
    else:
        x_ref, sh_ref, sc_ref, w_ref, _, _, f_ref, q_ref, k_ref, vt_ref = refs
    width = 2 * LANES
    part = vt_ref.shape[2]
    rows = lambda r: slice(r * part, (r + 1) * part)
    norm = lambda r: (_ln(x_ref[rows(r), :]) * (1.0 + sc_ref[...]) + sh_ref[...]).astype(BF16)
    if rope:
        lane = lax.broadcasted_iota(jnp.int32, (part, width), 1)
        first_half = (lane & ROPE_FREQS) == 0
    v0 = FOURIER_W + 2 * QK_W
    n_parts = vt_ref.shape[0]
    h_next = norm(0)
    for r in range(n_parts):
        h = h_next
        f_ref[rows(r), :] = jnp.dot(h, w_ref[:, :FOURIER_W], preferred_element_type=F32)
        if r + 1 < n_parts:
            h_next = norm(r + 1)
        if rope:
            cos = jnp.concatenate([cos_ref[rows(r), :]] * (width // LANES), axis=1)
            sin = jnp.concatenate([sin_ref[rows(r), :]] * (width // LANES), axis=1)
        for j in range(QK_W // width):
            for base, o_ref, scale in ((FOURIER_W, q_ref, Q_SCALE), (FOURIER_W + QK_W, k_ref, None)):
                t = jnp.dot(h, w_ref[:, base + j * width:base + (j + 1) * width], preferred_element_type=F32)
                if rope:
                    partner = jnp.where(first_half, pltpu.roll(t, width - ROPE_FREQS, 1),
                                        pltpu.roll(t, ROPE_FREQS, 1))
                    t = t * cos + partner * sin
                if scale is not None:
                    t = t * scale
                o_ref[rows(r), j * width:(j + 1) * width] = t.astype(BF16)
        for j in range(V_W // width):
            v = jnp.dot(h, w_ref[:, v0 + j * width:v0 + (j + 1) * width], preferred_element_type=F32)
            vt_ref[r, j * width:(j + 1) * width, :] = v.T.astype(BF16)


def _inproj_even(x2d, modl, w_bf, rope_tabs, kv_all, *, batch, n_keys, key_row0, tm, row_fn):
    rows = x2d.shape[0]
    rope = rope_tabs is not None
    tiles_per_batch = rows // batch // tm
    blk0 = key_row0 // tm
    in_specs = [
        pl.BlockSpec((tm, D_MODEL), lambda i: (i, 0)),
        _mod_spec(row_fn, 0),
        _mod_spec(row_fn, 1),
        _const_spec((D_MODEL, EVEN_IN_W)),
    ]
    args = [x2d, modl, modl, w_bf]
    if rope:
        in_specs += [pl.BlockSpec((tm, LANES), lambda i: (i % tiles_per_batch, 0))] * 2
        args += list(rope_tabs)
        aliases = {}
    else:
        in_specs += [pl.BlockSpec(memory_space=pl.ANY)] * 2
        args += list(kv_all)
        aliases = {4: 2, 5: 3}
    kv_idx = lambda i: (i // tiles_per_batch, blk0 + i % tiles_per_batch, 0)
    return pl.pallas_call(
        functools.partial(_inproj_even_kernel, rope=rope),
        grid=(rows // tm,),
        in_specs=in_specs,
        out_specs=[
            pl.BlockSpec((tm, FOURIER_W), lambda i: (i, 0)),
            pl.BlockSpec((tm, QK_W), lambda i: (i, 0)),
            pl.BlockSpec((None, tm, QK_W), kv_idx),
            pl.BlockSpec((None, tm // ATTN_TK, V_W, ATTN_TK), lambda i: kv_idx(i) + (0,)),
        ],
        out_shape=[
            jax.ShapeDtypeStruct((rows, FOURIER_W), F32),
            jax.ShapeDtypeStruct((rows, QK_W), BF16),
            jax.ShapeDtypeStruct((batch, n_keys, QK_W), BF16),
            jax.ShapeDtypeStruct((batch, n_keys // ATTN_TK, V_W, ATTN_TK), BF16),
        ],
        input_output_aliases=aliases,
        compiler_params=_cparams(1),
        name="inproj_even_rope" if rope else "inproj_even",
    )(*args)


def _rope_tables(n):
    rows = n // GRID_W
    row = jnp.repeat(jnp.arange(rows, dtype=F32), GRID_W)
    col = jnp.tile(jnp.arange(GRID_W, dtype=F32), rows)
    inv = jnp.power(ROPE_BASE, -jnp.arange(ROPE_FREQS, dtype=F32) / ROPE_FREQS)
    ang = jnp.stack([row[:, None] * inv, col[:, None] * inv], axis=1)
    cos, sin = jnp.cos(ang), jnp.sin(ang)
    cos_map = jnp.concatenate([cos, cos], axis=-1).reshape(n, 4 * ROPE_FREQS)
    sin_map = jnp.concatenate([-sin, sin], axis=-1).reshape(n, 4 * ROPE_FREQS)
    reps = LANES // (4 * ROPE_FREQS)
    return jnp.tile(cos_map, (1, reps)), jnp.tile(sin_map, (1, reps))


def _attn_kernel(lq1, lk1, lq2, lk2, g_ref, q_ref, k_ref, vt_ref, o_ref, acc_ref, *scratch, lam_init):
    tq = q_ref.shape[0]
    n_chunks, _, tk = vt_ref.shape
    q = q_ref[...]
    lane = lax.broadcasted_iota(jnp.int32, q.shape, 1)
    zero = jnp.zeros_like(q)
    q_cat = jnp.concatenate([jnp.where(lane < DIFF_HD, q, zero), jnp.where(lane >= DIFF_HD, q, zero)], axis=0)
    nt_dims = (((1,), (1,)), ((), ()))

    p_ref, al_ref = scratch[-2:]
    grp = (len(scratch) - 2) // 2
    s_refs, mc_refs = scratch[:grp], scratch[grp:2 * grp]
    n_groups = n_chunks // grp

    def scores(g, r):
        kb = k_ref[pl.ds(pl.multiple_of((g * grp + r) * tk, tk), tk), :]
        s = lax.dot_general(kb, q_cat, nt_dims, preferred_element_type=F32)
        s_refs[r][...] = s
        mc_refs[r][...] = jnp.max(s, axis=0, keepdims=True)

    def group_max(m_prev):
        m = m_prev
        for r in range(grp):
            m = jnp.maximum(m, mc_refs[r][...])
        al_ref[...] = jnp.exp2(m_prev - m)
        return m

    def probs(r, m):
        p_ref[r * tk:(r + 1) * tk, :] = jnp.exp2(s_refs[r][...] - m).astype(BF16)

    ones_rows = (lax.broadcasted_iota(jnp.int32, (ATTN_SUM_ROWS, grp * tk), 0) == 0).astype(BF16)

    def values(g):
        vt = jnp.concatenate([vt_ref[g * grp + r] for r in range(grp)], axis=1)
        vt1 = jnp.concatenate([vt, ones_rows], axis=0)
        acc_ref[...] = acc_ref[...] * al_ref[...] + jnp.dot(vt1, p_ref[...], preferred_element_type=F32)

    acc_ref[...] = jnp.zeros_like(acc_ref)
    m = jnp.full((1, 2 * tq), -jnp.inf, F32)
    for r in range(grp):
        scores(0, r)
    m = group_max(m)
    for r in range(grp):
        probs(r, m)
        if n_groups > 1:
            scores(1, r)

    def body(g, m):
        values(g - 1)
        m = group_max(m)
        for r in range(grp):
            probs(r, m)
            scores(g + 1, r)
        return m

    if n_groups > 2:
        m = lax.fori_loop(1, n_groups - 1, body, m)
    if n_groups > 1:
        values(n_groups - 2)
        m = group_max(m)
        for r in range(grp):
            probs(r, m)
    values(n_groups - 1)

    lam = (jnp.exp(jnp.sum(lq1[...] * lk1[...], keepdims=True))
           - jnp.exp(jnp.sum(lq2[...] * lk2[...], keepdims=True)) + lam_init)
    on = acc_ref[:DIFF_VD, :] / acc_ref[DIFF_VD:DIFF_VD + 1, :]
    o = on[:, :tq] - lam * on[:, tq:]
    o = o * lax.rsqrt(jnp.mean(o * o, axis=0, keepdims=True) + LN_EPS)
    o_ref[...] = o.T * (g_ref[...] * (1.0 - lam_init))


def _diff_attention(q, k_all, vt_all, lam_vecs, subln_g, lam_init, *, batch, tq, key_row0, n_keys):
    rows = q.shape[0]
    nq = rows // batch // tq
    tk = vt_all.shape[3]
    n_chunks = n_keys // tk
    assert key_row0 % n_keys == 0
    kblk = key_row0 // n_keys
    grp = ATTN_GROUP if n_chunks % ATTN_GROUP == 0 else 1
    cols = 2 * tq
    slot_shapes = [((tk, cols), F32), ((1, cols), F32)]
    vec_spec = pl.BlockSpec((1, DIFF_HD), lambda b, h, i: (0, 0))
    in_specs = [vec_spec] * 4 + [
        pl.BlockSpec((1, DIFF_VD), lambda b, h, i: (0, 0)),
        pl.BlockSpec((tq, DIFF_VD), lambda b, h, i: (b * nq + i, h)),
        pl.BlockSpec((None, n_keys, DIFF_VD), lambda b, h, i: (b, kblk, h)),
        pl.BlockSpec((None, n_chunks, DIFF_VD, tk), lambda b, h, i: (b, kblk, h, 0)),
    ]
    args = [v.reshape(1, DIFF_HD) for v in lam_vecs] + [subln_g.reshape(1, DIFF_VD), q, k_all, vt_all]
    return pl.pallas_call(
        functools.partial(_attn_kernel, lam_init=lam_init),
        grid=(batch, DIFF_HEADS, nq),
        in_specs=in_specs,
        out_specs=pl.BlockSpec((tq, DIFF_VD), lambda b, h, i: (b * nq + i, h)),
        out_shape=jax.ShapeDtypeStruct((rows, V_W), F32),
        scratch_shapes=[pltpu.VMEM((DIFF_VD + ATTN_SUM_ROWS, cols), F32)]
        + [pltpu.VMEM(shape, dtype) for shape, dtype in slot_shapes for _ in range(grp)]
        + [pltpu.VMEM((grp * tk, cols), BF16), pltpu.VMEM((1, cols), F32)],
        compiler_params=_cparams(3),
        name="diff_attention" if n_chunks > 1 else "diff_attention_ctx",
    )(*args)


def _dft_cos_sin(n):
    idx = np.outer(np.arange(n), np.arange(n)) % n
    ang = 2.0 * np.pi * idx / n
    return np.cos(ang), np.sin(ang)


def _channel_dft():
    c, s = _dft_cos_sin(FOURIER_GC)
    eye = np.eye(LANES // FOURIER_GC)
    return jnp.asarray(np.kron(eye, c), F32).astype(BF16), jnp.asarray(np.kron(eye, s), F32).astype(BF16)


def _fourier_dense_kernel(x_ref, cl_ref, sl_ref, bdc_ref, bds_ref, o_ref, *, norm):
    x = x_ref[...]
    pr = _mm(cl_ref[...], x)
    pi = -_mm(sl_ref[...], x)
    out = _mm(pr, bdc_ref[...]) + _mm(pi, bds_ref[...])
    o_ref[...] = out * norm


def _fourier_fft_kernel(x_ref, f1_ref, twc_ref, tws_ref, f3_ref, bdc_ref, bds_ref, o_ref, ur_ref, ui_ref, xs_ref,
                        *, n1, norm):
    n2 = LANES
    pitch = n2 + FFT_PITCH_PAD
    nb = FFT_BATCH
    lanes = lambda j: slice(j * LANES, (j + 1) * LANES)

    xpitch = n1 + FFT_PITCH_PAD

    def repitch(b, carry):
        src = pl.multiple_of(b * n1, FFT_PITCH_PAD)
        dst = pl.multiple_of(b * xpitch, FFT_PITCH_PAD)
        xs_ref[pl.ds(dst, n1), :] = x_ref[pl.ds(src, n1), :]
        return carry

    lax.fori_loop(0, n2, repitch, 0, unroll=FFT_BATCH)

    def stage1(ib, carry):
        i0 = ib * nb
        xs = jnp.concatenate([xs_ref[pl.ds(i0 + j, n2, stride=xpitch), :] for j in range(nb)], axis=1)
        t = _mm(f1_ref[...], xs)
        for j in range(nb):
            tr, ti = t[:n2, lanes(j)], t[n2:, lanes(j)]
            r0 = pl.multiple_of((i0 + j) * n2, n2)
            c = twc_ref[pl.ds(r0, n2), :]
            s = tws_ref[pl.ds(r0, n2), :]
            p0 = pl.multiple_of((i0 + j) * pitch, FFT_PITCH_PAD)
            ur_ref[pl.ds(p0, n2), :] = tr * c + ti * s
            ui_ref[pl.ds(p0, n2), :] = ti * c - tr * s
        return carry

    lax.fori_loop(0, n1 // nb, stage1, 0)

    def stage2(kb, carry):
        k0 = kb * nb
        u = jnp.concatenate(
            [jnp.concatenate([ur_ref[pl.ds(k0 + j, n1, stride=pitch), :], ui_ref[pl.ds(k0 + j, n1, stride=pitch), :]],
                             axis=0) for j in range(nb)], axis=1)
        p = _mm(f3_ref[...], u)
        for j in range(nb):
            ur_ref[pl.ds(k0 + j, n1, stride=pitch), :] = p[:n1, lanes(j)]
            ui_ref[pl.ds(k0 + j, n1, stride=pitch), :] = p[n1:, lanes(j)]
        return carry

    lax.fori_loop(0, n2 // nb, stage2, 0)

    def stage3(kb, carry):
        k0 = kb * nb
        blocks = [pl.ds(pl.multiple_of((k0 + j) * pitch, FFT_PITCH_PAD), n2) for j in range(nb)]
        pr = jnp.concatenate([ur_ref[blk, :] for blk in blocks], axis=0)
        pi = jnp.concatenate([ui_ref[blk, :] for blk in blocks], axis=0)
        out = _mm(pr, bdc_ref[...]) + _mm(pi, bds_ref[...])
        o_ref[pl.ds(pl.multiple_of(k0 * n2, n2), nb * n2), :] = out * norm
        return carry

    lax.fori_loop(0, n1 // nb, stage3, 0)


def _fourier_mix(f2d, *, batch):
    rows = f2d.shape[0]
    n = rows // batch
    norm = 1.0 / math.sqrt(n * FOURIER_GC)
    bdc, bds = _channel_dft()
    blk = pl.BlockSpec((n, LANES), lambda b, j: (b, j))
    grid = (batch, FOURIER_W // LANES)
    out_shape = jax.ShapeDtypeStruct((rows, FOURIER_W), F32)
    mat = _const_spec((LANES, LANES))
    if n <= ROW_TILE:
        c, s = _dft_cos_sin(n)
        return pl.pallas_call(
            functools.partial(_fourier_dense_kernel, norm=norm),
            grid=grid,
            in_specs=[blk, _const_spec((n, n)), _const_spec((n, n)), mat, mat],
            out_specs=blk,
            out_shape=out_shape,
            compiler_params=_cparams(2),
            name="fourier_dense",
        )(f2d, jnp.asarray(c, F32).astype(BF16), jnp.asarray(s, F32).astype(BF16), bdc, bds)
    n2 = LANES
    n1 = n // n2
    c2, s2 = _dft_cos_sin(n2)
    f1 = jnp.asarray(np.concatenate([c2, -s2], axis=0), F32).astype(BF16)
    c1, s1 = _dft_cos_sin(n1)
    f3 = jnp.asarray(np.block([[c1, s1], [-s1, c1]]), F32).astype(BF16)
    tw_idx = np.outer(np.arange(n1), np.arange(n2)).reshape(-1)
    tw_ang = 2.0 * np.pi * tw_idx / n
    twc = jnp.broadcast_to(jnp.asarray(np.cos(tw_ang), F32)[:, None], (n, LANES))
    tws = jnp.broadcast_to(jnp.asarray(np.sin(tw_ang), F32)[:, None], (n, LANES))
    return pl.pallas_call(
        functools.partial(_fourier_fft_kernel, n1=n1, norm=norm),
        grid=grid,
        in_specs=[blk, _const_spec((2 * n2, n2)), _const_spec((n, LANES)), _const_spec((n, LANES)),
                  _const_spec((2 * n1, 2 * n1)), mat, mat],
        out_specs=blk,
        out_shape=out_shape,
        scratch_shapes=[pltpu.VMEM((n1 * (n2 + FFT_PITCH_PAD), LANES), F32)] * 2
        + [pltpu.VMEM((n2 * (n1 + FFT_PITCH_PAD), LANES), F32)],
        compiler_params=_cparams(2),
        name="fourier_fft",
    )(f2d, f1, twc, tws, f3, bdc, bds)


def _gelu_tanh(x):
    return 0.5 * x * (1.0 + jnp.tanh(math.sqrt(2.0 / math.pi) * (x + 0.044715 * (x * x * x))))


def _tail_kernel(*refs, glu, wa):
    if glu:
        x_ref, ya_ref, yb_ref, wg_ref, bg_ref, *rest = refs
    else:
        x_ref, ya_ref, yb_ref, *rest = refs
    (wo_ref, bo_ref, g1_ref, lg1_ref, lb1_ref, sh2_ref, sc2_ref, g2_ref,
     w1_ref, b1_ref, w2_ref, b2_ref, lg2_ref, lb2_ref, o_ref) = rest
    part = min(TAIL_PART_ROWS, x_ref.shape[0])
    n_parts = x_ref.shape[0] // part
    rows = lambda r: slice(r * part, (r + 1) * part)

    def mix_out(r):
        ya = ya_ref[rows(r), :]
        if glu:
            g = _gelu_tanh(ya)
            ya = g * jax.nn.sigmoid(jnp.dot(g.astype(BF16), wg_ref[...], preferred_element_type=F32) + bg_ref[...])
        y = (jnp.dot(ya.astype(BF16), wo_ref[:wa, :], preferred_element_type=F32)
             + jnp.dot(yb_ref[rows(r), :].astype(BF16), wo_ref[wa:, :], preferred_element_type=F32) + bo_ref[...])
        x1 = _ln(ALPHA * x_ref[rows(r), :] + g1_ref[...] * y) * lg1_ref[...] + lb1_ref[...]
        return x1, (_ln(x1) * (1.0 + sc2_ref[...]) + sh2_ref[...]).astype(BF16)

    def ffn_chunk(h, c):
        cols = slice(c * FFN_CHUNK, (c + 1) * FFN_CHUNK)
        a = jnp.maximum(jnp.dot(h, w1_ref[:, cols], preferred_element_type=F32) + b1_ref[:, cols], 0.0)
        return jnp.dot((a * a).astype(BF16), w2_ref[cols, :], preferred_element_type=F32)

    def finish(r, x1, y2):
        o_ref[rows(r), :] = _ln(ALPHA * x1 + g2_ref[...] * (y2 + b2_ref[...])) * lg2_ref[...] + lb2_ref[...]

    state = mix_out(0)
    done = None
    for r in range(n_parts):
        x1, h = state
        y2 = ffn_chunk(h, 0)
        if r + 1 < n_parts:
            state = mix_out(r + 1)
        if done is not None:
            finish(*done)
        for c in range(1, FFN_W // FFN_CHUNK):
            y2 = y2 + ffn_chunk(h, c)
        done = (r, x1, y2)
    finish(*done)


def _layer_tail(x2d, ya, yb, modl, glu_params, w_out, b_out, lg1, lb1, w1, b1, w2, b2, lg2, lb2, *, layer, tm,
                row_fn):
    rows = x2d.shape[0]
    wa, wb = ya.shape[-1], yb.shape[1]
    glu = glu_params is not None
    row = lambda v: v.reshape(1, -1)
    vec = _const_spec((1, D_MODEL))
    if ya.ndim == 3:
        tiles_per_batch = rows // ya.shape[0] // tm
        ya_spec = pl.BlockSpec((None, tm, wa), lambda i: (i // tiles_per_batch, i % tiles_per_batch, 0))
    else:
        ya_spec = pl.BlockSpec((tm, wa), lambda i: (i, 0))
    in_specs = [pl.BlockSpec((tm, D_MODEL), lambda i: (i, 0)), ya_spec,
                pl.BlockSpec((tm, wb), lambda i: (i, 0))]
    args = [x2d, ya, yb]
    if glu:
        in_specs += [_const_spec((wa, wa)), _const_spec((1, wa))]
        args += [glu_params[0], row(glu_params[1])]
    slab = lambda shape: pl.BlockSpec((None,) + shape, lambda i: (layer, 0, 0), pipeline_mode=pl.Buffered(1))
    in_specs += [slab((D_MODEL, D_MODEL)), vec, _mod_spec(row_fn, 2), vec, vec,
                 _mod_spec(row_fn, 3), _mod_spec(row_fn, 4), _mod_spec(row_fn, 5),
                 slab((D_MODEL, FFN_W)), _const_spec((1, FFN_W)), slab((FFN_W, D_MODEL)), vec, vec, vec]
    args += [w_out, row(b_out), modl, row(lg1), row(lb1), modl, modl, modl,
             w1, row(b1), w2, row(b2), row(lg2), row(lb2)]
    return pl.pallas_call(
        functools.partial(_tail_kernel, glu=glu, wa=wa),
        grid=(rows // tm,),
        in_specs=in_specs,
        out_specs=pl.BlockSpec((tm, D_MODEL), lambda i: (i, 0)),
        out_shape=jax.ShapeDtypeStruct((rows, D_MODEL), F32),
        compiler_params=_cparams(1),
        name="layer_tail_glu" if glu else "layer_tail",
    )(*args)


def _inproj_odd_kernel(x_ref, sh_ref, sc_ref, w_ref, wsp_ref, bsp_ref, s_ref, gm_ref):
    part = min(ODD_PART_ROWS, x_ref.shape[0])
    n_parts = x_ref.shape[0] // part
    n_ch = part // CHUNK
    rows = lambda r: slice(r * part, (r + 1) * part)
    norm = lambda r: (_ln(x_ref[rows(r), :]) * (1.0 + sc_ref[...]) + sh_ref[...]).astype(BF16)

    def gate(r, u, v):
        chunk = lambda ch: slice(ch * CHUNK, (ch + 1) * CHUNK)
        for g in range(GMLP_GROUPS):
            cols = slice(g * GMLP_GC, (g + 1) * GMLP_GC)
            vn = jnp.concatenate([_ln(v[chunk(ch), cols]).astype(BF16) for ch in range(n_ch)], axis=1)
            sp = jnp.dot(wsp_ref[g], vn, preferred_element_type=F32)
            for ch in range(n_ch):
                out_rows = slice(r * part + ch * CHUNK, r * part + (ch + 1) * CHUNK)
                gm_ref[out_rows, cols] = u[chunk(ch), cols] * (sp[:, ch * GMLP_GC:(ch + 1) * GMLP_GC] + bsp_ref[g])

    h_next = norm(0)
    pending = None
    for r in range(n_parts):
        h = h_next
        s_ref[rows(r), :] = jnp.dot(h, w_ref[:, :S5_W], preferred_element_type=F32)
        if pending is not None:
            gate(*pending)
        u = jnp.dot(h, w_ref[:, S5_W:S5_W + GMLP_W], preferred_element_type=F32)
        if r + 1 < n_parts:
            h_next = norm(r + 1)
        v = jnp.dot(h, w_ref[:, S5_W + GMLP_W:], preferred_element_type=F32)
        pending = (r, u, v)
    gate(*pending)


def _inproj_odd(x2d, modl, w_bf, wsp_bf, bsp_lanes, *, batch, n_rows, tm, row_fn):
    rows = x2d.shape[0]
    tiles_per_batch = rows // batch // tm
    return pl.pallas_call(
        _inproj_odd_kernel,
        grid=(rows // tm,),
        in_specs=[pl.BlockSpec((tm, D_MODEL), lambda i: (i, 0)), _mod_spec(row_fn, 0), _mod_spec(row_fn, 1),
                  _const_spec((D_MODEL, ODD_IN_W)), _const_spec((GMLP_GROUPS, CHUNK, CHUNK)),
                  _const_spec((GMLP_GROUPS, CHUNK, GMLP_GC))],
        out_specs=[pl.BlockSpec((None, tm, S5_W), lambda i: (i // tiles_per_batch, i % tiles_per_batch, 0)),
                   pl.BlockSpec((tm, GMLP_W), lambda i: (i, 0))],
        out_shape=[jax.ShapeDtypeStruct((batch, n_rows, S5_W), F32), jax.ShapeDtypeStruct((rows, GMLP_W), F32)],
        compiler_params=_cparams(1),
        name="inproj_odd_gmlp",
    )(x2d, modl, modl, w_bf, wsp_bf, bsp_lanes)


def _ln_mod_matmul_kernel(x_ref, sh_ref, sc_ref, w_ref, _, o_ref):
    h = (_ln(x_ref[...]) * (1.0 + sc_ref[...]) + sh_ref[...]).astype(BF16)
    o_ref[...] = jnp.dot(h, w_ref[...], preferred_element_type=F32)


def _ln_mod_matmul_into(x2d, modl, w_bf, dst, *, batch, row0, tm, row_fn):
    rows, n = x2d.shape[0], w_bf.shape[1]
    tiles_per_batch = rows // batch // tm
    blk0 = row0 // tm
    return pl.pallas_call(
        _ln_mod_matmul_kernel,
        grid=(rows // tm,),
        in_specs=[pl.BlockSpec((tm, D_MODEL), lambda i: (i, 0)), _mod_spec(row_fn, 0), _mod_spec(row_fn, 1),
                  _const_spec((D_MODEL, n)), pl.BlockSpec(memory_space=pl.ANY)],
        out_specs=pl.BlockSpec((None, tm, n), lambda i: (i // tiles_per_batch, blk0 + i % tiles_per_batch, 0)),
        out_shape=jax.ShapeDtypeStruct(dst.shape, dst.dtype),
        input_output_aliases={4: 0},
        compiler_params=_cparams(1),
        name="ln_mod_matmul",
    )(x2d, modl, modl, w_bf, dst)


def _s5_prep_kernel(pw_ref, bb_ref, cc_ref, dw_ref, w_ref, vt_ref, m_ref):
    t_len, gc = S5_T, S5_GC
    width = t_len * gc
    nt_dims = (((1,), (1,)), ((), ()))
    cols = lambda q: slice(q * LANES, (q + 1) * LANES)
    lane = lax.broadcasted_iota(jnp.int32, (gc, width), 1)
    for g in range(2):
        wide = []
        for d in range(2):
            pr, pi = pw_ref[g, d, 0], pw_ref[g, d, 1]
            bbr, bbi = bb_ref[g, d, 0], bb_ref[g, d, 1]
            cr, ci = cc_ref[g, d, 0], cc_ref[g, d, 1]

            def power(xr, xi, t, pr=pr, pi=pi):
                return pr[t:t + 1] * xr - pi[t:t + 1] * xi, pr[t:t + 1] * xi + pi[t:t + 1] * xr

            lc = [power(cr, ci, t) for t in range(t_len + 1)]
            for s in range(t_len):
                blk = slice(g * width + s * gc, g * width + (s + 1) * gc)
                lbr, lbi = power(bbr, bbi, t_len - 1 - s if d == 0 else s)
                w_ref[blk, cols(2 * d)] = lbr.astype(BF16)
                w_ref[blk, cols(2 * d + 1)] = lbi.astype(BF16)
                lcr, lci = lc[s + 1 if d == 0 else t_len - s]
                vt_ref[blk, cols(2 * d)] = lcr.astype(BF16)
                vt_ref[blk, cols(2 * d + 1)] = (-lci).astype(BF16)
            order = range(t_len) if d == 0 else range(t_len - 1, -1, -1)
            lcr_all = jnp.concatenate([lc[t][0] for t in order], axis=0)
            lci_all = jnp.concatenate([lc[t][1] for t in order], axis=0)
            wide.append(lax.dot_general(bbr, lcr_all, nt_dims, preferred_element_type=F32, precision=HIGHEST)
                        - lax.dot_general(bbi, lci_all, nt_dims, preferred_element_type=F32, precision=HIGHEST))
        kf = wide[0] + dw_ref[g]
        kb = wide[1]
        for s in range(t_len):
            fwd = jnp.where(lane >= s * gc, pltpu.roll(kf, s * gc, 1), 0.0)
            bwd = jnp.where(lane < (s + 1) * gc, pltpu.roll(kb, (s + 1) * gc % width, 1), 0.0)
            m_ref[g, s * gc:(s + 1) * gc, :] = (fwd + bwd).astype(BF16)


def _s5_matrices(lam_re, lam_im, log_dt, b_re, b_im, c_re, c_im, d_skip):
    t_len, g_n, p_n, i_n = S5_T, S5_GROUPS, S5_STATE, S5_GC
    tau_rows = -(-(t_len + 1) // 8) * 8
    dt = jnp.exp(log_dt)[..., None]
    mag, th = lam_re * dt, lam_im * dt
    er = jnp.exp(mag)
    lbr, lbi = er * jnp.cos(th), er * jnp.sin(th)
    den = lam_re * lam_re + lam_im * lam_im
    nr, ni = lbr - 1.0, lbi
    cr = (nr * lam_re + ni * lam_im) / den
    ci = (ni * lam_re - nr * lam_im) / den
    tau = jnp.arange(tau_rows, dtype=F32)[:, None]
    pm = jnp.exp(tau * mag[:, :, None, :])
    ang = tau * th[:, :, None, :]
    pw = jnp.stack([pm * jnp.cos(ang), pm * jnp.sin(ang)], axis=2)
    to_ip = lambda a: jnp.swapaxes(a, -1, -2)
    bb = jnp.stack([cr[..., None, :] * to_ip(b_re) - ci[..., None, :] * to_ip(b_im),
                    cr[..., None, :] * to_ip(b_im) + ci[..., None, :] * to_ip(b_re)], axis=2)
    cc = jnp.stack([c_re, c_im], axis=2)

    def per_group(a):
        a = jnp.swapaxes(a, 0, 1)
        a = a.reshape((g_n // 2, 2) + a.shape[1:])
        zero = jnp.zeros_like(a[:, 0])
        both = jnp.stack([jnp.concatenate([a[:, 0], zero], axis=-1), jnp.concatenate([zero, a[:, 1]], axis=-1)], axis=1)
        return both.reshape((g_n,) + both.shape[2:])

    width = t_len * i_n
    d_wide = jnp.pad(d_skip.reshape(g_n, i_n)[:, None, :] * jnp.asarray(np.eye(i_n), F32),
                     ((0, 0), (0, 0), (0, width - i_n)))
    pair = lambda shape: pl.BlockSpec((2,) + shape, lambda j: (j,) + (0,) * len(shape))
    state_spec = pl.BlockSpec((None, 2 * width, 4 * LANES), lambda j: (j, 0, 0))
    state_shape = jax.ShapeDtypeStruct((S5_PAIRS, 2 * width, 4 * LANES), BF16)
    w_state, vt_state, m_pair = pl.pallas_call(
        _s5_prep_kernel,
        grid=(S5_PAIRS,),
        in_specs=[pair((2, 2, tau_rows, 2 * p_n)), pair((2, 2, i_n, 2 * p_n)), pair((2, 2, i_n, 2 * p_n)),
                  pair((i_n, width))],
        out_specs=[state_spec, state_spec, pl.BlockSpec((None, 2, width, width), lambda j: (j, 0, 0, 0))],
        out_shape=[state_shape, state_shape, jax.ShapeDtypeStruct((S5_PAIRS, 2, width, width), BF16)],
        compiler_params=_cparams(1),
        name="s5_prep",
    )(per_group(pw), per_group(bb), per_group(cc), d_wide)
    a_pow = jnp.stack([pw[0, :, 0, t_len], pw[0, :, 1, t_len], pw[1, :, 0, t_len], pw[1, :, 1, t_len]], axis=0)
    return m_pair, w_state, vt_state, a_pow.reshape(4, g_n * p_n)


def _s5_fold_perm():
    n_t, n_g = LANES // S5_GC, LANES // S5_GC
    src = np.arange(n_t * LANES)
    t_lo, g, i = src // LANES, (src % LANES) // S5_GC, src % S5_GC
    perm = np.zeros((n_t * LANES, n_g * LANES), np.float32)
    perm[src, g * LANES + t_lo * S5_GC + i] = 1.0
    return perm


def _s5_kernel(s_ref, pb_ref, pbt_ref, m_ref, w_ref, v_ref, a_ref, y_ref, u_scr, st_scr, yp_scr, *, n_lat, n_ctx):
    n_all = n_lat + n_ctx
    n_t = LANES // S5_GC
    width = S5_T * S5_GC
    blk = lambda j: slice(j * LANES, (j + 1) * LANES)
    for t_hi in range(S5_T // n_t):
        x = jnp.concatenate([s_ref[pl.ds(n_t * t_hi + t_lo, n_all, stride=S5_T), :].astype(BF16)
                             for t_lo in range(n_t)], axis=1)
        xp = jnp.dot(x, pb_ref[...], preferred_element_type=F32).astype(BF16)
        for g in range(n_t):
            u_scr[g, :, blk(t_hi)] = xp[:, blk(g)]

    n_pairs = w_ref.shape[0]
    for pr in range(n_pairs):
        u_pair = jnp.concatenate([u_scr[2 * pr], u_scr[2 * pr + 1]], axis=1)
        st = jnp.dot(u_pair, w_ref[pr], preferred_element_type=F32)
        for q in range(4):
            st_scr[q, :, blk(pr)] = st[:, blk(q)]

    afr, afi, abr, abi = (a_ref[pl.ds(q, 1), :] for q in range(4))

    def scan_step(k, carry):
        hr, hi, gr, gi = carry
        rf = pl.ds(jnp.where(k < n_ctx, k + n_lat, k - n_ctx), 1)
        rb = pl.ds(n_all - 1 - k, 1)
        sfr, sfi, sbr, sbi = st_scr[0, rf, :], st_scr[1, rf, :], st_scr[2, rb, :], st_scr[3, rb, :]
        st_scr[0, rf, :] = hr
        st_scr[1, rf, :] = hi
        st_scr[2, rb, :] = gr
        st_scr[3, rb, :] = gi
        return (afr * hr - afi * hi + sfr, afr * hi + afi * hr + sfi,
                abr * gr - abi * gi + sbr, abr * gi + abi * gr + sbi)

    lax.fori_loop(0, n_all, scan_step, (jnp.zeros_like(afr),) * 4, unroll=S5_SCAN_UNROLL)

    for pr in range(n_pairs):
        hst = jnp.concatenate([st_scr[q, :, blk(pr)].astype(BF16) for q in range(4)], axis=1)
        y_pair = lax.dot_general(hst, v_ref[pr], (((1,), (1,)), ((), ())), preferred_element_type=F32)
        for g in range(2):
            y_g = (y_pair[:, g * width:(g + 1) * width]
                   + jnp.dot(u_scr[2 * pr + g], m_ref[pr, g], preferred_element_type=F32))
            for t_hi in range(S5_T // n_t):
                yp_scr[t_hi, :, blk(2 * pr + g)] = y_g[:, blk(t_hi)].astype(BF16)

    for t_hi in range(S5_T // n_t):
        z = jnp.dot(yp_scr[t_hi], pbt_ref[...], preferred_element_type=F32)
        for t_lo in range(n_t):
            y_ref[pl.ds(n_t * t_hi + t_lo, n_all, stride=S5_T), :] = z[:, blk(t_lo)]


def _s5_mix(s_all, mats, *, n_lat_rows):
    m_pair, w_state, v_state, a_pow = mats
    batch, n_rows, _ = s_all.shape
    n_lat = n_lat_rows // S5_T
    n_all = n_rows // S5_T
    width = S5_T * S5_GC
    n_blocks = S5_W // LANES
    pairs = S5_PAIRS // n_blocks
    perm = _s5_fold_perm()
    rows_spec = pl.BlockSpec((None, n_rows, LANES), lambda b, v: (b, 0, v))
    return pl.pallas_call(
        functools.partial(_s5_kernel, n_lat=n_lat, n_ctx=n_all - n_lat),
        grid=(batch, n_blocks),
        in_specs=[rows_spec, _const_spec(perm.shape), _const_spec(perm.shape),
                  pl.BlockSpec((pairs, 2, width, width), lambda b, v: (v, 0, 0, 0)),
                  pl.BlockSpec((pairs, 2 * width, 4 * LANES), lambda b, v: (v, 0, 0)),
                  pl.BlockSpec((pairs, 4 * LANES, 2 * width), lambda b, v: (v, 0, 0)),
                  pl.BlockSpec((4, pairs * LANES), lambda b, v: (0, v))],
        out_specs=rows_spec,
        out_shape=jax.ShapeDtypeStruct(s_all.shape, F32),
        scratch_shapes=[pltpu.VMEM((2 * pairs, n_all, width), BF16), pltpu.VMEM((4, n_all, pairs * LANES), F32),
                        pltpu.VMEM((S5_T * S5_GC // LANES, n_all, 2 * pairs * LANES), BF16)],
        compiler_params=_cparams(2),
        name="s5_mix",
    )(s_all, jnp.asarray(perm, BF16), jnp.asarray(perm.T, BF16), m_pair, w_state, v_state, a_pow)


def kernel(x, c, ctx, c_ctx, w_mod, b_mod, w_out, b_out, ln_mix_g, ln_mix_b, w_ffn1, b_ffn1, w_ffn2,
           b_ffn2, ln_ffn_g, ln_ffn_b, w_in_ab, lam_q1, lam_k1, lam_q2, lam_k2, subln_g, w_in_cd,
           s5_lam_re, s5_lam_im, s5_log_dt, s5_b_re, s5_b_im, s5_c_re, s5_c_im, s5_d, w_glu, b_glu,
           w_sp, b_sp):
    batch, n_lat, d = x.shape
    n_ctx = ctx.shape[1]
    assert d == D_MODEL and n_lat % TAIL_TILE == 0 and n_ctx % CHUNK == 0 and batch + 1 <= MOD_ROWS
    mod = _modulation(c, c_ctx, w_mod, b_mod)
    xl = x.reshape(batch * n_lat, d)
    xc = ctx.reshape(batch * n_ctx, d)
    tiles = n_lat // ROW_TILE
    lat_rows = dict(tm=ROW_TILE, row_fn=lambda i: i // tiles)
    tail_tiles = n_lat // TAIL_TILE
    lat_tail_rows = dict(tm=TAIL_TILE, row_fn=lambda i: i // tail_tiles)
    ctx_rows = dict(tm=n_ctx, row_fn=lambda i: batch)
    rope_tabs = _rope_tables(n_lat)
    w_out_bf, w_ffn1_bf, w_ffn2_bf = w_out.astype(BF16), w_ffn1.astype(BF16), w_ffn2.astype(BF16)
    for l in range(DEPTH):
        last = l == DEPTH - 1
        e = l // 2
        modl = mod[l]
        tail_w = (w_out_bf, b_out[l], ln_mix_g[l], ln_mix_b[l], w_ffn1_bf, b_ffn1[l],
                  w_ffn2_bf, b_ffn2[l], ln_ffn_g[l], ln_ffn_b[l])
        if l % 2 == 0:
            w_in = w_in_ab[e].astype(BF16)
            lam_init = 0.8 - 0.6 * math.exp(-0.3 * l)
            lam_vecs = (lam_q1[e], lam_k1[e], lam_q2[e], lam_k2[e])
            n_keys = n_lat + n_ctx
            f, q, k_all, vt_all = _inproj_even(xl, modl, w_in, rope_tabs, None, batch=batch, n_keys=n_keys,
                                               key_row0=0, **lat_rows)
            fc, qc, k_all, vt_all = _inproj_even(xc, modl, w_in, None, (k_all, vt_all), batch=batch, n_keys=n_keys,
                                                 key_row0=n_lat, **ctx_rows)
            attend = functools.partial(_diff_attention, lam_vecs=lam_vecs, subln_g=subln_g[e], lam_init=lam_init,
                                       batch=batch)
            ya = attend(q, k_all, vt_all, tq=min(ATTN_TQ, n_lat), key_row0=0, n_keys=n_keys)
            yf = _fourier_mix(f, batch=batch)
            xl = _layer_tail(xl, yf, ya, modl, None, *tail_w, layer=l, **lat_tail_rows)
            if not last:
                yac = attend(qc, k_all, vt_all, tq=n_ctx, key_row0=n_lat, n_keys=n_ctx)
                yfc = _fourier_mix(fc, batch=batch)
                xc = _layer_tail(xc, yfc, yac, modl, None, *tail_w, layer=l, tm=batch * n_ctx, row_fn=lambda i: batch)
        else:
            assert last, "the S5 / gMLP layer is only implemented as the final layer"
            w_in = w_in_cd[e].astype(BF16)
            bsp_lanes = jnp.broadcast_to(b_sp[e][:, :, None], (GMLP_GROUPS, CHUNK, GMLP_GC))
            s_all, gm = _inproj_odd(xl, modl, w_in, w_sp[e].astype(BF16), bsp_lanes, batch=batch,
                                    n_rows=n_lat + n_ctx, **lat_rows)
            s_all = _ln_mod_matmul_into(xc, modl, w_in[:, :S5_W], s_all, batch=batch, row0=n_lat, **ctx_rows)
            mats = _s5_matrices(s5_lam_re[e], s5_lam_im[e], s5_log_dt[e], s5_b_re[e], s5_b_im[e], s5_c_re[e],
                                s5_c_im[e], s5_d[e])
            ys = _s5_mix(s_all, mats, n_lat_rows=n_lat)
            xl = _layer_tail(xl, ys, gm, modl, (w_glu[e].astype(BF16), b_glu[e]), *tail_w, layer=l, **lat_tail_rows)
    return xl.reshape(batch, n_lat, d)
```

```python
import functools
import math

import numpy as np
import jax
import jax.numpy as jnp
from jax import lax
from jax.experimental import pallas as pl
from jax.experimental.pallas import tpu as pltpu

D_MODEL = 1024
DEPTH = 2
GRID_W = 64
FOURIER_W = D_MODEL // 4
FOURIER_GC = 64
DIFF_HD = 64
DIFF_VD = 2 * DIFF_HD
DIFF_HEADS = (D_MODEL - FOURIER_W) // DIFF_VD
QK_W = DIFF_HEADS * 2 * DIFF_HD
V_W = DIFF_HEADS * DIFF_VD
EVEN_IN_W = FOURIER_W + 2 * QK_W + V_W
DIFF_SCALE = DIFF_HD ** -0.5
ROPE_BASE = 10000.0
ROPE_FREQS = DIFF_HD // 4
S5_W = D_MODEL // 2
S5_GC = 16
S5_GROUPS = S5_W // S5_GC
S5_STATE = 64
GMLP_W = D_MODEL // 2
GMLP_GC = 128
GMLP_GROUPS = GMLP_W // GMLP_GC
CHUNK = 128
ODD_IN_W = S5_W + 2 * GMLP_W
FFN_W = 4 * D_MODEL
LN_EPS = 1e-5
ALPHA = (2 * DEPTH) ** 0.25

F32 = jnp.float32
BF16 = jnp.bfloat16
HIGHEST = lax.Precision.HIGHEST

LANES = 128
MOD_ROWS = 8
ROW_TILE = 1024
ATTN_TQ = 2048
ATTN_TK = 256
ATTN_GROUP = 3
ATTN_SUM_ROWS = 16
FFT_BATCH = 8
FFT_PITCH_PAD = 8
S5_T = 16
S5_PAIRS = S5_GROUPS // 2
S5_SCAN_UNROLL = 4
FFN_CHUNK = 1024
ODD_PART_ROWS = 256
TAIL_TILE = 1024
TAIL_PART_ROWS = 256
VMEM_LIMIT = 56 * 2 ** 20
Q_SCALE = DIFF_SCALE * math.log2(math.e)


def _cparams(n_axes, vmem=VMEM_LIMIT):
    return pltpu.CompilerParams(dimension_semantics=("arbitrary",) * n_axes, vmem_limit_bytes=vmem)


def _const_spec(shape):
    zeros = (0,) * len(shape)
    return pl.BlockSpec(shape, lambda *_: zeros, pipeline_mode=pl.Buffered(1))


def _ln(x):
    xc = x - jnp.mean(x, -1, keepdims=True)
    var = jnp.mean(xc * xc, -1, keepdims=True)
    return xc * lax.rsqrt(var + LN_EPS)


def _mm(a, b):
    return jnp.dot(a.astype(BF16), b.astype(BF16), preferred_element_type=F32)


def _mod_spec(row_fn, which):
    return pl.BlockSpec((None, 1, D_MODEL), lambda i: (row_fn(i) * 6 + which, 0, 0))


def _mod_kernel(c_ref, w_ref, b_ref, o_ref):
    c = c_ref[...]
    a = c * jax.nn.sigmoid(c)
    w = w_ref[...]
    a_hi, w_hi = a.astype(BF16), w.astype(BF16)
    a_lo = (a - a_hi.astype(F32)).astype(BF16)
    w_lo = (w - w_hi.astype(F32)).astype(BF16)
    on_hi = jnp.dot(jnp.concatenate([a_hi, a_lo], axis=0), w_hi, preferred_element_type=F32)
    o_ref[...] = (on_hi[:MOD_ROWS] + on_hi[MOD_ROWS:] + jnp.dot(a_hi, w_lo, preferred_element_type=F32)
                  + b_ref[...])


def _modulation(c, c_ctx, w_mod, b_mod):
    batch = c.shape[0]
    rows = jnp.concatenate([c, c_ctx[None], jnp.zeros((MOD_ROWS - batch - 1, D_MODEL), F32)], axis=0)
    out = pl.pallas_call(
        _mod_kernel,
        grid=(DEPTH, 6),
        in_specs=[
            pl.BlockSpec((MOD_ROWS, D_MODEL), lambda l, j: (0, 0)),
            pl.BlockSpec((None, D_MODEL, D_MODEL), lambda l, j: (l, 0, j)),
            pl.BlockSpec((None, 1, D_MODEL), lambda l, j: (l, 0, j)),
        ],
        out_specs=pl.BlockSpec((None, MOD_ROWS, D_MODEL), lambda l, j: (l, 0, j)),
        out_shape=jax.ShapeDtypeStruct((DEPTH, MOD_ROWS, 6 * D_MODEL), F32),
        compiler_params=_cparams(2),
        name="modulation",
    )(rows, w_mod, b_mod.reshape(DEPTH, 1, 6 * D_MODEL))
    return out.reshape(DEPTH, MOD_ROWS * 6, 1, D_MODEL)


def _inproj_even_kernel(*refs, rope):
    if rope:
        x_ref, sh_ref, sc_ref, w_ref, cos_ref, sin_ref, f_ref, q_ref, k_ref, vt_ref = refs
    else:
        x_ref, sh_ref, sc_ref, w_ref, _, _, f_ref, q_ref, k_ref, vt_ref = refs
    width = 2 * LANES
    part = vt_ref.shape[2]
    rows = lambda r: slice(r * part, (r + 1) * part)
    norm = lambda r: (_ln(x_ref[rows(r), :]) * (1.0 + sc_ref[...]) + sh_ref[...]).astype(BF16)
    if rope:
        lane = lax.broadcasted_iota(jnp.int32, (part, width), 1)
        first_half = (lane & ROPE_FREQS) == 0
    v0 = FOURIER_W + 2 * QK_W
    n_parts = vt_ref.shape[0]
    h_next = norm(0)
    for r in range(n_parts):
        h = h_next
        f_ref[rows(r), :] = jnp.dot(h, w_ref[:, :FOURIER_W], preferred_element_type=F32)
        if r + 1 < n_parts:
            h_next = norm(r + 1)
        if rope:
            cos = jnp.concatenate([cos_ref[rows(r), :]] * (width // LANES), axis=1)
            sin = jnp.concatenate([sin_ref[rows(r), :]] * (width // LANES), axis=1)
        for j in range(QK_W // width):
            for base, o_ref, scale in ((FOURIER_W, q_ref, Q_SCALE), (FOURIER_W + QK_W, k_ref, None)):
                t = jnp.dot(h, w_ref[:, base + j * width:base + (j + 1) * width], preferred_element_type=F32)
                if rope:
                    partner = jnp.where(first_half, pltpu.roll(t, width - ROPE_FREQS, 1),
                                        pltpu.roll(t, ROPE_FREQS, 1))
                    t = t * cos + partner * sin
                if scale is not None:
                    t = t * scale
                o_ref[rows(r), j * width:(j + 1) * width] = t.astype(BF16)
        for j in range(V_W // width):
            v = jnp.dot(h, w_ref[:, v0 + j * width:v0 + (j + 1) * width], preferred_element_type=F32)
            vt_ref[r, j * width:(j + 1) * width, :] = v.T.astype(BF16)


def _inproj_even(x2d, modl, w_bf, rope_tabs, kv_all, *, batch, n_keys, key_row0, tm, row_fn):
    rows = x2d.shape[0]
    rope = rope_tabs is not None
    tiles_per_batch = rows // batch // tm
    blk0 = key_row0 // tm
    in_specs = [
        pl.BlockSpec((tm, D_MODEL), lambda i: (i, 0)),
        _mod_spec(row_fn, 0),
        _mod_spec(row_fn, 1),
        _const_spec((D_MODEL, EVEN_IN_W)),
    ]
    args = [x2d, modl, modl, w_bf]
    if rope:
        in_specs += [pl.BlockSpec((tm, LANES), lambda i: (i % tiles_per_batch, 0))] * 2
        args += list(rope_tabs)
        aliases = {}
    else:
        in_specs += [pl.BlockSpec(memory_space=pl.ANY)] * 2
        args += list(kv_all)
        aliases = {4: 2, 5: 3}
    kv_idx = lambda i: (i // tiles_per_batch, blk0 + i % tiles_per_batch, 0)
    return pl.pallas_call(
        functools.partial(_inproj_even_kernel, rope=rope),
        grid=(rows // tm,),
        in_specs=in_specs,
        out_specs=[
            pl.BlockSpec((tm, FOURIER_W), lambda i: (i, 0)),
            pl.BlockSpec((tm, QK_W), lambda i: (i, 0)),
            pl.BlockSpec((None, tm, QK_W), kv_idx),
            pl.BlockSpec((None, tm // ATTN_TK, V_W, ATTN_TK), lambda i: kv_idx(i) + (0,)),
        ],
        out_shape=[
            jax.ShapeDtypeStruct((rows, FOURIER_W), F32),
            jax.ShapeDtypeStruct((rows, QK_W), BF16),
            jax.ShapeDtypeStruct((batch, n_keys, QK_W), BF16),
            jax.ShapeDtypeStruct((batch, n_keys // ATTN_TK, V_W, ATTN_TK), BF16),
        ],
        input_output_aliases=aliases,
        compiler_params=_cparams(1),
        name="inproj_even_rope" if rope else "inproj_even",
    )(*args)


def _rope_tables(n):
    rows = n // GRID_W
    row = jnp.repeat(jnp.arange(rows, dtype=F32), GRID_W)
    col = jnp.tile(jnp.arange(GRID_W, dtype=F32), rows)
    inv = jnp.power(ROPE_BASE, -jnp.arange(ROPE_FREQS, dtype=F32) / ROPE_FREQS)
    ang = jnp.stack([row[:, None] * inv, col[:, None] * inv], axis=1)
    cos, sin = jnp.cos(ang), jnp.sin(ang)
    cos_map = jnp.concatenate([cos, cos], axis=-1).reshape(n, 4 * ROPE_FREQS)
    sin_map = jnp.concatenate([-sin, sin], axis=-1).reshape(n, 4 * ROPE_FREQS)
    reps = LANES // (4 * ROPE_FREQS)
    return jnp.tile(cos_map, (1, reps)), jnp.tile(sin_map, (1, reps))


def _attn_kernel(lq1, lk1, lq2, lk2, g_ref, q_ref, k_ref, vt_ref, o_ref, acc_ref, *scratch, lam_init):
    tq = q_ref.shape[0]
    n_chunks, _, tk = vt_ref.shape
    q = q_ref[...]
    lane = lax.broadcasted_iota(jnp.int32, q.shape, 1)
    zero = jnp.zeros_like(q)
    q_cat = jnp.concatenate([jnp.where(lane < DIFF_HD, q, zero), jnp.where(lane >= DIFF_HD, q, zero)], axis=0)
    nt_dims = (((1,), (1,)), ((), ()))

    p_ref, al_ref = scratch[-2:]
    grp = (len(scratch) - 2) // 2
    s_refs, mc_refs = scratch[:grp], scratch[grp:2 * grp]
    n_groups = n_chunks // grp

    def scores(g, r):
        kb = k_ref[pl.ds(pl.multiple_of((g * grp + r) * tk, tk), tk), :]
        s = lax.dot_general(kb, q_cat, nt_dims, preferred_element_type=F32)
        s_refs[r][...] = s
        mc_refs[r][...] = jnp.max(s, axis=0, keepdims=True)

    def group_max(m_prev):
        m = m_prev
        for r in range(grp):
            m = jnp.maximum(m, mc_refs[r][...])
        al_ref[...] = jnp.exp2(m_prev - m)
        return m

    def probs(r, m):
        p_ref[r * tk:(r + 1) * tk, :] = jnp.exp2(s_refs[r][...] - m).astype(BF16)

    ones_rows = (lax.broadcasted_iota(jnp.int32, (ATTN_SUM_ROWS, grp * tk), 0) == 0).astype(BF16)

    def values(g):
        vt = jnp.concatenate([vt_ref[g * grp + r] for r in range(grp)], axis=1)
        vt1 = jnp.concatenate([vt, ones_rows], axis=0)
        acc_ref[...] = acc_ref[...] * al_ref[...] + jnp.dot(vt1, p_ref[...], preferred_element_type=F32)

    acc_ref[...] = jnp.zeros_like(acc_ref)
    m = jnp.full((1, 2 * tq), -jnp.inf, F32)
    for r in range(grp):
        scores(0, r)
    m = group_max(m)
    for r in range(grp):
        probs(r, m)
        if n_groups > 1:
            scores(1, r)

    def body(g, m):
        values(g - 1)
        m = group_max(m)
        for r in range(grp):
            probs(r, m)
            scores(g + 1, r)
        return m

    if n_groups > 2:
        m = lax.fori_loop(1, n_groups - 1, body, m)
    if n_groups > 1:
        values(n_groups - 2)
        m = group_max(m)
        for r in range(grp):
            probs(r, m)
    values(n_groups - 1)

    lam = (jnp.exp(jnp.sum(lq1[...] * lk1[...], keepdims=True))
           - jnp.exp(jnp.sum(lq2[...] * lk2[...], keepdims=True)) + lam_init)
    on = acc_ref[:DIFF_VD, :] / acc_ref[DIFF_VD:DIFF_VD + 1, :]
    o = on[:, :tq] - lam * on[:, tq:]
    o = o * lax.rsqrt(jnp.mean(o * o, axis=0, keepdims=True) + LN_EPS)
    o_ref[...] = o.T * (g_ref[...] * (1.0 - lam_init))


def _diff_attention(q, k_all, vt_all, lam_vecs, subln_g, lam_init, *, batch, tq, key_row0, n_keys):
    rows = q.shape[0]
    nq = rows // batch // tq
    tk = vt_all.shape[3]
    n_chunks = n_keys // tk
    assert key_row0 % n_keys == 0
    kblk = key_row0 // n_keys
    grp = ATTN_GROUP if n_chunks % ATTN_GROUP == 0 else 1
    cols = 2 * tq
    slot_shapes = [((tk, cols), F32), ((1, cols), F32)]
    vec_spec = pl.BlockSpec((1, DIFF_HD), lambda b, h, i: (0, 0))
    in_specs = [vec_spec] * 4 + [
        pl.BlockSpec((1, DIFF_VD), lambda b, h, i: (0, 0)),
        pl.BlockSpec((tq, DIFF_VD), lambda b, h, i: (b * nq + i, h)),
        pl.BlockSpec((None, n_keys, DIFF_VD), lambda b, h, i: (b, kblk, h)),
        pl.BlockSpec((None, n_chunks, DIFF_VD, tk), lambda b, h, i: (b, kblk, h, 0)),
    ]
    args = [v.reshape(1, DIFF_HD) for v in lam_vecs] + [subln_g.reshape(1, DIFF_VD), q, k_all, vt_all]
    return pl.pallas_call(
        functools.partial(_attn_kernel, lam_init=lam_init),
        grid=(batch, DIFF_HEADS, nq),
        in_specs=in_specs,
        out_specs=pl.BlockSpec((tq, DIFF_VD), lambda b, h, i: (b * nq + i, h)),
        out_shape=jax.ShapeDtypeStruct((rows, V_W), F32),
        scratch_shapes=[pltpu.VMEM((DIFF_VD + ATTN_SUM_ROWS, cols), F32)]
        + [pltpu.VMEM(shape, dtype) for shape, dtype in slot_shapes for _ in range(grp)]
        + [pltpu.VMEM((grp * tk, cols), BF16), pltpu.VMEM((1, cols), F32)],
        compiler_params=_cparams(3),
        name="diff_attention" if n_chunks > 1 else "diff_attention_ctx",
    )(*args)


def _dft_cos_sin(n):
    idx = np.outer(np.arange(n), np.arange(n)) % n
    ang = 2.0 * np.pi * idx / n
    return np.cos(ang), np.sin(ang)


def _channel_dft():
    c, s = _dft_cos_sin(FOURIER_GC)
    eye = np.eye(LANES // FOURIER_GC)
    return jnp.asarray(np.kron(eye, c), F32).astype(BF16), jnp.asarray(np.kron(eye, s), F32).astype(BF16)


def _fourier_dense_kernel(x_ref, cl_ref, sl_ref, bdc_ref, bds_ref, o_ref, *, norm):
    x = x_ref[...]
    pr = _mm(cl_ref[...], x)
    pi = -_mm(sl_ref[...], x)
    out = _mm(pr, bdc_ref[...]) + _mm(pi, bds_ref[...])
    o_ref[...] = out * norm


def _fourier_fft_kernel(x_ref, f1_ref, twc_ref, tws_ref, f3_ref, bdc_ref, bds_ref, o_ref, ur_ref, ui_ref, xs_ref,
                        *, n1, norm):
    n2 = LANES
    pitch = n2 + FFT_PITCH_PAD
    nb = FFT_BATCH
    lanes = lambda j: slice(j * LANES, (j + 1) * LANES)

    xpitch = n1 + FFT_PITCH_PAD

    def repitch(b, carry):
        src = pl.multiple_of(b * n1, FFT_PITCH_PAD)
        dst = pl.multiple_of(b * xpitch, FFT_PITCH_PAD)
        xs_ref[pl.ds(dst, n1), :] = x_ref[pl.ds(src, n1), :]
        return carry

    lax.fori_loop(0, n2, repitch, 0, unroll=FFT_BATCH)

    def stage1(ib, carry):
        i0 = ib * nb
        xs = jnp.concatenate([xs_ref[pl.ds(i0 + j, n2, stride=xpitch), :] for j in range(nb)], axis=1)
        t = _mm(f1_ref[...], xs)
        for j in range(nb):
            tr, ti = t[:n2, lanes(j)], t[n2:, lanes(j)]
            r0 = pl.multiple_of((i0 + j) * n2, n2)
            c = twc_ref[pl.ds(r0, n2), :]
            s = tws_ref[pl.ds(r0, n2), :]
            p0 = pl.multiple_of((i0 + j) * pitch, FFT_PITCH_PAD)
            ur_ref[pl.ds(p0, n2), :] = tr * c + ti * s
            ui_ref[pl.ds(p0, n2), :] = ti * c - tr * s
        return carry

    lax.fori_loop(0, n1 // nb, stage1, 0)

    def stage2(kb, carry):
        k0 = kb * nb
        u = jnp.concatenate(
            [jnp.concatenate([ur_ref[pl.ds(k0 + j, n1, stride=pitch), :], ui_ref[pl.ds(k0 + j, n1, stride=pitch), :]],
                             axis=0) for j in range(nb)], axis=1)
        p = _mm(f3_ref[...], u)
        for j in range(nb):
            ur_ref[pl.ds(k0 + j, n1, stride=pitch), :] = p[:n1, lanes(j)]
            ui_ref[pl.ds(k0 + j, n1, stride=pitch), :] = p[n1:, lanes(j)]
        return carry

    lax.fori_loop(0, n2 // nb, stage2, 0)

    def stage3(kb, carry):
        k0 = kb * nb
        blocks = [pl.ds(pl.multiple_of((k0 + j) * pitch, FFT_PITCH_PAD), n2) for j in range(nb)]
        pr = jnp.concatenate([ur_ref[blk, :] for blk in blocks], axis=0)
        pi = jnp.concatenate([ui_ref[blk, :] for blk in blocks], axis=0)
        out = _mm(pr, bdc_ref[...]) + _mm(pi, bds_ref[...])
        o_ref[pl.ds(pl.multiple_of(k0 * n2, n2), nb * n2), :] = out * norm
        return carry

    lax.fori_loop(0, n1 // nb, stage3, 0)


def _fourier_mix(f2d, *, batch):
    rows = f2d.shape[0]
    n = rows // batch
    norm = 1.0 / math.sqrt(n * FOURIER_GC)
    bdc, bds = _channel_dft()
    blk = pl.BlockSpec((n, LANES), lambda b, j: (b, j))
    grid = (batch, FOURIER_W // LANES)
    out_shape = jax.ShapeDtypeStruct((rows, FOURIER_W), F32)
    mat = _const_spec((LANES, LANES))
    if n <= ROW_TILE:
        c, s = _dft_cos_sin(n)
        return pl.pallas_call(
            functools.partial(_fourier_dense_kernel, norm=norm),
            grid=grid,
            in_specs=[blk, _const_spec((n, n)), _const_spec((n, n)), mat, mat],
            out_specs=blk,
            out_shape=out_shape,
            compiler_params=_cparams(2),
            name="fourier_dense",
        )(f2d, jnp.asarray(c, F32).astype(BF16), jnp.asarray(s, F32).astype(BF16), bdc, bds)
    n2 = LANES
    n1 = n // n2
    c2, s2 = _dft_cos_sin(n2)
    f1 = jnp.asarray(np.concatenate([c2, -s2], axis=0), F32).astype(BF16)
    c1, s1 = _dft_cos_sin(n1)
    f3 = jnp.asarray(np.block([[c1, s1], [-s1, c1]]), F32).astype(BF16)
    tw_idx = np.outer(np.arange(n1), np.arange(n2)).reshape(-1)
    tw_ang = 2.0 * np.pi * tw_idx / n
    twc = jnp.broadcast_to(jnp.asarray(np.cos(tw_ang), F32)[:, None], (n, LANES))
    tws = jnp.broadcast_to(jnp.asarray(np.sin(tw_ang), F32)[:, None], (n, LANES))
    return pl.pallas_call(
        functools.partial(_fourier_fft_kernel, n1=n1, norm=norm),
        grid=grid,
        in_specs=[blk, _const_spec((2 * n2, n2)), _const_spec((n, LANES)), _const_spec((n, LANES)),
                  _const_spec((2 * n1, 2 * n1)), mat, mat],
        out_specs=blk,
        out_shape=out_shape,
        scratch_shapes=[pltpu.VMEM((n1 * (n2 + FFT_PITCH_PAD), LANES), F32)] * 2
        + [pltpu.VMEM((n2 * (n1 + FFT_PITCH_PAD), LANES), F32)],
        compiler_params=_cparams(2),
        name="fourier_fft",
    )(f2d, f1, twc, tws, f3, bdc, bds)


def _gelu_tanh(x):
    return 0.5 * x * (1.0 + jnp.tanh(math.sqrt(2.0 / math.pi) * (x + 0.044715 * (x * x * x))))


def _tail_kernel(*refs, glu, wa):
    if glu:
        x_ref, ya_ref, yb_ref, wg_ref, bg_ref, *rest = refs
    else:
        x_ref, ya_ref, yb_ref, *rest = refs
    (wo_ref, bo_ref, g1_ref, lg1_ref, lb1_ref, sh2_ref, sc2_ref, g2_ref,
     w1_ref, b1_ref, w2_ref, b2_ref, lg2_ref, lb2_ref, o_ref) = rest
    part = min(TAIL_PART_ROWS, x_ref.shape[0])
    n_parts = x_ref.shape[0] // part
    rows = lambda r: slice(r * part, (r + 1) * part)

    def mix_out(r):
        ya = ya_ref[rows(r), :]
        if glu:
            g = _gelu_tanh(ya)
            ya = g * jax.nn.sigmoid(jnp.dot(g.astype(BF16), wg_ref[...], preferred_element_type=F32) + bg_ref[...])
        y = (jnp.dot(ya.astype(BF16), wo_ref[:wa, :], preferred_element_type=F32)
             + jnp.dot(yb_ref[rows(r), :].astype(BF16), wo_ref[wa:, :], preferred_element_type=F32) + bo_ref[...])
        x1 = _ln(ALPHA * x_ref[rows(r), :] + g1_ref[...] * y) * lg1_ref[...] + lb1_ref[...]
        return x1, (_ln(x1) * (1.0 + sc2_ref[...]) + sh2_ref[...]).astype(BF16)

    def ffn_chunk(h, c):
        cols = slice(c * FFN_CHUNK, (c + 1) * FFN_CHUNK)
        a = jnp.maximum(jnp.dot(h, w1_ref[:, cols], preferred_element_type=F32) + b1_ref[:, cols], 0.0)
        return jnp.dot((a * a).astype(BF16), w2_ref[cols, :], preferred_element_type=F32)

    def finish(r, x1, y2):
        o_ref[rows(r), :] = _ln(ALPHA * x1 + g2_ref[...] * (y2 + b2_ref[...])) * lg2_ref[...] + lb2_ref[...]

    state = mix_out(0)
    done = None
    for r in range(n_parts):
        x1, h = state
        y2 = ffn_chunk(h, 0)
        if r + 1 < n_parts:
            state = mix_out(r + 1)
        if done is not None:
            finish(*done)
        for c in range(1, FFN_W // FFN_CHUNK):
            y2 = y2 + ffn_chunk(h, c)
        done = (r, x1, y2)
    finish(*done)


def _layer_tail(x2d, ya, yb, modl, glu_params, w_out, b_out, lg1, lb1, w1, b1, w2, b2, lg2, lb2, *, layer, tm,
                row_fn):
    rows = x2d.shape[0]
    wa, wb = ya.shape[-1], yb.shape[1]
    glu = glu_params is not None
    row = lambda v: v.reshape(1, -1)
    vec = _const_spec((1, D_MODEL))
    if ya.ndim == 3:
        tiles_per_batch = rows // ya.shape[0] // tm
        ya_spec = pl.BlockSpec((None, tm, wa), lambda i: (i // tiles_per_batch, i % tiles_per_batch, 0))
    else:
        ya_spec = pl.BlockSpec((tm, wa), lambda i: (i, 0))
    in_specs = [pl.BlockSpec((tm, D_MODEL), lambda i: (i, 0)), ya_spec,
                pl.BlockSpec((tm, wb), lambda i: (i, 0))]
    args = [x2d, ya, yb]
    if glu:
        in_specs += [_const_spec((wa, wa)), _const_spec((1, wa))]
        args += [glu_params[0], row(glu_params[1])]
    slab = lambda shape: pl.BlockSpec((None,) + shape, lambda i: (layer, 0, 0), pipeline_mode=pl.Buffered(1))
    in_specs += [slab((D_MODEL, D_MODEL)), vec, _mod_spec(row_fn, 2), vec, vec,
                 _mod_spec(row_fn, 3), _mod_spec(row_fn, 4), _mod_spec(row_fn, 5),
                 slab((D_MODEL, FFN_W)), _const_spec((1, FFN_W)), slab((FFN_W, D_MODEL)), vec, vec, vec]
    args += [w_out, row(b_out), modl, row(lg1), row(lb1), modl, modl, modl,
             w1, row(b1), w2, row(b2), row(lg2), row(lb2)]
    return pl.pallas_call(
        functools.partial(_tail_kernel, glu=glu, wa=wa),
        grid=(rows // tm,),
        in_specs=in_specs,
        out_specs=pl.BlockSpec((tm, D_MODEL), lambda i: (i, 0)),
        out_shape=jax.ShapeDtypeStruct((rows, D_MODEL), F32),
        compiler_params=_cparams(1),
        name="layer_tail_glu" if glu else "layer_tail",
    )(*args)


def _inproj_odd_kernel(x_ref, sh_ref, sc_ref, w_ref, wsp_ref, bsp_ref, s_ref, gm_ref):
    part = min(ODD_PART_ROWS, x_ref.shape[0])
    n_parts = x_ref.shape[0] // part
    n_ch = part // CHUNK
    rows = lambda r: slice(r * part, (r + 1) * part)
    norm = lambda r: (_ln(x_ref[rows(r), :]) * (1.0 + sc_ref[...]) + sh_ref[...]).astype(BF16)

    def gate(r, u, v):
        chunk = lambda ch: slice(ch * CHUNK, (ch + 1) * CHUNK)
        for g in range(GMLP_GROUPS):
            cols = slice(g * GMLP_GC, (g + 1) * GMLP_GC)
            vn = jnp.concatenate([_ln(v[chunk(ch), cols]).astype(BF16) for ch in range(n_ch)], axis=1)
            sp = jnp.dot(wsp_ref[g], vn, preferred_element_type=F32)
            for ch in range(n_ch):
                out_rows = slice(r * part + ch * CHUNK, r * part + (ch + 1) * CHUNK)
                gm_ref[out_rows, cols] = u[chunk(ch), cols] * (sp[:, ch * GMLP_GC:(ch + 1) * GMLP_GC] + bsp_ref[g])

    h_next = norm(0)
    pending = None
    for r in range(n_parts):
        h = h_next
        s_ref[rows(r), :] = jnp.dot(h, w_ref[:, :S5_W], preferred_element_type=F32)
        if pending is not None:
            gate(*pending)
        u = jnp.dot(h, w_ref[:, S5_W:S5_W + GMLP_W], preferred_element_type=F32)
        if r + 1 < n_parts:
            h_next = norm(r + 1)
        v = jnp.dot(h, w_ref[:, S5_W + GMLP_W:], preferred_element_type=F32)
        pending = (r, u, v)
    gate(*pending)


def _inproj_odd(x2d, modl, w_bf, wsp_bf, bsp_lanes, *, batch, n_rows, tm, row_fn):
    rows = x2d.shape[0]
    tiles_per_batch = rows // batch // tm
    return pl.pallas_call(
        _inproj_odd_kernel,
        grid=(rows // tm,),
        in_specs=[pl.BlockSpec((tm, D_MODEL), lambda i: (i, 0)), _mod_spec(row_fn, 0), _mod_spec(row_fn, 1),
                  _const_spec((D_MODEL, ODD_IN_W)), _const_spec((GMLP_GROUPS, CHUNK, CHUNK)),
                  _const_spec((GMLP_GROUPS, CHUNK, GMLP_GC))],
        out_specs=[pl.BlockSpec((None, tm, S5_W), lambda i: (i // tiles_per_batch, i % tiles_per_batch, 0)),
                   pl.BlockSpec((tm, GMLP_W), lambda i: (i, 0))],
        out_shape=[jax.ShapeDtypeStruct((batch, n_rows, S5_W), F32), jax.ShapeDtypeStruct((rows, GMLP_W), F32)],
        compiler_params=_cparams(1),
        name="inproj_odd_gmlp",
    )(x2d, modl, modl, w_bf, wsp_bf, bsp_lanes)


def _ln_mod_matmul_kernel(x_ref, sh_ref, sc_ref, w_ref, _, o_ref):
    h = (_ln(x_ref[...]) * (1.0 + sc_ref[...]) + sh_ref[...]).astype(BF16)
    o_ref[...] = jnp.dot(h, w_ref[...], preferred_element_type=F32)


def _ln_mod_matmul_into(x2d, modl, w_bf, dst, *, batch, row0, tm, row_fn):
    rows, n = x2d.shape[0], w_bf.shape[1]
    tiles_per_batch = rows // batch // tm
    blk0 = row0 // tm
    return pl.pallas_call(
        _ln_mod_matmul_kernel,
        grid=(rows // tm,),
        in_specs=[pl.BlockSpec((tm, D_MODEL), lambda i: (i, 0)), _mod_spec(row_fn, 0), _mod_spec(row_fn, 1),
                  _const_spec((D_MODEL, n)), pl.BlockSpec(memory_space=pl.ANY)],
        out_specs=pl.BlockSpec((None, tm, n), lambda i: (i // tiles_per_batch, blk0 + i % tiles_per_batch, 0)),
        out_shape=jax.ShapeDtypeStruct(dst.shape, dst.dtype),
        input_output_aliases={4: 0},
        compiler_params=_cparams(1),
        name="ln_mod_matmul",
    )(x2d, modl, modl, w_bf, dst)


def _s5_prep_kernel(pw_ref, bb_ref, cc_ref, dw_ref, w_ref, vt_ref, m_ref):
    t_len, gc = S5_T, S5_GC
    width = t_len * gc
    nt_dims = (((1,), (1,)), ((), ()))
    cols = lambda q: slice(q * LANES, (q + 1) * LANES)
    lane = lax.broadcasted_iota(jnp.int32, (gc, width), 1)
    for g in range(2):
        wide = []
        for d in range(2):
            pr, pi = pw_ref[g, d, 0], pw_ref[g, d, 1]
            bbr, bbi = bb_ref[g, d, 0], bb_ref[g, d, 1]
            cr, ci = cc_ref[g, d, 0], cc_ref[g, d, 1]

            def power(xr, xi, t, pr=pr, pi=pi):
                return pr[t:t + 1] * xr - pi[t:t + 1] * xi, pr[t:t + 1] * xi + pi[t:t + 1] * xr

            lc = [power(cr, ci, t) for t in range(t_len + 1)]
            for s in range(t_len):
                blk = slice(g * width + s * gc, g * width + (s + 1) * gc)
                lbr, lbi = power(bbr, bbi, t_len - 1 - s if d == 0 else s)
                w_ref[blk, cols(2 * d)] = lbr.astype(BF16)
                w_ref[blk, cols(2 * d + 1)] = lbi.astype(BF16)
                lcr, lci = lc[s + 1 if d == 0 else t_len - s]
                vt_ref[blk, cols(2 * d)] = lcr.astype(BF16)
                vt_ref[blk, cols(2 * d + 1)] = (-lci).astype(BF16)
            order = range(t_len) if d == 0 else range(t_len - 1, -1, -1)
            lcr_all = jnp.concatenate([lc[t][0] for t in order], axis=0)
            lci_all = jnp.concatenate([lc[t][1] for t in order], axis=0)
            wide.append(lax.dot_general(bbr, lcr_all, nt_dims, preferred_element_type=F32, precision=HIGHEST)
                        - lax.dot_general(bbi, lci_all, nt_dims, preferred_element_type=F32, precision=HIGHEST))
        kf = wide[0] + dw_ref[g]
        kb = wide[1]
        for s in range(t_len):
            fwd = jnp.where(lane >= s * gc, pltpu.roll(kf, s * gc, 1), 0.0)
            bwd = jnp.where(lane < (s + 1) * gc, pltpu.roll(kb, (s + 1) * gc % width, 1), 0.0)
            m_ref[g, s * gc:(s + 1) * gc, :] = (fwd + bwd).astype(BF16)


def _s5_matrices(lam_re, lam_im, log_dt, b_re, b_im, c_re, c_im, d_skip):
    t_len, g_n, p_n, i_n = S5_T, S5_GROUPS, S5_STATE, S5_GC
    tau_rows = -(-(t_len + 1) // 8) * 8
    dt = jnp.exp(log_dt)[..., None]
    mag, th = lam_re * dt, lam_im * dt
    er = jnp.exp(mag)
    lbr, lbi = er * jnp.cos(th), er * jnp.sin(th)
    den = lam_re * lam_re + lam_im * lam_im
    nr, ni = lbr - 1.0, lbi
    cr = (nr * lam_re + ni * lam_im) / den
    ci = (ni * lam_re - nr * lam_im) / den
    tau = jnp.arange(tau_rows, dtype=F32)[:, None]
    pm = jnp.exp(tau * mag[:, :, None, :])
    ang = tau * th[:, :, None, :]
    pw = jnp.stack([pm * jnp.cos(ang), pm * jnp.sin(ang)], axis=2)
    to_ip = lambda a: jnp.swapaxes(a, -1, -2)
    bb = jnp.stack([cr[..., None, :] * to_ip(b_re) - ci[..., None, :] * to_ip(b_im),
                    cr[..., None, :] * to_ip(b_im) + ci[..., None, :] * to_ip(b_re)], axis=2)
    cc = jnp.stack([c_re, c_im], axis=2)

    def per_group(a):
        a = jnp.swapaxes(a, 0, 1)
        a = a.reshape((g_n // 2, 2) + a.shape[1:])
        zero = jnp.zeros_like(a[:, 0])
        both = jnp.stack([jnp.concatenate([a[:, 0], zero], axis=-1), jnp.concatenate([zero, a[:, 1]], axis=-1)], axis=1)
        return both.reshape((g_n,) + both.shape[2:])

    width = t_len * i_n
    d_wide = jnp.pad(d_skip.reshape(g_n, i_n)[:, None, :] * jnp.asarray(np.eye(i_n), F32),
                     ((0, 0), (0, 0), (0, width - i_n)))
    pair = lambda shape: pl.BlockSpec((2,) + shape, lambda j: (j,) + (0,) * len(shape))
    state_spec = pl.BlockSpec((None, 2 * width, 4 * LANES), lambda j: (j, 0, 0))
    state_shape = jax.ShapeDtypeStruct((S5_PAIRS, 2 * width, 4 * LANES), BF16)
    w_state, vt_state, m_pair = pl.pallas_call(
        _s5_prep_kernel,
        grid=(S5_PAIRS,),
        in_specs=[pair((2, 2, tau_rows, 2 * p_n)), pair((2, 2, i_n, 2 * p_n)), pair((2, 2, i_n, 2 * p_n)),
                  pair((i_n, width))],
        out_specs=[state_spec, state_spec, pl.BlockSpec((None, 2, width, width), lambda j: (j, 0, 0, 0))],
        out_shape=[state_shape, state_shape, jax.ShapeDtypeStruct((S5_PAIRS, 2, width, width), BF16)],
        compiler_params=_cparams(1),
        name="s5_prep",
    )(per_group(pw), per_group(bb), per_group(cc), d_wide)
    a_pow = jnp.stack([pw[0, :, 0, t_len], pw[0, :, 1, t_len], pw[1, :, 0, t_len], pw[1, :, 1, t_len]], axis=0)
    return m_pair, w_state, vt_state, a_pow.reshape(4, g_n * p_n)


def _s5_fold_perm():
    n_t, n_g = LANES // S5_GC, LANES // S5_GC
    src = np.arange(n_t * LANES)
    t_lo, g, i = src // LANES, (src % LANES) // S5_GC, src % S5_GC
    perm = np.zeros((n_t * LANES, n_g * LANES), np.float32)
    perm[src, g * LANES + t_lo * S5_GC + i] = 1.0
    return perm


def _s5_kernel(s_ref, pb_ref, pbt_ref, m_ref, w_ref, v_ref, a_ref, y_ref, u_scr, st_scr, yp_scr, *, n_lat, n_ctx):
    n_all = n_lat + n_ctx
    n_t = LANES // S5_GC
    width = S5_T * S5_GC
    blk = lambda j: slice(j * LANES, (j + 1) * LANES)
    for t_hi in range(S5_T // n_t):
        x = jnp.concatenate([s_ref[pl.ds(n_t * t_hi + t_lo, n_all, stride=S5_T), :].astype(BF16)
                             for t_lo in range(n_t)], axis=1)
        xp = jnp.dot(x, pb_ref[...], preferred_element_type=F32).astype(BF16)
        for g in range(n_t):
            u_scr[g, :, blk(t_hi)] = xp[:, blk(g)]

    n_pairs = w_ref.shape[0]
    for pr in range(n_pairs):
        u_pair = jnp.concatenate([u_scr[2 * pr], u_scr[2 * pr + 1]], axis=1)
        st = jnp.dot(u_pair, w_ref[pr], preferred_element_type=F32)
        for q in range(4):
            st_scr[q, :, blk(pr)] = st[:, blk(q)]

    afr, afi, abr, abi = (a_ref[pl.ds(q, 1), :] for q in range(4))

    def scan_step(k, carry):
        hr, hi, gr, gi = carry
        rf = pl.ds(jnp.where(k < n_ctx, k + n_lat, k - n_ctx), 1)
        rb = pl.ds(n_all - 1 - k, 1)
        sfr, sfi, sbr, sbi = st_scr[0, rf, :], st_scr[1, rf, :], st_scr[2, rb, :], st_scr[3, rb, :]
        st_scr[0, rf, :] = hr
        st_scr[1, rf, :] = hi
        st_scr[2, rb, :] = gr
        st_scr[3, rb, :] = gi
        return (afr * hr - afi * hi + sfr, afr * hi + afi * hr + sfi,
                abr * gr - abi * gi + sbr, abr * gi + abi * gr + sbi)

    lax.fori_loop(0, n_all, scan_step, (jnp.zeros_like(afr),) * 4, unroll=S5_SCAN_UNROLL)

    for pr in range(n_pairs):
        hst = jnp.concatenate([st_scr[q, :, blk(pr)].astype(BF16) for q in range(4)], axis=1)
        y_pair = lax.dot_general(hst, v_ref[pr], (((1,), (1,)), ((), ())), preferred_element_type=F32)
        for g in range(2):
            y_g = (y_pair[:, g * width:(g + 1) * width]
                   + jnp.dot(u_scr[2 * pr + g], m_ref[pr, g], preferred_element_type=F32))
            for t_hi in range(S5_T // n_t):
                yp_scr[t_hi, :, blk(2 * pr + g)] = y_g[:, blk(t_hi)].astype(BF16)

    for t_hi in range(S5_T // n_t):
        z = jnp.dot(yp_scr[t_hi], pbt_ref[...], preferred_element_type=F32)
        for t_lo in range(n_t):
            y_ref[pl.ds(n_t * t_hi + t_lo, n_all, stride=S5_T), :] = z[:, blk(t_lo)]


def _s5_mix(s_all, mats, *, n_lat_rows):
    m_pair, w_state, v_state, a_pow = mats
    batch, n_rows, _ = s_all.shape
    n_lat = n_lat_rows // S5_T
    n_all = n_rows // S5_T
    width = S5_T * S5_GC
    n_blocks = S5_W // LANES
    pairs = S5_PAIRS // n_blocks
    perm = _s5_fold_perm()
    rows_spec = pl.BlockSpec((None, n_rows, LANES), lambda v, b: (b, 0, v))
    return pl.pallas_call(
        functools.partial(_s5_kernel, n_lat=n_lat, n_ctx=n_all - n_lat),
        grid=(n_blocks, batch),
        in_specs=[rows_spec, _const_spec(perm.shape), _const_spec(perm.shape),
                  pl.BlockSpec((pairs, 2, width, width), lambda v, b: (v, 0, 0, 0)),
                  pl.BlockSpec((pairs, 2 * width, 4 * LANES), lambda v, b: (v, 0, 0)),
                  pl.BlockSpec((pairs, 4 * LANES, 2 * width), lambda v, b: (v, 0, 0)),
                  pl.BlockSpec((4, pairs * LANES), lambda v, b: (0, v))],
        out_specs=rows_spec,
        out_shape=jax.ShapeDtypeStruct(s_all.shape, F32),
        scratch_shapes=[pltpu.VMEM((2 * pairs, n_all, width), BF16), pltpu.VMEM((4, n_all, pairs * LANES), F32),
                        pltpu.VMEM((S5_T * S5_GC // LANES, n_all, 2 * pairs * LANES), BF16)],
        compiler_params=_cparams(2),
        name="s5_mix",
    )(s_all, jnp.asarray(perm, BF16), jnp.asarray(perm.T, BF16), m_pair, w_state, v_state, a_pow)


def kernel(x, c, ctx, c_ctx, w_mod, b_mod, w_out, b_out, ln_mix_g, ln_mix_b, w_ffn1, b_ffn1, w_ffn2,
           b_ffn2, ln_ffn_g, ln_ffn_b, w_in_ab, lam_q1, lam_k1, lam_q2, lam_k2, subln_g, w_in_cd,
           s5_lam_re, s5_lam_im, s5_log_dt, s5_b_re, s5_b_im, s5_c_re, s5_c_im, s5_d, w_glu, b_glu,
           w_sp, b_sp):
    batch, n_lat, d = x.shape
    n_ctx = ctx.shape[1]
    assert d == D_MODEL and n_lat % TAIL_TILE == 0 and n_ctx % CHUNK == 0 and batch + 1 <= MOD_ROWS
    mod = _modulation(c, c_ctx, w_mod, b_mod)
    xl = x.reshape(batch * n_lat, d)
    xc = ctx.reshape(batch * n_ctx, d)
    tiles = n_lat // ROW_TILE
    lat_rows = dict(tm=ROW_TILE, row_fn=lambda i: i // tiles)
    tail_tiles = n_lat // TAIL_TILE
    lat_tail_rows = dict(tm=TAIL_TILE, row_fn=lambda i: i // tail_tiles)
    ctx_rows = dict(tm=n_ctx, row_fn=lambda i: batch)
    rope_tabs = _rope_tables(n_lat)
    w_out_bf, w_ffn1_bf, w_ffn2_bf = w_out.astype(BF16), w_ffn1.astype(BF16), w_ffn2.astype(BF16)
    for l in range(DEPTH):
        last = l == DEPTH - 1
        e = l // 2
        modl = mod[l]
        tail_w = (w_out_bf, b_out[l], ln_mix_g[l], ln_mix_b[l], w_ffn1_bf, b_ffn1[l],
                  w_ffn2_bf, b_ffn2[l], ln_ffn_g[l], ln_ffn_b[l])
        if l % 2 == 0:
            w_in = w_in_ab[e].astype(BF16)
            lam_init = 0.8 - 0.6 * math.exp(-0.3 * l)
            lam_vecs = (lam_q1[e], lam_k1[e], lam_q2[e], lam_k2[e])
            n_keys = n_lat + n_ctx
            f, q, k_all, vt_all = _inproj_even(xl, modl, w_in, rope_tabs, None, batch=batch, n_keys=n_keys,
                                               key_row0=0, **lat_rows)
            fc, qc, k_all, vt_all = _inproj_even(xc, modl, w_in, None, (k_all, vt_all), batch=batch, n_keys=n_keys,
                                                 key_row0=n_lat, **ctx_rows)
            attend = functools.partial(_diff_attention, lam_vecs=lam_vecs, subln_g=subln_g[e], lam_init=lam_init,
                                       batch=batch)
            ya = attend(q, k_all, vt_all, tq=min(ATTN_TQ, n_lat), key_row0=0, n_keys=n_keys)
            yf = _fourier_mix(f, batch=batch)
            xl = _layer_tail(xl, yf, ya, modl, None, *tail_w, layer=l, **lat_tail_rows)
            if not last:
                yac = attend(qc, k_all, vt_all, tq=n_ctx, key_row0=n_lat, n_keys=n_ctx)
                yfc = _fourier_mix(fc, batch=batch)
                xc = _layer_tail(xc, yfc, yac, modl, None, *tail_w, layer=l, tm=batch * n_ctx, row_fn=lambda i: batch)
        else:
            assert last, "the S5 / gMLP layer is only implemented as the final layer"
            w_in = w_in_cd[e].astype(BF16)
            bsp_lanes = jnp.broadcast_to(b_sp[e][:, :, None], (GMLP_GROUPS, CHUNK, GMLP_GC))
            s_all, gm = _inproj_odd(xl, modl, w_in, w_sp[e].astype(BF16), bsp_lanes, batch=batch,
                                    n_rows=n_lat + n_ctx, **lat_rows)
            s_all = _ln_mod_matmul_into(xc, modl, w_in[:, :S5_W], s_all, batch=batch, row0=n_lat, **ctx_rows)
            mats = _s5_matrices(s5_lam_re[e], s5_lam_im[e], s5_log_dt[e], s5_b_re[e], s5_b_im[e], s5_c_re[e],
                                s5_c_im[e], s5_d[e])
            ys = _s5_mix(s_all, mats, n_lat_rows=n_lat)
            xl = _layer_tail(xl, ys, gm, modl, (w_glu[e].astype(BF16), b_glu[e]), *tail_w, layer=l, **lat_tail_rows)
    return xl.reshape(batch, n_lat, d)
```
